```python
import math
import jax, jax.numpy as jnp
from jax import lax
import numpy as np

D_MODEL = 1024
BATCH = 8
SEQ = 2048
DEPTH = 2

CHUNK = 64
Q_BLOCK = 128
PLE_DIM = 256

A_HEADS = 6
A_DIM = 32
A_VDIM = 2 * A_DIM
B_HEADS = 6
B_Q_RANK = 256
B_KV_RANK = 128
B_NOPE = 64
B_ROPE = 32
B_VDIM = 64
ROPE_THETA = 10000.0
C_HEADS = 4
C_DIM = 64

MIX_A = A_HEADS * A_VDIM
MIX_B = B_HEADS * B_VDIM
MIX_C = C_HEADS * C_DIM
D_MIX = MIX_A + MIX_B + MIX_C
IN_SPLITS = (A_HEADS * 2 * A_DIM, A_HEADS * 2 * A_DIM, MIX_A,
             B_Q_RANK, B_KV_RANK, B_ROPE,
             MIX_C, MIX_C, MIX_C, C_HEADS)
D_IN = sum(IN_SPLITS)

N_GROUPS = 4
EXPERTS_PER_GROUP = 8
N_EXPERTS = N_GROUPS * EXPERTS_PER_GROUP
TOP_K_IN_GROUP = 2
D_EXPERT = 256

DEEPNORM_ALPHA = (2 * DEPTH) ** 0.25
DEEPNORM_BETA = (8 * DEPTH) ** -0.25
LN_EPS = 1e-5
RMS_EPS = 1e-6

kernel_name = "hymba_style_diff_mla_fox_hiermoe_deepnorm"


def _layer_norm(x, g, b):
    xf = x.astype(jnp.float32)
    mu = jnp.mean(xf, axis=-1, keepdims=True)
    var = jnp.mean(jnp.square(xf - mu), axis=-1, keepdims=True)
    y = (xf - mu) * lax.rsqrt(var + LN_EPS) * g.astype(jnp.float32) + b.astype(jnp.float32)
    return y.astype(x.dtype)


def _rms_norm(x, g):
    xf = x.astype(jnp.float32)
    y = xf * lax.rsqrt(jnp.mean(jnp.square(xf), axis=-1, keepdims=True) + RMS_EPS)
    return (y * g.astype(jnp.float32)).astype(x.dtype)


def _rope_tables(pos):
    half = B_ROPE // 2
    inv = ROPE_THETA ** (-jnp.arange(half, dtype=jnp.float32) / half)
    ang = pos.astype(jnp.float32)[..., None] * inv
    return jnp.cos(ang), jnp.sin(ang)


def _rope(x, cos, sin):
    half = x.shape[-1] // 2
    x1, x2 = x[..., :half], x[..., half:]
    c, s = cos.astype(x.dtype), sin.astype(x.dtype)
    return jnp.concatenate([x1 * c - x2 * s, x2 * c + x1 * s], axis=-1)


def _alibi_slopes(n):
    return 2.0 ** (-8.0 * jnp.arange(1, n + 1, dtype=jnp.float32) / n)


def _to_heads(t, n_heads):
    b, s, _ = t.shape
    return t.reshape(b, s, n_heads, -1).transpose(0, 2, 1, 3)


def _from_heads(t):
    b, h, s, d = t.shape
    return t.transpose(0, 2, 1, 3).reshape(b, s, h * d)


def _block_attention(q, k, v, scale, bias_fn, per_frame_causal):
    seq = q.shape[2]
    outs = []
    for t0 in range(0, seq, Q_BLOCK):
        t1 = t0 + Q_BLOCK
        logits = jnp.einsum('bhqd,bhkd->bhqk', q[:, :, t0:t1], k[:, :, :t1],
                            preferred_element_type=jnp.float32) * scale
        if bias_fn is not None:
            logits = logits + bias_fn(t0, t1)
        t_idx = jnp.arange(t0, t1)[:, None]
        s_idx = jnp.arange(t1)[None, :]
        allowed = (s_idx <= t_idx) if per_frame_causal else (s_idx // CHUNK <= t_idx // CHUNK)
        logits = jnp.where(allowed, logits, -jnp.inf)
        probs = jax.nn.softmax(logits, axis=-1).astype(v.dtype)
        outs.append(jnp.einsum('bhqk,bhkd->bhqd', probs, v[:, :, :t1]))
    return jnp.concatenate(outs, axis=2)


def _token_mixers(x, layer, pos, cos, sin, w_in, w_uq, w_ukv, g_cq, g_ckv,
                  lam_q1, lam_k1, lam_q2, lam_k2, g_diff, b_forget, w_out):
    b, s, _ = x.shape
    z = x @ w_in
    qa, ka, va, cq, ckv, kr, qc, kc, vc, fc = jnp.split(
        z, np.cumsum(IN_SPLITS)[:-1].tolist(), axis=-1)

    qa = qa.reshape(b, s, A_HEADS, 2, A_DIM)
    ka = ka.reshape(b, s, A_HEADS, 2, A_DIM)
    qa1, qa2 = qa[:, :, :, 0].transpose(0, 2, 1, 3), qa[:, :, :, 1].transpose(0, 2, 1, 3)
    ka1, ka2 = ka[:, :, :, 0].transpose(0, 2, 1, 3), ka[:, :, :, 1].transpose(0, 2, 1, 3)
    v_a = _to_heads(va, A_HEADS)
    slopes = _alibi_slopes(A_HEADS)
    posf = pos.astype(jnp.float32)

    def alibi(t0, t1):
        dist = jnp.abs(posf[:, None, t0:t1, None] - posf[:, None, None, :t1])
        return -slopes[None, :, None, None] * dist

    scale_a = 1.0 / math.sqrt(A_DIM)
    o_a1 = _block_attention(qa1, ka1, v_a, scale_a, alibi, False)
    o_a2 = _block_attention(qa2, ka2, v_a, scale_a, alibi, False)
    lam_init = 0.8 - 0.6 * math.exp(-0.3 * layer)
    lam = (jnp.exp(jnp.sum(lam_q1.astype(jnp.float32) * lam_k1.astype(jnp.float32)))
           - jnp.exp(jnp.sum(lam_q2.astype(jnp.float32) * lam_k2.astype(jnp.float32))) + lam_init)
    o_a = o_a1 - lam.astype(o_a1.dtype) * o_a2
    o_a = _rms_norm(o_a, g_diff) * (1.0 - lam_init)

    c_q = _rms_norm(cq, g_cq)
    q_b = (c_q @ w_uq).reshape(b, s, B_HEADS, B_NOPE + B_ROPE)
    q_rope = _rope(q_b[..., B_NOPE:], cos[:, :, None], sin[:, :, None])
    q_b = jnp.concatenate([q_b[..., :B_NOPE], q_rope], axis=-1)
    c_kv = _rms_norm(ckv, g_ckv)
    kv_b = (c_kv @ w_ukv).reshape(b, s, B_HEADS, B_NOPE + B_VDIM)
    k_nope, v_b = kv_b[..., :B_NOPE], kv_b[..., B_NOPE:]
    k_rope = _rope(kr, cos, sin)
    k_b = jnp.concatenate(
        [k_nope, jnp.broadcast_to(k_rope[:, :, None], (b, s, B_HEADS, B_ROPE))], axis=-1)
    o_b = _block_attention(q_b.transpose(0, 2, 1, 3), k_b.transpose(0, 2, 1, 3),
                           v_b.transpose(0, 2, 1, 3), 1.0 / math.sqrt(B_NOPE + B_ROPE),
                           None, False)

    log_f = jax.nn.log_sigmoid(fc.astype(jnp.float32) + b_forget.astype(jnp.float32))
    cum = jnp.cumsum(log_f, axis=1).transpose(0, 2, 1)

    def decay(t0, t1):
        return cum[:, :, t0:t1, None] - cum[:, :, None, :t1]

    o_c = _block_attention(_to_heads(qc, C_HEADS), _to_heads(kc, C_HEADS),
                           _to_heads(vc, C_HEADS), 1.0 / math.sqrt(C_DIM), decay, True)

    mixed = jnp.concatenate([_from_heads(o_a), _from_heads(o_b), _from_heads(o_c)], axis=-1)
    return mixed @ w_out


def _hier_moe(h, w_group, b_group, w_erouter, b_erouter, w_gate_e, w_up_e, w_down_e):
    b, s, d = h.shape
    xt = h.reshape(b * s, d)
    g_logits = (xt @ w_group).astype(jnp.float32) + b_group.astype(jnp.float32)
    g_prob = jax.nn.softmax(g_logits, axis=-1)
    _, g_idx = lax.top_k(g_logits, 1)
    g_w = jnp.take_along_axis(g_prob, g_idx, axis=-1)
    e_all = jnp.einsum('nd,gde->nge', xt, w_erouter).astype(jnp.float32) + b_erouter.astype(jnp.float32)
    e_logits = jnp.einsum('ng,nge->ne', jax.nn.one_hot(g_idx[:, 0], N_GROUPS, dtype=jnp.float32), e_all)
    top_val, top_idx = lax.top_k(e_logits, TOP_K_IN_GROUP)
    e_w = jax.nn.softmax(top_val, axis=-1) * g_w
    expert_id = g_idx * EXPERTS_PER_GROUP + top_idx
    combine = jnp.sum(jax.nn.one_hot(expert_id, N_EXPERTS, dtype=jnp.float32) * e_w[..., None], axis=1)
    gate = jnp.einsum('nd,edf->nef', xt, w_gate_e)
    up = jnp.einsum('nd,edf->nef', xt, w_up_e)
    act = jax.nn.silu(gate) * up * combine[:, :, None].astype(xt.dtype)
    y = jnp.einsum('nef,efd->nd', act, w_down_e)
    return y.reshape(b, s, d)


def setup_inputs(seed: int = 0) -> dict:
    key = jax.random.key(seed)
    ks = iter(jax.random.split(key, 32))

    def nrm(shape, scale):
        return jax.random.normal(next(ks), shape, jnp.float32) * scale

    def gain(shape):
        return 1.0 + nrm(shape, 0.02)

    x = nrm((BATCH, SEQ, D_MODEL), 1.0)
    p = nrm((DEPTH, BATCH, SEQ, PLE_DIM), 1.0)
    offset = jax.random.randint(next(ks), (BATCH,), 0, 64, jnp.int32) * CHUNK
    positions = offset[:, None] + jnp.arange(SEQ, dtype=jnp.int32)[None, :]
    return {
        "x": x,
        "p": p,
        "positions": positions,
        "w_in": nrm((DEPTH, D_MODEL, D_IN), D_MODEL ** -0.5),
        "w_uq": nrm((DEPTH, B_Q_RANK, B_HEADS * (B_NOPE + B_ROPE)), B_Q_RANK ** -0.5),
        "w_ukv": nrm((DEPTH, B_KV_RANK, B_HEADS * (B_NOPE + B_VDIM)), B_KV_RANK ** -0.5),
        "g_cq": gain((DEPTH, B_Q_RANK)),
        "g_ckv": gain((DEPTH, B_KV_RANK)),
        "lam_q1": nrm((DEPTH, A_DIM), 0.1),
        "lam_k1": nrm((DEPTH, A_DIM), 0.1),
        "lam_q2": nrm((DEPTH, A_DIM), 0.1),
        "lam_k2": nrm((DEPTH, A_DIM), 0.1),
        "g_diff": gain((DEPTH, A_VDIM)),
        "b_forget": 2.0 + nrm((DEPTH, C_HEADS), 0.1),
        "w_out": nrm((DEPTH, D_MIX, D_MODEL), D_MIX ** -0.5 * DEEPNORM_BETA),
        "ln1_g": gain((DEPTH, D_MODEL)),
        "ln1_b": nrm((DEPTH, D_MODEL), 0.02),
        "w_group": nrm((DEPTH, D_MODEL, N_GROUPS), D_MODEL ** -0.5),
        "b_group": nrm((DEPTH, N_GROUPS), 0.01),
        "w_erouter": nrm((DEPTH, N_GROUPS, D_MODEL, EXPERTS_PER_GROUP), D_MODEL ** -0.5),
        "b_erouter": nrm((DEPTH, N_GROUPS, EXPERTS_PER_GROUP), 0.01),
        "w_gate_e": nrm((DEPTH, N_EXPERTS, D_MODEL, D_EXPERT), D_MODEL ** -0.5),
        "w_up_e": nrm((DEPTH, N_EXPERTS, D_MODEL, D_EXPERT), D_MODEL ** -0.5),
        "w_down_e": nrm((DEPTH, N_EXPERTS, D_EXPERT, D_MODEL), D_EXPERT ** -0.5 * DEEPNORM_BETA),
        "w_ple_gate": nrm((DEPTH, D_MODEL, D_MODEL), D_MODEL ** -0.5),
        "w_ple_proj": nrm((DEPTH, PLE_DIM, D_MODEL), PLE_DIM ** -0.5 * DEEPNORM_BETA),
        "ln2_g": gain((DEPTH, D_MODEL)),
        "ln2_b": nrm((DEPTH, D_MODEL), 0.02),
    }


def reference(x, p, positions, w_in, w_uq, w_ukv, g_cq, g_ckv, lam_q1, lam_k1, lam_q2, lam_k2,
              g_diff, b_forget, w_out, ln1_g, ln1_b, w_group, b_group, w_erouter, b_erouter,
              w_gate_e, w_up_e, w_down_e, w_ple_gate, w_ple_proj, ln2_g, ln2_b):
    cos, sin = _rope_tables(positions)
    for i in range(DEPTH):
        mix = _token_mixers(x, i, positions, cos, sin, w_in[i], w_uq[i], w_ukv[i], g_cq[i],
                            g_ckv[i], lam_q1[i], lam_k1[i], lam_q2[i], lam_k2[i], g_diff[i],
                            b_forget[i], w_out[i])
        h = _layer_norm(DEEPNORM_ALPHA * x + mix, ln1_g[i], ln1_b[i])
        moe = _hier_moe(h, w_group[i], b_group[i], w_erouter[i], b_erouter[i],
                        w_gate_e[i], w_up_e[i], w_down_e[i])
        ple = jax.nn.sigmoid(h @ w_ple_gate[i]) * (p[i].astype(h.dtype) @ w_ple_proj[i])
        x = _layer_norm(DEEPNORM_ALPHA * h + moe + ple, ln2_g[i], ln2_b[i])
    return x
```

```python
import functools
import math

import jax
import jax.numpy as jnp
from jax import lax
from jax.experimental import pallas as pl
from jax.experimental.pallas import tpu as pltpu

F32 = jnp.float32
BF16 = jnp.bfloat16

D_MODEL = 1024
DEPTH = 2
CHUNK = 64
PLE_DIM = 256

A_HEADS = 6
A_DIM = 32
A_VDIM = 64
B_HEADS = 6
B_Q_RANK = 256
B_KV_RANK = 128
B_NOPE = 64
B_ROPE = 32
B_VDIM = 64
ROPE_THETA = 10000.0
C_HEADS = 4
C_DIM = 64

MIX_A = A_HEADS * A_VDIM
MIX_B = B_HEADS * B_VDIM
MIX_C = C_HEADS * C_DIM
B_PAD = 128

N_GROUPS = 4
EXPERTS_PER_GROUP = 8
N_EXPERTS = N_GROUPS * EXPERTS_PER_GROUP
D_EXPERT = 256
ROUTER_COLS = 128

DEEPNORM_ALPHA = (2 * DEPTH) ** 0.25
LN_EPS = 1e-5
RMS_EPS = 1e-6
NEG_BIG = -1e30

_OFF = {}
_o = 0
for _name, _w in (("qa", 384), ("ka", 384), ("va", 384), ("cq", 256), ("ckv", 128), ("kr", 32),
                  ("qc", 256), ("kc", 256), ("vc", 256), ("fc", 4)):
    _OFF[_name] = (_o, _o + _w)
    _o += _w

T_KA, T_KC, T_KR, T_KRS, T_FC, T_ROWS = 0, 384, 640, 672, 704, 720
FC_ROWS = T_ROWS - T_FC

VMEM_LIMIT = 48 * 1024 * 1024

NT_DIMS = (((1,), (1,)), ((), ()))


def _dot(a, b):
    return jnp.dot(a, b, preferred_element_type=F32)


def _dot_nt(a, b):
    return lax.dot_general(a, b, NT_DIMS, preferred_element_type=F32)


def _split3(a):
    a1 = a.astype(BF16)
    r1 = a - a1.astype(F32)
    a2 = r1.astype(BF16)
    a3 = (r1 - a2.astype(F32)).astype(BF16)
    return a1, a2, a3


def _prep_kernel(x_ref, wmain_ref, wt_ref, wuq_ref, wukvk_ref, wukvv_ref, gcq_ref, gckv_ref,
                 bf_ref, cos_ref, sin_ref, cost_ref, sint_ref,
                 qa_ref, qb_ref, qc_ref, kta_ref, ktb_ref, ktc_ref, va_ref, vb_ref, vc_ref,
                 cum_ref, carry_ref, *, tm):
    j = pl.program_id(1)

    @pl.when(j == 0)
    def _():
        carry_ref[...] = jnp.zeros_like(carry_ref)

    xb = x_ref[0].astype(BF16)
    z = _dot(xb, wmain_ref[...])
    zt = _dot_nt(wt_ref[...], xb)

    qa_ref[0] = (z[:, 0:384] * (1.0 / math.sqrt(A_DIM))).astype(BF16)
    va_ref[0] = z[:, 384:768].astype(BF16)
    kta_ref[0] = zt[T_KA:T_KA + 384].astype(BF16)

    qc_ref[0] = (z[:, 1152:1408] * (1.0 / math.sqrt(C_DIM))).astype(BF16)
    vc_ref[0] = z[:, 1408:1664].astype(BF16)
    ktc_ref[0] = zt[T_KC:T_KC + 256].astype(BF16)

    fct = zt[T_FC:T_ROWS] + bf_ref[...]
    logf = jnp.minimum(fct, 0.0) - jnp.log(1.0 + jnp.exp(-jnp.abs(fct)))
    r_i = lax.broadcasted_iota(jnp.int32, (tm, tm), 0)
    c_i = lax.broadcasted_iota(jnp.int32, (tm, tm), 1)
    tri = jnp.where(r_i <= c_i, 1.0, 0.0).astype(BF16)
    l1, l2, l3 = _split3(logf)
    cum = _dot(l1, tri) + _dot(l2, tri) + _dot(l3, tri) + carry_ref[:, 0:1]
    cum_ref[0] = cum
    carry_ref[...] = jnp.broadcast_to(cum[:, tm - 1:tm], carry_ref.shape)

    cq = z[:, 768:1024]
    cq_n = cq * lax.rsqrt(jnp.mean(cq * cq, axis=-1, keepdims=True) + RMS_EPS) * gcq_ref[...]
    zq = _dot(cq_n.astype(BF16), wuq_ref[...])
    cosb = cos_ref[0]
    sinb = sin_ref[0]
    scale_b = 1.0 / math.sqrt(B_NOPE + B_ROPE)
    for h in range(B_HEADS):
        lo = h * B_PAD
        qh = zq[:, lo:lo + B_PAD] * cosb + zq[:, 768 + lo:768 + lo + B_PAD] * sinb
        qb_ref[0, :, lo:lo + B_PAD] = (qh * scale_b).astype(BF16)

    ckv = z[:, 1024:1152]
    ckv_n = ckv * lax.rsqrt(jnp.mean(ckv * ckv, axis=-1, keepdims=True) + RMS_EPS) * gckv_ref[...]
    ckv_b = ckv_n.astype(BF16)
    vb_ref[0] = _dot(ckv_b, wukvv_ref[...]).astype(BF16)
    knt = _dot_nt(wukvk_ref[...], ckv_b)
    krt = (zt[T_KR:T_KR + 32] * cost_ref[0] + zt[T_KRS:T_KRS + 32] * sint_ref[0]).astype(BF16)
    zpad = jnp.zeros((B_PAD - B_NOPE - B_ROPE, tm), BF16)
    for h in range(B_HEADS):
        lo = h * B_PAD
        ktb_ref[0, lo:lo + B_NOPE, :] = knt[h * B_NOPE:(h + 1) * B_NOPE].astype(BF16)
        ktb_ref[0, lo + B_NOPE:lo + B_NOPE + B_ROPE, :] = krt
        ktb_ref[0, lo + B_NOPE + B_ROPE:lo + B_PAD, :] = zpad


def _prep_call(x, wmain, wt, wuq, wukvk, wukvv, gcq, gckv, bfc, cos128, sin128, cost, sint, *, tm):
    b, s, d = x.shape
    grid = (b, s // tm)
    full = lambda shape: pl.BlockSpec(shape, lambda bi, j: (0,) * len(shape))
    tok = lambda w: pl.BlockSpec((1, tm, w), lambda bi, j: (bi, j, 0))
    tr = lambda r: pl.BlockSpec((1, r, tm), lambda bi, j: (bi, 0, j))
    out_shape = (
        jax.ShapeDtypeStruct((b, s, 384), BF16),
        jax.ShapeDtypeStruct((b, s, 768), BF16),
        jax.ShapeDtypeStruct((b, s, 256), BF16),
        jax.ShapeDtypeStruct((b, 384, s), BF16),
        jax.ShapeDtypeStruct((b, 768, s), BF16),
        jax.ShapeDtypeStruct((b, 256, s), BF16),
        jax.ShapeDtypeStruct((b, s, 384), BF16),
        jax.ShapeDtypeStruct((b, s, 384), BF16),
        jax.ShapeDtypeStruct((b, s, 256), BF16),
        jax.ShapeDtypeStruct((b, FC_ROWS, s), F32),
    )
    out_specs = (tok(384), tok(768), tok(256), tr(384), tr(768), tr(256), tok(384), tok(384),
                 tok(256), tr(FC_ROWS))
    in_specs = [tok(d), full(wmain.shape), full(wt.shape), full(wuq.shape), full(wukvk.shape),
                full(wukvv.shape), full(gcq.shape), full(gckv.shape), full(bfc.shape),
                tok(128), tok(128), tr(32), tr(32)]
    return pl.pallas_call(
        functools.partial(_prep_kernel, tm=tm),
        grid=grid, in_specs=in_specs, out_specs=out_specs, out_shape=out_shape,
        scratch_shapes=[pltpu.VMEM((FC_ROWS, 128), F32)],
        compiler_params=pltpu.CompilerParams(
            dimension_semantics=("arbitrary", "arbitrary"), vmem_limit_bytes=VMEM_LIMIT),
        name="prep",
    )(x, wmain, wt, wuq, wukvk, wukvv, gcq, gckv, bfc, cos128, sin128, cost, sint)


def _flash(i, tq, n_maps, scores_fn, v_fn, mask):
    def update(state, j, diag):
        v = v_fn(j)
        new = []
        for (m, l, acc), s in zip(state, scores_fn(j)):
            if diag:
                s = jnp.where(mask, s, NEG_BIG)
            m_new = jnp.maximum(m, jnp.max(s, axis=1, keepdims=True))
            alpha = jnp.exp(m - m_new)
            p = jnp.exp(s - m_new)
            l_new = alpha * l + jnp.sum(p, axis=1, keepdims=True)
            acc_new = alpha * acc + _dot(p.astype(BF16), v)
            new.append((m_new, l_new, acc_new))
        return tuple(new)

    init = tuple((jnp.full((tq, 1), NEG_BIG, F32), jnp.zeros((tq, 1), F32),
                  jnp.zeros((tq, 128), F32)) for _ in range(n_maps))
    state = update(init, i, True)
    state = lax.fori_loop(0, i, lambda j, st: update(st, j, False), state)
    return [acc * (1.0 / l) for (_, l, acc) in state]


def _chunk_mask(tq):
    r = lax.broadcasted_iota(jnp.int32, (tq, tq), 0)
    c = lax.broadcasted_iota(jnp.int32, (tq, tq), 1)
    return (c // CHUNK) <= (r // CHUNK)


def _causal_mask(tq):
    r = lax.broadcasted_iota(jnp.int32, (tq, tq), 0)
    c = lax.broadcasted_iota(jnp.int32, (tq, tq), 1)
    return c <= r


def _attn_a_kernel(q_ref, kt_ref, v_ref, pcol_ref, prow_ref, slope_ref, lam_ref, g_ref, o_ref,
                   *, tq, lam_init):
    i = pl.program_id(2)
    q = q_ref[0]
    lane = lax.broadcasted_iota(jnp.int32, (tq, 128), 1)
    pt = pcol_ref[0]
    mask = _chunk_mask(tq)
    lam_v = lam_ref[...]
    lam = (jnp.exp(jnp.sum(lam_v[0:1] * lam_v[1:2], axis=1, keepdims=True))
           - jnp.exp(jnp.sum(lam_v[2:3] * lam_v[3:4], axis=1, keepdims=True)) + lam_init)

    def koff(j):
        return pl.multiple_of(j * tq, tq)

    outs = []
    for hh in range(2):
        qms = [jnp.where((lane >= 64 * hh + 32 * mm) & (lane < 64 * hh + 32 * mm + 32), q,
                         jnp.zeros_like(q)) for mm in range(2)]
        nslope = -slope_ref[0, hh:hh + 1, :]

        def scores_fn(j, qms=qms, nslope=nslope):
            kt = kt_ref[0, :, pl.ds(koff(j), tq)]
            ps = prow_ref[0, :, pl.ds(koff(j), tq)]
            bias = nslope * jnp.abs(pt - ps)
            return [_dot(qm, kt) + bias for qm in qms]

        def v_fn(j):
            return v_ref[0, pl.ds(koff(j), tq), :]

        o1, o2 = _flash(i, tq, 2, scores_fn, v_fn, mask)
        outs.append(o1 - lam * o2)
    o = jnp.where(lane < 64, outs[0], outs[1])
    sq = o * o
    ms0 = jnp.sum(jnp.where(lane < 64, sq, 0.0), axis=1, keepdims=True) * (1.0 / A_VDIM)
    ms1 = jnp.sum(jnp.where(lane >= 64, sq, 0.0), axis=1, keepdims=True) * (1.0 / A_VDIM)
    ms = jnp.where(lane < 64, ms0, ms1)
    y = o * lax.rsqrt(ms + RMS_EPS) * g_ref[...] * (1.0 - lam_init)
    o_ref[0] = y.astype(BF16)


def _attn_b_kernel(q_ref, kt_ref, v_ref, o_ref, *, tq):
    i = pl.program_id(2)
    lane = lax.broadcasted_iota(jnp.int32, (tq, 128), 1)
    mask = _chunk_mask(tq)

    def koff(j):
        return pl.multiple_of(j * tq, tq)

    outs = []
    for hh in range(2):
        qh = q_ref[0, :, hh * B_PAD:(hh + 1) * B_PAD]

        def scores_fn(j, qh=qh, hh=hh):
            kt = kt_ref[0, hh * B_PAD:(hh + 1) * B_PAD, pl.ds(koff(j), tq)]
            return [_dot(qh, kt)]

        def v_fn(j):
            return v_ref[0, pl.ds(koff(j), tq), :]

        outs.append(_flash(i, tq, 1, scores_fn, v_fn, mask)[0])
    o_ref[0] = jnp.where(lane < 64, outs[0], outs[1]).astype(BF16)


def _attn_c_kernel(q_ref, kt_ref, v_ref, ccol_ref, crow_ref, o_ref, *, tq):
    i = pl.program_id(2)
    q = q_ref[0]
    lane = lax.broadcasted_iota(jnp.int32, (tq, 128), 1)
    mask = _causal_mask(tq)

    def koff(j):
        return pl.multiple_of(j * tq, tq)

    outs = []
    for hh in range(2):
        qm = jnp.where((lane >= 64 * hh) & (lane < 64 * hh + 64), q, jnp.zeros_like(q))
        ct = ccol_ref[0, 0, :, hh:hh + 1]

        def scores_fn(j, qm=qm, ct=ct, hh=hh):
            kt = kt_ref[0, :, pl.ds(koff(j), tq)]
            cs = crow_ref[0, 0, hh:hh + 1, pl.ds(koff(j), tq)]
            return [_dot(qm, kt) + (ct - cs)]

        def v_fn(j):
            return v_ref[0, pl.ds(koff(j), tq), :]

        outs.append(_flash(i, tq, 1, scores_fn, v_fn, mask)[0])
    o_ref[0] = jnp.where(lane < 64, outs[0], outs[1]).astype(BF16)


def _attn_params():
    return pltpu.CompilerParams(
        dimension_semantics=("arbitrary", "arbitrary", "arbitrary"), vmem_limit_bytes=VMEM_LIMIT)


def _attn_a_call(qa, kta, va, pcol, prow, slopes, lamv, g128, *, tq, lam_init):
    b, s, _ = qa.shape
    grid = (b, 3, s // tq)
    return pl.pallas_call(
        functools.partial(_attn_a_kernel, tq=tq, lam_init=lam_init),
        grid=grid,
        in_specs=[
            pl.BlockSpec((1, tq, 128), lambda bi, c, i: (bi, i, c)),
            pl.BlockSpec((1, 128, s), lambda bi, c, i: (bi, c, 0)),
            pl.BlockSpec((1, s, 128), lambda bi, c, i: (bi, 0, c)),
            pl.BlockSpec((1, tq, 1), lambda bi, c, i: (bi, i, 0)),
            pl.BlockSpec((1, 1, s), lambda bi, c, i: (bi, 0, 0)),
            pl.BlockSpec((1, 8, tq), lambda bi, c, i: (c, 0, 0)),
            pl.BlockSpec((8, 128), lambda bi, c, i: (0, 0)),
            pl.BlockSpec((1, 128), lambda bi, c, i: (0, 0)),
        ],
        out_specs=pl.BlockSpec((1, tq, 128), lambda bi, c, i: (bi, i, c)),
        out_shape=jax.ShapeDtypeStruct((b, s, MIX_A), BF16),
        compiler_params=_attn_params(), name="attn_a",
    )(qa, kta, va, pcol, prow, slopes, lamv, g128)


def _attn_b_call(qb, ktb, vb, *, tq):
    b, s, _ = qb.shape
    grid = (b, 3, s // tq)
    return pl.pallas_call(
        functools.partial(_attn_b_kernel, tq=tq),
        grid=grid,
        in_specs=[
            pl.BlockSpec((1, tq, 2 * B_PAD), lambda bi, c, i: (bi, i, c)),
            pl.BlockSpec((1, 2 * B_PAD, s), lambda bi, c, i: (bi, c, 0)),
            pl.BlockSpec((1, s, 128), lambda bi, c, i: (bi, 0, c)),
        ],
        out_specs=pl.BlockSpec((1, tq, 128), lambda bi, c, i: (bi, i, c)),
        out_shape=jax.ShapeDtypeStruct((b, s, MIX_B), BF16),
        compiler_params=_attn_params(), name="attn_b",
    )(qb, ktb, vb)


def _attn_c_call(qc, ktc, vc, ccol, crow, *, tq):
    b, s, _ = qc.shape
    grid = (b, 2, s // tq)
    return pl.pallas_call(
        functools.partial(_attn_c_kernel, tq=tq),
        grid=grid,
        in_specs=[
            pl.BlockSpec((1, tq, 128), lambda bi, c, i: (bi, i, c)),
            pl.BlockSpec((1, 128, s), lambda bi, c, i: (bi, c, 0)),
            pl.BlockSpec((1, s, 128), lambda bi, c, i: (bi, 0, c)),
            pl.BlockSpec((1, 1, tq, 2), lambda bi, c, i: (bi, c, i, 0)),
            pl.BlockSpec((1, 1, 2, s), lambda bi, c, i: (bi, c, 0, 0)),
        ],
        out_specs=pl.BlockSpec((1, tq, 128), lambda bi, c, i: (bi, i, c)),
        out_shape=jax.ShapeDtypeStruct((b, s, MIX_C), BF16),
        compiler_params=_attn_params(), name="attn_c",
    )(qc, ktc, vc, ccol, crow)


def _layer_norm(v, g, b):
    mu = jnp.mean(v, axis=-1, keepdims=True)
    c = v - mu
    var = jnp.mean(c * c, axis=-1, keepdims=True)
    return c * lax.rsqrt(var + LN_EPS) * g + b


def _route(h, wr1_ref, wr2_ref, rbias_ref):
    tm = h.shape[0]
    h1 = h.astype(BF16)
    h2 = (h - h1.astype(F32)).astype(BF16)
    w1 = wr1_ref[...]
    logits = _dot(h1, w1) + _dot(h2, w1) + _dot(h1, wr2_ref[...]) + rbias_ref[...]
    col = lax.broadcasted_iota(jnp.int32, (tm, ROUTER_COLS), 1)
    colf = col.astype(F32)
    big = float(ROUTER_COLS)
    gmask = col < N_GROUPS
    gmax = jnp.max(jnp.where(gmask, logits, NEG_BIG), axis=1, keepdims=True)
    g_idx = jnp.min(jnp.where(gmask & (logits == gmax), colf, big), axis=1, keepdims=True)
    g_w = 1.0 / jnp.sum(jnp.where(gmask, jnp.exp(logits - gmax), 0.0), axis=1, keepdims=True)
    lo = N_GROUPS + g_idx * EXPERTS_PER_GROUP
    emask = (colf >= lo) & (colf < lo + EXPERTS_PER_GROUP)
    top1 = jnp.max(jnp.where(emask, logits, NEG_BIG), axis=1, keepdims=True)
    idx1 = jnp.min(jnp.where(emask & (logits == top1), colf, big), axis=1, keepdims=True)
    emask2 = emask & (colf != idx1)
    top2 = jnp.max(jnp.where(emask2, logits, NEG_BIG), axis=1, keepdims=True)
    idx2 = jnp.min(jnp.where(emask2 & (logits == top2), colf, big), axis=1, keepdims=True)
    e2 = jnp.exp(top2 - top1)
    w_1 = g_w / (1.0 + e2)
    w_2 = g_w * e2 / (1.0 + e2)
    return jnp.where(colf == idx1, w_1, 0.0) + jnp.where(colf == idx2, w_2, 0.0)


def _moe_kernel(x_ref, oa_ref, ob_ref, oc_ref, p_ref, woa_ref, wob_ref, woc_ref, g1_ref, b1_ref,
                wr1_ref, wr2_ref, rbias_ref, wg_ref, wu_ref, wd_ref, wpg_ref, wpp_ref, g2_ref,
                b2_ref, y_ref, h_ref, hb_ref, comb_ref, acc_ref):
    e = pl.program_id(1)

    @pl.when(e == 0)
    def _():
        mix = (_dot(oa_ref[...], woa_ref[...]) + _dot(ob_ref[...], wob_ref[...])
               + _dot(oc_ref[...], woc_ref[...]))
        h = _layer_norm(DEEPNORM_ALPHA * x_ref[...] + mix, g1_ref[...], b1_ref[...])
        h_ref[...] = h
        hb_ref[...] = h.astype(BF16)
        comb_ref[...] = _route(h, wr1_ref, wr2_ref, rbias_ref)
        acc_ref[...] = jnp.zeros_like(acc_ref)

    hb = hb_ref[...]
    gate = _dot(hb, wg_ref[0])
    up = _dot(hb, wu_ref[0])
    col = lax.broadcasted_iota(jnp.int32, comb_ref.shape, 1)
    cw = jnp.sum(jnp.where(col == e + N_GROUPS, comb_ref[...], 0.0), axis=1, keepdims=True)
    act = gate * (1.0 / (1.0 + jnp.exp(-gate))) * up * cw
    acc_ref[...] += _dot(act.astype(BF16), wd_ref[0])

    @pl.when(e == N_EXPERTS - 1)
    def _():
        h = h_ref[...]
        gl = _dot(hb_ref[...], wpg_ref[...])
        ple = (1.0 / (1.0 + jnp.exp(-gl))) * _dot(p_ref[...].astype(BF16), wpp_ref[...])
        y_ref[...] = _layer_norm(DEEPNORM_ALPHA * h + acc_ref[...] + ple, g2_ref[...], b2_ref[...])


def _moe_call(x2, oa, ob, oc, p2, woa, wob, woc, g1, b1, wr1, wr2, rbias, wg, wu, wd, wpg, wpp,
              g2, b2, *, tm):
    n, d = x2.shape
    grid = (n // tm, N_EXPERTS)
    tok = lambda w: pl.BlockSpec((tm, w), lambda t, e: (t, 0))
    full = lambda a: pl.BlockSpec(a.shape, lambda t, e: (0,) * a.ndim)
    exp = lambda a: pl.BlockSpec((1,) + a.shape[1:], lambda t, e: (e, 0, 0))
    return pl.pallas_call(
        _moe_kernel,
        grid=grid,
        in_specs=[tok(d), tok(MIX_A), tok(MIX_B), tok(MIX_C), tok(PLE_DIM), full(woa), full(wob),
                  full(woc), full(g1), full(b1), full(wr1), full(wr2), full(rbias), exp(wg), exp(wu),
                  exp(wd), full(wpg), full(wpp), full(g2), full(b2)],
        out_specs=tok(d),
        out_shape=jax.ShapeDtypeStruct((n, d), F32),
        scratch_shapes=[pltpu.VMEM((tm, d), F32), pltpu.VMEM((tm, d), BF16),
                        pltpu.VMEM((tm, ROUTER_COLS), F32), pltpu.VMEM((tm, d), F32)],
        compiler_params=pltpu.CompilerParams(
            dimension_semantics=("arbitrary", "arbitrary"), vmem_limit_bytes=VMEM_LIMIT),
        name="moe",
    )(x2, oa, ob, oc, p2, woa, wob, woc, g1, b1, wr1, wr2, rbias, wg, wu, wd, wpg, wpp, g2, b2)


def _cols(w, name):
    lo, hi = _OFF[name]
    return w[:, lo:hi]


def _rot_pairs(w):
    return jnp.concatenate([-w[..., 16:32], w[..., 0:16]], axis=-1)


def _layer_weights(w_in, w_uq, w_ukv, b_forget):
    wmain = jnp.concatenate([_cols(w_in, n) for n in ("qa", "va", "cq", "ckv", "qc", "vc")], axis=1)
    kr = _cols(w_in, "kr")
    fc_pad = jnp.pad(_cols(w_in, "fc"), ((0, 0), (0, FC_ROWS - C_HEADS)))
    wt = jnp.concatenate([_cols(w_in, "ka"), _cols(w_in, "kc"), kr, _rot_pairs(kr), fc_pad], axis=1).T
    wq = w_uq.reshape(B_Q_RANK, B_HEADS, B_NOPE + B_ROPE)
    zpad = jnp.zeros((B_Q_RANK, B_HEADS, B_PAD - B_NOPE - B_ROPE), w_uq.dtype)
    plain = jnp.concatenate([wq, zpad], axis=-1).reshape(B_Q_RANK, B_HEADS * B_PAD)
    rot = jnp.concatenate([jnp.zeros_like(wq[..., :B_NOPE]), _rot_pairs(wq[..., B_NOPE:]), zpad],
                          axis=-1).reshape(B_Q_RANK, B_HEADS * B_PAD)
    wuq = jnp.concatenate([plain, rot], axis=1)
    wkv = w_ukv.reshape(B_KV_RANK, B_HEADS, B_NOPE + B_VDIM)
    wukvk = wkv[..., :B_NOPE].reshape(B_KV_RANK, B_HEADS * B_NOPE).T
    wukvv = wkv[..., B_NOPE:].reshape(B_KV_RANK, B_HEADS * B_VDIM)
    bfc = jnp.pad(b_forget, (0, FC_ROWS - C_HEADS)).reshape(FC_ROWS, 1)
    return (wmain.astype(BF16), wt.astype(BF16), wuq.astype(BF16), wukvk.astype(BF16),
            wukvv.astype(BF16), bfc.astype(F32))


def _rope_inputs(positions):
    half = B_ROPE // 2
    inv = ROPE_THETA ** (-jnp.arange(half, dtype=F32) / half)
    ang = positions.astype(F32)[..., None] * inv
    cos, sin = jnp.cos(ang), jnp.sin(ang)
    b, s, _ = cos.shape
    ones = jnp.ones((b, s, B_NOPE), F32)
    zeros = jnp.zeros((b, s, B_PAD - B_NOPE - B_ROPE), F32)
    cos128 = jnp.concatenate([ones, cos, cos, zeros], axis=-1)
    sin128 = jnp.concatenate([jnp.zeros_like(ones), sin, sin, zeros], axis=-1)
    cost = jnp.swapaxes(jnp.concatenate([cos, cos], axis=-1), 1, 2)
    sint = jnp.swapaxes(jnp.concatenate([sin, sin], axis=-1), 1, 2)
    return cos128, sin128, cost, sint


def kernel(x, p, positions, w_in, w_uq, w_ukv, g_cq, g_ckv, lam_q1, lam_k1, lam_q2, lam_k2, g_diff,
           b_forget, w_out, ln1_g, ln1_b, w_group, b_group, w_erouter, b_erouter, w_gate_e, w_up_e,
           w_down_e, w_ple_gate, w_ple_proj, ln2_g, ln2_b):
    b, s, d = x.shape
    n = b * s
    tq = 256
    tm_prep = 512
    tm_moe = 512
    cos128, sin128, cost, sint = _rope_inputs(positions)
    posf = positions.astype(F32)
    pcol = posf[:, :, None]
    prow = posf[:, None, :]
    slopes = 2.0 ** (-8.0 * jnp.arange(1, A_HEADS + 1, dtype=F32) / A_HEADS)
    slopes = jnp.broadcast_to(
        jnp.pad(slopes.reshape(3, 2), ((0, 0), (0, 6)))[:, :, None], (3, 8, tq))
    row = lambda v: v.reshape(1, -1).astype(F32)

    for i in range(DEPTH):
        wmain, wt, wuq, wukvk, wukvv, bfc = _layer_weights(w_in[i], w_uq[i], w_ukv[i], b_forget[i])
        qa, qb, qc, kta, ktb, ktc, va, vb, vc, cumt = _prep_call(
            x, wmain, wt, wuq, wukvk, wukvv, row(g_cq[i]), row(g_ckv[i]), bfc,
            cos128, sin128, cost, sint, tm=tm_prep)
        lamv = jnp.pad(jnp.stack([lam_q1[i], lam_k1[i], lam_q2[i], lam_k2[i]]).astype(F32),
                       ((0, 4), (0, 128 - A_DIM)))
        g128 = jnp.tile(g_diff[i].astype(F32), 2).reshape(1, 128)
        lam_init = 0.8 - 0.6 * math.exp(-0.3 * i)
        oa = _attn_a_call(qa, kta, va, pcol, prow, slopes, lamv, g128, tq=tq, lam_init=lam_init)
        ob = _attn_b_call(qb, ktb, vb, tq=tq)
        crow = cumt[:, :C_HEADS].reshape(b, 2, 2, s)
        ccol = jnp.swapaxes(crow, 2, 3)
        oc = _attn_c_call(qc, ktc, vc, ccol, crow, tq=tq)

        wo = w_out[i].astype(BF16)
        wr = jnp.concatenate([w_group[i], jnp.moveaxis(w_erouter[i], 0, 1).reshape(d, N_EXPERTS)],
                             axis=1)
        wr = jnp.pad(wr, ((0, 0), (0, ROUTER_COLS - wr.shape[1]))).astype(F32)
        wr1 = wr.astype(BF16)
        wr2 = (wr - wr1.astype(F32)).astype(BF16)
        rbias = jnp.pad(jnp.concatenate([b_group[i], b_erouter[i].reshape(-1)]),
                        (0, ROUTER_COLS - N_GROUPS - N_EXPERTS)).reshape(1, ROUTER_COLS).astype(F32)
        y = _moe_call(
            x.reshape(n, d), oa.reshape(n, MIX_A), ob.reshape(n, MIX_B), oc.reshape(n, MIX_C),
            p[i].reshape(n, PLE_DIM), wo[:MIX_A], wo[MIX_A:MIX_A + MIX_B], wo[MIX_A + MIX_B:],
            row(ln1_g[i]), row(ln1_b[i]), wr1, wr2, rbias, w_gate_e[i].astype(BF16),
            w_up_e[i].astype(BF16), w_down_e[i].astype(BF16), w_ple_gate[i].astype(BF16),
            w_ple_proj[i].astype(BF16), row(ln2_g[i]), row(ln2_b[i]), tm=tm_moe)
        x = y.reshape(b, s, d)
    return x
```

```python
import functools
import math

import jax
import jax.numpy as jnp
from jax import lax
from jax.experimental import pallas as pl
from jax.experimental.pallas import tpu as pltpu

F32 = jnp.float32
BF16 = jnp.bfloat16

D_MODEL = 1024
DEPTH = 2
CHUNK = 64
PLE_DIM = 256

A_HEADS = 6
A_DIM = 32
A_VDIM = 64
B_HEADS = 6
B_Q_RANK = 256
B_KV_RANK = 128
B_NOPE = 64
B_ROPE = 32
B_VDIM = 64
ROPE_THETA = 10000.0
C_HEADS = 4
C_DIM = 64

MIX_A = A_HEADS * A_VDIM
MIX_B = B_HEADS * B_VDIM
MIX_C = C_HEADS * C_DIM
B_PAD = 128

N_GROUPS = 4
EXPERTS_PER_GROUP = 8
N_EXPERTS = N_GROUPS * EXPERTS_PER_GROUP
D_EXPERT = 256
ROUTER_COLS = 128

DEEPNORM_ALPHA = (2 * DEPTH) ** 0.25
LN_EPS = 1e-5
RMS_EPS = 1e-6
NEG_BIG = -1e30
LOG2E = math.log2(math.e)

_OFF = {}
_o = 0
for _name, _w in (("qa", 384), ("ka", 384), ("va", 384), ("cq", 256), ("ckv", 128), ("kr", 32),
                  ("qc", 256), ("kc", 256), ("vc", 256), ("fc", 4)):
    _OFF[_name] = (_o, _o + _w)
    _o += _w

T_KA, T_KC, T_KR, T_KRS, T_FC, T_ROWS = 0, 384, 640, 672, 704, 720
FC_ROWS = T_ROWS - T_FC

VMEM_LIMIT = 48 * 1024 * 1024

NT_DIMS = (((1,), (1,)), ((), ()))


def _dot(a, b):
    return jnp.dot(a, b, preferred_element_type=F32)


def _dot_nt(a, b):
    return lax.dot_general(a, b, NT_DIMS, preferred_element_type=F32)


def _split3(a):
    a1 = a.astype(BF16)
    r1 = a - a1.astype(F32)
    a2 = r1.astype(BF16)
    a3 = (r1 - a2.astype(F32)).astype(BF16)
    return a1, a2, a3


def _prep_kernel(x_ref, wmain_ref, wt_ref, wuq_ref, wukvk_ref, wukvv_ref, gcq_ref, gckv_ref,
                 bf_ref, cos_ref, sin_ref, cost_ref, sint_ref,
                 qa_ref, qb_ref, qc_ref, kta_ref, ktb_ref, ktc_ref, va_ref, vb_ref, vc_ref,
                 cum_ref, carry_ref, *, tm):
    j = pl.program_id(1)

    @pl.when(j == 0)
    def _():
        carry_ref[...] = jnp.zeros_like(carry_ref)

    xb = x_ref[0].astype(BF16)
    z = _dot(xb, wmain_ref[...])
    zt = _dot_nt(wt_ref[...], xb)

    qa_ref[0] = (z[:, 0:384] * (LOG2E / math.sqrt(A_DIM))).astype(BF16)
    va_ref[0] = z[:, 384:768].astype(BF16)
    kta_ref[0] = zt[T_KA:T_KA + 384].astype(BF16)

    qc_ref[0] = (z[:, 1152:1408] * (LOG2E / math.sqrt(C_DIM))).astype(BF16)
    vc_ref[0] = z[:, 1408:1664].astype(BF16)
    ktc_ref[0] = zt[T_KC:T_KC + 256].astype(BF16)

    fct = zt[T_FC:T_ROWS] + bf_ref[...]
    logf = jnp.minimum(fct, 0.0) - jnp.log(1.0 + jnp.exp(-jnp.abs(fct)))
    r_i = lax.broadcasted_iota(jnp.int32, (tm, tm), 0)
    c_i = lax.broadcasted_iota(jnp.int32, (tm, tm), 1)
    tri = jnp.where(r_i <= c_i, 1.0, 0.0).astype(BF16)
    l1, l2, l3 = _split3(logf)
    cum = _dot(l1, tri) + _dot(l2, tri) + _dot(l3, tri) + carry_ref[:, 0:1]
    cum_ref[0] = cum * LOG2E
    carry_ref[...] = jnp.broadcast_to(cum[:, tm - 1:tm], carry_ref.shape)

    cq = z[:, 768:1024]
    cq_n = cq * lax.rsqrt(jnp.mean(cq * cq, axis=-1, keepdims=True) + RMS_EPS) * gcq_ref[...]
    zq = _dot(cq_n.astype(BF16), wuq_ref[...])
    cosb = cos_ref[0]
    sinb = sin_ref[0]
    scale_b = LOG2E / math.sqrt(B_NOPE + B_ROPE)
    for h in range(B_HEADS):
        lo = h * B_PAD
        qh = zq[:, lo:lo + B_PAD] * cosb + zq[:, 768 + lo:768 + lo + B_PAD] * sinb
        qb_ref[0, :, lo:lo + B_PAD] = (qh * scale_b).astype(BF16)

    ckv = z[:, 1024:1152]
    ckv_n = ckv * lax.rsqrt(jnp.mean(ckv * ckv, axis=-1, keepdims=True) + RMS_EPS) * gckv_ref[...]
    ckv_b = ckv_n.astype(BF16)
    vb_ref[0] = _dot(ckv_b, wukvv_ref[...]).astype(BF16)
    knt = _dot_nt(wukvk_ref[...], ckv_b)
    krt = (zt[T_KR:T_KR + 32] * cost_ref[0] + zt[T_KRS:T_KRS + 32] * sint_ref[0]).astype(BF16)
    zpad = jnp.zeros((B_PAD - B_NOPE - B_ROPE, tm), BF16)
    for h in range(B_HEADS):
        lo = h * B_PAD
        ktb_ref[0, lo:lo + B_NOPE, :] = knt[h * B_NOPE:(h + 1) * B_NOPE].astype(BF16)
        ktb_ref[0, lo + B_NOPE:lo + B_NOPE + B_ROPE, :] = krt
        ktb_ref[0, lo + B_NOPE + B_ROPE:lo + B_PAD, :] = zpad


def _prep_call(x, wmain, wt, wuq, wukvk, wukvv, gcq, gckv, bfc, cos128, sin128, cost, sint, *, tm):
    b, s, d = x.shape
    grid = (b, s // tm)
    full = lambda shape: pl.BlockSpec(shape, lambda bi, j: (0,) * len(shape))
    tok = lambda w: pl.BlockSpec((1, tm, w), lambda bi, j: (bi, j, 0))
    tr = lambda r: pl.BlockSpec((1, r, tm), lambda bi, j: (bi, 0, j))
    out_shape = (
        jax.ShapeDtypeStruct((b, s, 384), BF16),
        jax.ShapeDtypeStruct((b, s, 768), BF16),
        jax.ShapeDtypeStruct((b, s, 256), BF16),
        jax.ShapeDtypeStruct((b, 384, s), BF16),
        jax.ShapeDtypeStruct((b, 768, s), BF16),
        jax.ShapeDtypeStruct((b, 256, s), BF16),
        jax.ShapeDtypeStruct((b, s, 384), BF16),
        jax.ShapeDtypeStruct((b, s, 384), BF16),
        jax.ShapeDtypeStruct((b, s, 256), BF16),
        jax.ShapeDtypeStruct((b, FC_ROWS, s), F32),
    )
    out_specs = (tok(384), tok(768), tok(256), tr(384), tr(768), tr(256), tok(384), tok(384),
                 tok(256), tr(FC_ROWS))
    in_specs = [tok(d), full(wmain.shape), full(wt.shape), full(wuq.shape), full(wukvk.shape),
                full(wukvv.shape), full(gcq.shape), full(gckv.shape), full(bfc.shape),
                tok(128), tok(128), tr(32), tr(32)]
    return pl.pallas_call(
        functools.partial(_prep_kernel, tm=tm),
        grid=grid, in_specs=in_specs, out_specs=out_specs, out_shape=out_shape,
        scratch_shapes=[pltpu.VMEM((FC_ROWS, 128), F32)],
        compiler_params=pltpu.CompilerParams(
            dimension_semantics=("arbitrary", "arbitrary"), vmem_limit_bytes=VMEM_LIMIT),
        name="prep",
    )(x, wmain, wt, wuq, wukvk, wukvv, gcq, gckv, bfc, cos128, sin128, cost, sint)


def _flash(i, tq, n_maps, scores_fn, mask):
    def update(state, j, diag):
        new = []
        for (m, l, acc), (s, v) in zip(state, scores_fn(j)):
            if diag:
                s = jnp.where(mask, s, NEG_BIG)
            m_new = jnp.maximum(m, jnp.max(s, axis=1, keepdims=True))
            alpha = jnp.exp2(m - m_new)
            p = jnp.exp2(s - m_new)
            l_new = alpha * l + jnp.sum(p, axis=1, keepdims=True)
            acc_new = alpha * acc + _dot(p.astype(BF16), v)
            new.append((m_new, l_new, acc_new))
        return tuple(new)

    init = tuple((jnp.full((tq, 1), NEG_BIG, F32), jnp.zeros((tq, 1), F32),
                  jnp.zeros((tq, 128), F32)) for _ in range(n_maps))
    state = update(init, i, True)
    state = lax.fori_loop(0, i, lambda j, st: update(st, j, False), state)
    return [acc * (1.0 / l) for (_, l, acc) in state]


def _chunk_mask(tq):
    r = lax.broadcasted_iota(jnp.int32, (tq, tq), 0)
    c = lax.broadcasted_iota(jnp.int32, (tq, tq), 1)
    return (c // CHUNK) <= (r // CHUNK)


def _causal_mask(tq):
    r = lax.broadcasted_iota(jnp.int32, (tq, tq), 0)
    c = lax.broadcasted_iota(jnp.int32, (tq, tq), 1)
    return c <= r


def _attn_a_kernel(q_ref, kt_ref, v_ref, pcol_ref, prow_ref, slope_ref, lam_ref, g_ref, o_ref,
                   *, tq, lam_init):
    i = pl.program_id(2)
    q = q_ref[0]
    lane = lax.broadcasted_iota(jnp.int32, (tq, 128), 1)
    pt = pcol_ref[0]
    mask = _chunk_mask(tq)
    lam_v = lam_ref[...]
    lam = (jnp.exp(jnp.sum(lam_v[0:1] * lam_v[1:2], axis=1, keepdims=True))
           - jnp.exp(jnp.sum(lam_v[2:3] * lam_v[3:4], axis=1, keepdims=True)) + lam_init)

    def koff(j):
        return pl.multiple_of(j * tq, tq)

    qms = [jnp.where((lane >= 32 * mi) & (lane < 32 * mi + 32), q, jnp.zeros_like(q))
           for mi in range(4)]
    nslopes = [-LOG2E * slope_ref[0, hh:hh + 1, :] for hh in range(2)]

    def scores_fn(j):
        kt = kt_ref[0, :, pl.ds(koff(j), tq)]
        ps = prow_ref[0, :, pl.ds(koff(j), tq)]
        v = v_ref[0, pl.ds(koff(j), tq), :]
        dist = jnp.abs(pt - ps)
        out = []
        for hh in range(2):
            bias = nslopes[hh] * dist
            out += [(_dot(qms[2 * hh + mm], kt) + bias, v) for mm in range(2)]
        return out

    o = _flash(i, tq, 4, scores_fn, mask)
    outs = [o[0] - lam * o[1], o[2] - lam * o[3]]
    o = jnp.where(lane < 64, outs[0], outs[1])
    sq = o * o
    ms0 = jnp.sum(jnp.where(lane < 64, sq, 0.0), axis=1, keepdims=True) * (1.0 / A_VDIM)
    ms1 = jnp.sum(jnp.where(lane >= 64, sq, 0.0), axis=1, keepdims=True) * (1.0 / A_VDIM)
    ms = jnp.where(lane < 64, ms0, ms1)
    y = o * lax.rsqrt(ms + RMS_EPS) * g_ref[...] * (1.0 - lam_init)
    o_ref[0] = y.astype(BF16)


def _attn_b_kernel(q_ref, kt_ref, v_ref, o_ref, *, tq):
    i = pl.program_id(2)
    lane = lax.broadcasted_iota(jnp.int32, (tq, 128), 1)
    mask = _chunk_mask(tq)

    def koff(j):
        return pl.multiple_of(j * tq, tq)

    qs = [q_ref[0, :, hh * B_PAD:(hh + 1) * B_PAD] for hh in range(2)]

    def scores_fn(j):
        v = v_ref[0, pl.ds(koff(j), tq), :]
        return [(_dot(qs[hh], kt_ref[0, hh * B_PAD:(hh + 1) * B_PAD, pl.ds(koff(j), tq)]), v)
                for hh in range(2)]

    outs = _flash(i, tq, 2, scores_fn, mask)
    o_ref[0] = jnp.where(lane < 64, outs[0], outs[1]).astype(BF16)


def _attn_c_kernel(q_ref, kt_ref, v_ref, ccol_ref, crow_ref, o_ref, *, tq):
    i = pl.program_id(2)
    q = q_ref[0]
    lane = lax.broadcasted_iota(jnp.int32, (tq, 128), 1)
    mask = _causal_mask(tq)

    def koff(j):
        return pl.multiple_of(j * tq, tq)

    qms = [jnp.where((lane >= 64 * hh) & (lane < 64 * hh + 64), q, jnp.zeros_like(q))
           for hh in range(2)]
    cts = [ccol_ref[0, 0, :, hh:hh + 1] for hh in range(2)]

    def scores_fn(j):
        kt = kt_ref[0, :, pl.ds(koff(j), tq)]
        v = v_ref[0, pl.ds(koff(j), tq), :]
        out = []
        for hh in range(2):
            cs = crow_ref[0, 0, hh:hh + 1, pl.ds(koff(j), tq)]
            out.append((_dot(qms[hh], kt) + (cts[hh] - cs), v))
        return out

    outs = _flash(i, tq, 2, scores_fn, mask)
    o_ref[0] = jnp.where(lane < 64, outs[0], outs[1]).astype(BF16)


def _attn_params():
    return pltpu.CompilerParams(
        dimension_semantics=("arbitrary", "arbitrary", "arbitrary"), vmem_limit_bytes=VMEM_LIMIT)


def _attn_a_call(qa, kta, va, pcol, prow, slopes, lamv, g128, *, tq, lam_init):
    b, s, _ = qa.shape
    grid = (b, 3, s // tq)
    return pl.pallas_call(
        functools.partial(_attn_a_kernel, tq=tq, lam_init=lam_init),
        grid=grid,
        in_specs=[
            pl.BlockSpec((1, tq, 128), lambda bi, c, i: (bi, i, c)),
            pl.BlockSpec((1, 128, s), lambda bi, c, i: (bi, c, 0)),
            pl.BlockSpec((1, s, 128), lambda bi, c, i: (bi, 0, c)),
            pl.BlockSpec((1, tq, 1), lambda bi, c, i: (bi, i, 0)),
            pl.BlockSpec((1, 1, s), lambda bi, c, i: (bi, 0, 0)),
            pl.BlockSpec((1, 8, tq), lambda bi, c, i: (c, 0, 0)),
            pl.BlockSpec((8, 128), lambda bi, c, i: (0, 0)),
            pl.BlockSpec((1, 128), lambda bi, c, i: (0, 0)),
        ],
        out_specs=pl.BlockSpec((1, tq, 128), lambda bi, c, i: (bi, i, c)),
        out_shape=jax.ShapeDtypeStruct((b, s, MIX_A), BF16),
        compiler_params=_attn_params(), name="attn_a",
    )(qa, kta, va, pcol, prow, slopes, lamv, g128)


def _attn_b_call(qb, ktb, vb, *, tq):
    b, s, _ = qb.shape
    grid = (b, 3, s // tq)
    return pl.pallas_call(
        functools.partial(_attn_b_kernel, tq=tq),
        grid=grid,
        in_specs=[
            pl.BlockSpec((1, tq, 2 * B_PAD), lambda bi, c, i: (bi, i, c)),
            pl.BlockSpec((1, 2 * B_PAD, s), lambda bi, c, i: (bi, c, 0)),
            pl.BlockSpec((1, s, 128), lambda bi, c, i: (bi, 0, c)),
        ],
        out_specs=pl.BlockSpec((1, tq, 128), lambda bi, c, i: (bi, i, c)),
        out_shape=jax.ShapeDtypeStruct((b, s, MIX_B), BF16),
        compiler_params=_attn_params(), name="attn_b",
    )(qb, ktb, vb)


def _attn_c_call(qc, ktc, vc, ccol, crow, *, tq):
    b, s, _ = qc.shape
    grid = (b, 2, s // tq)
    return pl.pallas_call(
        functools.partial(_attn_c_kernel, tq=tq),
        grid=grid,
        in_specs=[
            pl.BlockSpec((1, tq, 128), lambda bi, c, i: (bi, i, c)),
            pl.BlockSpec((1, 128, s), lambda bi, c, i: (bi, c, 0)),
            pl.BlockSpec((1, s, 128), lambda bi, c, i: (bi, 0, c)),
            pl.BlockSpec((1, 1, tq, 2), lambda bi, c, i: (bi, c, i, 0)),
            pl.BlockSpec((1, 1, 2, s), lambda bi, c, i: (bi, c, 0, 0)),
        ],
        out_specs=pl.BlockSpec((1, tq, 128), lambda bi, c, i: (bi, i, c)),
        out_shape=jax.ShapeDtypeStruct((b, s, MIX_C), BF16),
        compiler_params=_attn_params(), name="attn_c",
    )(qc, ktc, vc, ccol, crow)


def _layer_norm(v, g, b):
    mu = jnp.mean(v, axis=-1, keepdims=True)
    c = v - mu
    var = jnp.mean(c * c, axis=-1, keepdims=True)
    return c * lax.rsqrt(var + LN_EPS) * g + b


def _route(h, wr1_ref, wr2_ref, rbias_ref):
    tm = h.shape[0]
    h1 = h.astype(BF16)
    h2 = (h - h1.astype(F32)).astype(BF16)
    w1 = wr1_ref[...]
    logits = _dot(h1, w1) + _dot(h2, w1) + _dot(h1, wr2_ref[...]) + rbias_ref[...]
    col = lax.broadcasted_iota(jnp.int32, (tm, ROUTER_COLS), 1)
    colf = col.astype(F32)
    big = float(ROUTER_COLS)
    gmask = col < N_GROUPS
    gmax = jnp.max(jnp.where(gmask, logits, NEG_BIG), axis=1, keepdims=True)
    g_idx = jnp.min(jnp.where(gmask & (logits == gmax), colf, big), axis=1, keepdims=True)
    g_w = 1.0 / jnp.sum(jnp.where(gmask, jnp.exp(logits - gmax), 0.0), axis=1, keepdims=True)
    lo = N_GROUPS + g_idx * EXPERTS_PER_GROUP
    emask = (colf >= lo) & (colf < lo + EXPERTS_PER_GROUP)
    top1 = jnp.max(jnp.where(emask, logits, NEG_BIG), axis=1, keepdims=True)
    idx1 = jnp.min(jnp.where(emask & (logits == top1), colf, big), axis=1, keepdims=True)
    emask2 = emask & (colf != idx1)
    top2 = jnp.max(jnp.where(emask2, logits, NEG_BIG), axis=1, keepdims=True)
    idx2 = jnp.min(jnp.where(emask2 & (logits == top2), colf, big), axis=1, keepdims=True)
    e2 = jnp.exp(top2 - top1)
    w_1 = g_w / (1.0 + e2)
    w_2 = g_w * e2 / (1.0 + e2)
    return jnp.where(colf == idx1, w_1, 0.0) + jnp.where(colf == idx2, w_2, 0.0)


def _moe_kernel(x_ref, oa_ref, ob_ref, oc_ref, p_ref, woa_ref, wob_ref, woc_ref, g1_ref, b1_ref,
                wr1_ref, wr2_ref, rbias_ref, wg_ref, wu_ref, wd_ref, wpg_ref, wpp_ref, g2_ref,
                b2_ref, y_ref, h_ref, hb_ref, comb_ref, acc_ref):
    e = pl.program_id(1)

    @pl.when(e == 0)
    def _():
        mix = (_dot(oa_ref[...], woa_ref[...]) + _dot(ob_ref[...], wob_ref[...])
               + _dot(oc_ref[...], woc_ref[...]))
        h = _layer_norm(DEEPNORM_ALPHA * x_ref[...] + mix, g1_ref[...], b1_ref[...])
        h_ref[...] = h
        hb_ref[...] = h.astype(BF16)
        comb_ref[...] = _route(h, wr1_ref, wr2_ref, rbias_ref)
        acc_ref[...] = jnp.zeros_like(acc_ref)

    hb = hb_ref[...]
    gate = _dot(hb, wg_ref[0])
    up = _dot(hb, wu_ref[0])
    col = lax.broadcasted_iota(jnp.int32, comb_ref.shape, 1)
    cw = jnp.sum(jnp.where(col == e + N_GROUPS, comb_ref[...], 0.0), axis=1, keepdims=True)
    act = gate * (1.0 / (1.0 + jnp.exp(-gate))) * up * cw
    acc_ref[...] += _dot(act.astype(BF16), wd_ref[0])

    @pl.when(e == N_EXPERTS - 1)
    def _():
        h = h_ref[...]
        gl = _dot(hb_ref[...], wpg_ref[...])
        ple = (1.0 / (1.0 + jnp.exp(-gl))) * _dot(p_ref[...].astype(BF16), wpp_ref[...])
        y_ref[...] = _layer_norm(DEEPNORM_ALPHA * h + acc_ref[...] + ple, g2_ref[...], b2_ref[...])


def _moe_call(x2, oa, ob, oc, p2, woa, wob, woc, g1, b1, wr1, wr2, rbias, wg, wu, wd, wpg, wpp,
              g2, b2, *, tm):
    n, d = x2.shape
    grid = (n // tm, N_EXPERTS)
    tok = lambda w: pl.BlockSpec((tm, w), lambda t, e: (t, 0))
    full = lambda a: pl.BlockSpec(a.shape, lambda t, e: (0,) * a.ndim)
    exp = lambda a: pl.BlockSpec((1,) + a.shape[1:], lambda t, e: (e, 0, 0))
    return pl.pallas_call(
        _moe_kernel,
        grid=grid,
        in_specs=[tok(d), tok(MIX_A), tok(MIX_B), tok(MIX_C), tok(PLE_DIM), full(woa), full(wob),
                  full(woc), full(g1), full(b1), full(wr1), full(wr2), full(rbias), exp(wg), exp(wu),
                  exp(wd), full(wpg), full(wpp), full(g2), full(b2)],
        out_specs=tok(d),
        out_shape=jax.ShapeDtypeStruct((n, d), F32),
        scratch_shapes=[pltpu.VMEM((tm, d), F32), pltpu.VMEM((tm, d), BF16),
                        pltpu.VMEM((tm, ROUTER_COLS), F32), pltpu.VMEM((tm, d), F32)],
        compiler_params=pltpu.CompilerParams(
            dimension_semantics=("arbitrary", "arbitrary"), vmem_limit_bytes=VMEM_LIMIT),
        name="moe",
    )(x2, oa, ob, oc, p2, woa, wob, woc, g1, b1, wr1, wr2, rbias, wg, wu, wd, wpg, wpp, g2, b2)


def _cols(w, name):
    lo, hi = _OFF[name]
    return w[:, lo:hi]


def _rot_pairs(w):
    return jnp.concatenate([-w[..., 16:32], w[..., 0:16]], axis=-1)


def _layer_weights(w_in, w_uq, w_ukv, b_forget):
    wmain = jnp.concatenate([_cols(w_in, n) for n in ("qa", "va", "cq", "ckv", "qc", "vc")], axis=1)
    kr = _cols(w_in, "kr")
    fc_pad = jnp.pad(_cols(w_in, "fc"), ((0, 0), (0, FC_ROWS - C_HEADS)))
    wt = jnp.concatenate([_cols(w_in, "ka"), _cols(w_in, "kc"), kr, _rot_pairs(kr), fc_pad], axis=1).T
    wq = w_uq.reshape(B_Q_RANK, B_HEADS, B_NOPE + B_ROPE)
    zpad = jnp.zeros((B_Q_RANK, B_HEADS, B_PAD - B_NOPE - B_ROPE), w_uq.dtype)
    plain = jnp.concatenate([wq, zpad], axis=-1).reshape(B_Q_RANK, B_HEADS * B_PAD)
    rot = jnp.concatenate([jnp.zeros_like(wq[..., :B_NOPE]), _rot_pairs(wq[..., B_NOPE:]), zpad],
                          axis=-1).reshape(B_Q_RANK, B_HEADS * B_PAD)
    wuq = jnp.concatenate([plain, rot], axis=1)
    wkv = w_ukv.reshape(B_KV_RANK, B_HEADS, B_NOPE + B_VDIM)
    wukvk = wkv[..., :B_NOPE].reshape(B_KV_RANK, B_HEADS * B_NOPE).T
    wukvv = wkv[..., B_NOPE:].reshape(B_KV_RANK, B_HEADS * B_VDIM)
    bfc = jnp.pad(b_forget, (0, FC_ROWS - C_HEADS)).reshape(FC_ROWS, 1)
    return (wmain.astype(BF16), wt.astype(BF16), wuq.astype(BF16), wukvk.astype(BF16),
            wukvv.astype(BF16), bfc.astype(F32))


def _rope_inputs(positions):
    half = B_ROPE // 2
    inv = ROPE_THETA ** (-jnp.arange(half, dtype=F32) / half)
    ang = positions.astype(F32)[..., None] * inv
    cos, sin = jnp.cos(ang), jnp.sin(ang)
    b, s, _ = cos.shape
    ones = jnp.ones((b, s, B_NOPE), F32)
    zeros = jnp.zeros((b, s, B_PAD - B_NOPE - B_ROPE), F32)
    cos128 = jnp.concatenate([ones, cos, cos, zeros], axis=-1)
    sin128 = jnp.concatenate([jnp.zeros_like(ones), sin, sin, zeros], axis=-1)
    cost = jnp.swapaxes(jnp.concatenate([cos, cos], axis=-1), 1, 2)
    sint = jnp.swapaxes(jnp.concatenate([sin, sin], axis=-1), 1, 2)
    return cos128, sin128, cost, sint


def kernel(x, p, positions, w_in, w_uq, w_ukv, g_cq, g_ckv, lam_q1, lam_k1, lam_q2, lam_k2, g_diff,
           b_forget, w_out, ln1_g, ln1_b, w_group, b_group, w_erouter, b_erouter, w_gate_e, w_up_e,
           w_down_e, w_ple_gate, w_ple_proj, ln2_g, ln2_b):
    b, s, d = x.shape
    n = b * s
    tq = 512
    tm_prep = 512
    tm_moe = 512
    cos128, sin128, cost, sint = _rope_inputs(positions)
    posf = positions.astype(F32)
    pcol = posf[:, :, None]
    prow = posf[:, None, :]
    slopes = 2.0 ** (-8.0 * jnp.arange(1, A_HEADS + 1, dtype=F32) / A_HEADS)
    slopes = jnp.broadcast_to(
        jnp.pad(slopes.reshape(3, 2), ((0, 0), (0, 6)))[:, :, None], (3, 8, tq))
    row = lambda v: v.reshape(1, -1).astype(F32)

    for i in range(DEPTH):
        wmain, wt, wuq, wukvk, wukvv, bfc = _layer_weights(w_in[i], w_uq[i], w_ukv[i], b_forget[i])
        qa, qb, qc, kta, ktb, ktc, va, vb, vc, cumt = _prep_call(
            x, wmain, wt, wuq, wukvk, wukvv, row(g_cq[i]), row(g_ckv[i]), bfc,
            cos128, sin128, cost, sint, tm=tm_prep)
        lamv = jnp.pad(jnp.stack([lam_q1[i], lam_k1[i], lam_q2[i], lam_k2[i]]).astype(F32),
                       ((0, 4), (0, 128 - A_DIM)))
        g128 = jnp.tile(g_diff[i].astype(F32), 2).reshape(1, 128)
        lam_init = 0.8 - 0.6 * math.exp(-0.3 * i)
        oa = _attn_a_call(qa, kta, va, pcol, prow, slopes, lamv, g128, tq=tq, lam_init=lam_init)
        ob = _attn_b_call(qb, ktb, vb, tq=tq)
        crow = cumt[:, :C_HEADS].reshape(b, 2, 2, s)
        ccol = jnp.swapaxes(crow, 2, 3)
        oc = _attn_c_call(qc, ktc, vc, ccol, crow, tq=tq)

        wo = w_out[i].astype(BF16)
        wr = jnp.concatenate([w_group[i], jnp.moveaxis(w_erouter[i], 0, 1).reshape(d, N_EXPERTS)],
                             axis=1)
        wr = jnp.pad(wr, ((0, 0), (0, ROUTER_COLS - wr.shape[1]))).astype(F32)
        wr1 = wr.astype(BF16)
        wr2 = (wr - wr1.astype(F32)).astype(BF16)
        rbias = jnp.pad(jnp.concatenate([b_group[i], b_erouter[i].reshape(-1)]),
                        (0, ROUTER_COLS - N_GROUPS - N_EXPERTS)).reshape(1, ROUTER_COLS).astype(F32)
        y = _moe_call(
            x.reshape(n, d), oa.reshape(n, MIX_A), ob.reshape(n, MIX_B), oc.reshape(n, MIX_C),
            p[i].reshape(n, PLE_DIM), wo[:MIX_A], wo[MIX_A:MIX_A + MIX_B], wo[MIX_A + MIX_B:],
            row(ln1_g[i]), row(ln1_b[i]), wr1, wr2, rbias, w_gate_e[i].astype(BF16),
            w_up_e[i].astype(BF16), w_down_e[i].astype(BF16), w_ple_gate[i].astype(BF16),
            w_ple_proj[i].astype(BF16), row(ln2_g[i]), row(ln2_b[i]), tm=tm_moe)
        x = y.reshape(b, s, d)
    return x
```

```python
import functools
import math

import jax
import jax.numpy as jnp
from jax import lax
from jax.experimental import pallas as pl
from jax.experimental.pallas import tpu as pltpu

F32 = jnp.float32
BF16 = jnp.bfloat16

D_MODEL = 1024
DEPTH = 2
CHUNK = 64
PLE_DIM = 256

A_HEADS = 6
A_DIM = 32
A_VDIM = 64
B_HEADS = 6
B_Q_RANK = 256
B_KV_RANK = 128
B_NOPE = 64
B_ROPE = 32
B_VDIM = 64
ROPE_THETA = 10000.0
C_HEADS = 4
C_DIM = 64

MIX_A = A_HEADS * A_VDIM
MIX_B = B_HEADS * B_VDIM
MIX_C = C_HEADS * C_DIM
B_PAD = 128

N_GROUPS = 4
EXPERTS_PER_GROUP = 8
N_EXPERTS = N_GROUPS * EXPERTS_PER_GROUP
D_EXPERT = 256
ROUTER_COLS = 128

DEEPNORM_ALPHA = (2 * DEPTH) ** 0.25
LN_EPS = 1e-5
RMS_EPS = 1e-6
NEG_BIG = -1e30
LOG2E = math.log2(math.e)

_OFF = {}
_o = 0
for _name, _w in (("qa", 384), ("ka", 384), ("va", 384), ("cq", 256), ("ckv", 128), ("kr", 32),
                  ("qc", 256), ("kc", 256), ("vc", 256), ("fc", 4)):
    _OFF[_name] = (_o, _o + _w)
    _o += _w

T_KA, T_KC, T_KR, T_KRS, T_FC, T_ROWS = 0, 384, 640, 672, 704, 720
FC_ROWS = T_ROWS - T_FC

VMEM_LIMIT = 48 * 1024 * 1024
DMA_BATCH = 4

NT_DIMS = (((1,), (1,)), ((), ()))


def _dot(a, b):
    return jnp.dot(a, b, preferred_element_type=F32)


def _dot_nt(a, b):
    return lax.dot_general(a, b, NT_DIMS, preferred_element_type=F32)


ROW_TILE = D_MODEL // 128


def _store_row_tiles(ref, v):
    rows = v.shape[0]
    for c in range(ROW_TILE):
        ref[pl.ds(c, rows, stride=ROW_TILE), :] = v[:, c * 128:(c + 1) * 128]


def _load_row_tiles(ref, start, rows):
    return jnp.concatenate(
        [ref[pl.ds(start * ROW_TILE + c, rows, stride=ROW_TILE), :] for c in range(ROW_TILE)],
        axis=1)


def _split3(a):
    a1 = a.astype(BF16)
    r1 = a - a1.astype(F32)
    a2 = r1.astype(BF16)
    a3 = (r1 - a2.astype(F32)).astype(BF16)
    return a1, a2, a3


def _prep_kernel(x_ref, wmain_ref, wt_ref, wuq_ref, wukvk_ref, wukvv_ref, gcq_ref, gckv_ref,
                 bf_ref, cos_ref, sin_ref, cost_ref, sint_ref,
                 qa_ref, qb_ref, qc_ref, kta_ref, ktb_ref, ktc_ref, va_ref, vb_ref, vc_ref,
                 cum_ref, carry_ref, *, tm):
    j = pl.program_id(1)

    @pl.when(j == 0)
    def _():
        carry_ref[...] = jnp.zeros_like(carry_ref)

    xb = x_ref[0].astype(BF16)
    z = _dot(xb, wmain_ref[...])
    zt = _dot_nt(wt_ref[...], xb)

    qa_ref[0] = (z[:, 0:384] * (LOG2E / math.sqrt(A_DIM))).astype(BF16)
    va_ref[0] = z[:, 384:768].astype(BF16)
    kta_ref[0] = zt[T_KA:T_KA + 384].astype(BF16)

    qc_ref[0] = (z[:, 1152:1408] * (LOG2E / math.sqrt(C_DIM))).astype(BF16)
    vc_ref[0] = z[:, 1408:1664].astype(BF16)
    ktc_ref[0] = zt[T_KC:T_KC + 256].astype(BF16)

    fct = zt[T_FC:T_ROWS] + bf_ref[...]
    logf = jnp.minimum(fct, 0.0) - jnp.log(1.0 + jnp.exp(-jnp.abs(fct)))
    r_i = lax.broadcasted_iota(jnp.int32, (tm, tm), 0)
    c_i = lax.broadcasted_iota(jnp.int32, (tm, tm), 1)
    tri = jnp.where(r_i <= c_i, 1.0, 0.0).astype(BF16)
    l1, l2, l3 = _split3(logf)
    cum = _dot(l1, tri) + _dot(l2, tri) + _dot(l3, tri) + carry_ref[:, 0:1]
    cum_ref[0] = cum * LOG2E
    carry_ref[...] = jnp.broadcast_to(cum[:, tm - 1:tm], carry_ref.shape)

    cq = z[:, 768:1024]
    cq_n = cq * lax.rsqrt(jnp.mean(cq * cq, axis=-1, keepdims=True) + RMS_EPS) * gcq_ref[...]
    zq = _dot(cq_n.astype(BF16), wuq_ref[...])
    cosb = cos_ref[0]
    sinb = sin_ref[0]
    scale_b = LOG2E / math.sqrt(B_NOPE + B_ROPE)
    for h in range(B_HEADS):
        lo = h * B_PAD
        qh = zq[:, lo:lo + B_PAD] * cosb + zq[:, 768 + lo:768 + lo + B_PAD] * sinb
        qb_ref[0, :, lo:lo + B_PAD] = (qh * scale_b).astype(BF16)

    ckv = z[:, 1024:1152]
    ckv_n = ckv * lax.rsqrt(jnp.mean(ckv * ckv, axis=-1, keepdims=True) + RMS_EPS) * gckv_ref[...]
    ckv_b = ckv_n.astype(BF16)
    vb_ref[0] = _dot(ckv_b, wukvv_ref[...]).astype(BF16)
    knt = _dot_nt(wukvk_ref[...], ckv_b)
    krt = (zt[T_KR:T_KR + 32] * cost_ref[0] + zt[T_KRS:T_KRS + 32] * sint_ref[0]).astype(BF16)
    zpad = jnp.zeros((B_PAD - B_NOPE - B_ROPE, tm), BF16)
    for h in range(B_HEADS):
        lo = h * B_PAD
        ktb_ref[0, lo:lo + B_NOPE, :] = knt[h * B_NOPE:(h + 1) * B_NOPE].astype(BF16)
        ktb_ref[0, lo + B_NOPE:lo + B_NOPE + B_ROPE, :] = krt
        ktb_ref[0, lo + B_NOPE + B_ROPE:lo + B_PAD, :] = zpad


def _prep_call(x, wmain, wt, wuq, wukvk, wukvv, gcq, gckv, bfc, cos128, sin128, cost, sint, *, tm):
    b, s, d = x.shape
    grid = (b, s // tm)
    full = lambda shape: pl.BlockSpec(shape, lambda bi, j: (0,) * len(shape))
    tok = lambda w: pl.BlockSpec((1, tm, w), lambda bi, j: (bi, j, 0))
    tr = lambda r: pl.BlockSpec((1, r, tm), lambda bi, j: (bi, 0, j))
    out_shape = (
        jax.ShapeDtypeStruct((b, s, 384), BF16),
        jax.ShapeDtypeStruct((b, s, 768), BF16),
        jax.ShapeDtypeStruct((b, s, 256), BF16),
        jax.ShapeDtypeStruct((b, 384, s), BF16),
        jax.ShapeDtypeStruct((b, 768, s), BF16),
        jax.ShapeDtypeStruct((b, 256, s), BF16),
        jax.ShapeDtypeStruct((b, s, 384), BF16),
        jax.ShapeDtypeStruct((b, s, 384), BF16),
        jax.ShapeDtypeStruct((b, s, 256), BF16),
        jax.ShapeDtypeStruct((b, FC_ROWS, s), F32),
    )
    out_specs = (tok(384), tok(768), tok(256), tr(384), tr(768), tr(256), tok(384), tok(384),
                 tok(256), tr(FC_ROWS))
    in_specs = [tok(d), full(wmain.shape), full(wt.shape), full(wuq.shape), full(wukvk.shape),
                full(wukvv.shape), full(gcq.shape), full(gckv.shape), full(bfc.shape),
                tok(128), tok(128), tr(32), tr(32)]
    return pl.pallas_call(
        functools.partial(_prep_kernel, tm=tm),
        grid=grid, in_specs=in_specs, out_specs=out_specs, out_shape=out_shape,
        scratch_shapes=[pltpu.VMEM((FC_ROWS, 128), F32)],
        compiler_params=pltpu.CompilerParams(
            dimension_semantics=("arbitrary", "arbitrary"), vmem_limit_bytes=VMEM_LIMIT),
        name="prep",
    )(x, wmain, wt, wuq, wukvk, wukvv, gcq, gckv, bfc, cos128, sin128, cost, sint)


def _flash(i, tq, n_maps, scores_fn, mask):
    def update(state, j, diag):
        new = []
        for (m, l, acc), (s, v) in zip(state, scores_fn(j)):
            if diag:
                s = jnp.where(mask, s, NEG_BIG)
            m_new = jnp.maximum(m, jnp.max(s, axis=1, keepdims=True))
            alpha = jnp.exp2(m - m_new)
            p = jnp.exp2(s - m_new)
            l_new = alpha * l + jnp.sum(p, axis=1, keepdims=True)
            acc_new = alpha * acc + _dot(p.astype(BF16), v)
            new.append((m_new, l_new, acc_new))
        return tuple(new)

    init = tuple((jnp.full((tq, 1), NEG_BIG, F32), jnp.zeros((tq, 1), F32),
                  jnp.zeros((tq, 128), F32)) for _ in range(n_maps))
    state = update(init, i, True)
    state = lax.fori_loop(0, i, lambda j, st: update(st, j, False), state)
    return [acc * (1.0 / l) for (_, l, acc) in state]


def _chunk_mask(tq):
    r = lax.broadcasted_iota(jnp.int32, (tq, tq), 0)
    c = lax.broadcasted_iota(jnp.int32, (tq, tq), 1)
    return (c // CHUNK) <= (r // CHUNK)


def _causal_mask(tq):
    r = lax.broadcasted_iota(jnp.int32, (tq, tq), 0)
    c = lax.broadcasted_iota(jnp.int32, (tq, tq), 1)
    return c <= r


def _attn_a_kernel(q_ref, kt_ref, v_ref, pcol_ref, prow_ref, slope_ref, lam_ref, g_ref, o_ref,
                   *, tq, lam_init):
    i = pl.program_id(2)
    q = q_ref[0]
    lane = lax.broadcasted_iota(jnp.int32, (tq, 128), 1)
    pt = pcol_ref[0]
    mask = _chunk_mask(tq)
    lam_v = lam_ref[...]
    lam = (jnp.exp(jnp.sum(lam_v[0:1] * lam_v[1:2], axis=1, keepdims=True))
           - jnp.exp(jnp.sum(lam_v[2:3] * lam_v[3:4], axis=1, keepdims=True)) + lam_init)

    def koff(j):
        return pl.multiple_of(j * tq, tq)

    qms = [jnp.where((lane >= 32 * mi) & (lane < 32 * mi + 32), q, jnp.zeros_like(q))
           for mi in range(4)]
    nslopes = [-LOG2E * slope_ref[0, hh:hh + 1, :] for hh in range(2)]

    def scores_fn(j):
        kt = kt_ref[0, :, pl.ds(koff(j), tq)]
        ps = prow_ref[0, :, pl.ds(koff(j), tq)]
        v = v_ref[0, pl.ds(koff(j), tq), :]
        dist = jnp.abs(pt - ps)
        out = []
        for hh in range(2):
            bias = nslopes[hh] * dist
            out += [(_dot(qms[2 * hh + mm], kt) + bias, v) for mm in range(2)]
        return out

    o = _flash(i, tq, 4, scores_fn, mask)
    outs = [o[0] - lam * o[1], o[2] - lam * o[3]]
    o = jnp.where(lane < 64, outs[0], outs[1])
    sq = o * o
    ms0 = jnp.sum(jnp.where(lane < 64, sq, 0.0), axis=1, keepdims=True) * (1.0 / A_VDIM)
    ms1 = jnp.sum(jnp.where(lane >= 64, sq, 0.0), axis=1, keepdims=True) * (1.0 / A_VDIM)
    ms = jnp.where(lane < 64, ms0, ms1)
    y = o * lax.rsqrt(ms + RMS_EPS) * g_ref[...] * (1.0 - lam_init)
    o_ref[0] = y.astype(BF16)


def _attn_b_kernel(q_ref, kt_ref, v_ref, o_ref, *, tq):
    i = pl.program_id(2)
    lane = lax.broadcasted_iota(jnp.int32, (tq, 128), 1)
    mask = _chunk_mask(tq)

    def koff(j):
        return pl.multiple_of(j * tq, tq)

    qs = [q_ref[0, :, hh * B_PAD:(hh + 1) * B_PAD] for hh in range(2)]

    def scores_fn(j):
        v = v_ref[0, pl.ds(koff(j), tq), :]
        return [(_dot(qs[hh], kt_ref[0, hh * B_PAD:(hh + 1) * B_PAD, pl.ds(koff(j), tq)]), v)
                for hh in range(2)]

    outs = _flash(i, tq, 2, scores_fn, mask)
    o_ref[0] = jnp.where(lane < 64, outs[0], outs[1]).astype(BF16)


def _attn_c_kernel(q_ref, kt_ref, v_ref, ccol_ref, crow_ref, o_ref, *, tq):
    i = pl.program_id(2)
    q = q_ref[0]
    lane = lax.broadcasted_iota(jnp.int32, (tq, 128), 1)
    mask = _causal_mask(tq)

    def koff(j):
        return pl.multiple_of(j * tq, tq)

    qms = [jnp.where((lane >= 64 * hh) & (lane < 64 * hh + 64), q, jnp.zeros_like(q))
           for hh in range(2)]
    cts = [ccol_ref[0, 0, :, hh:hh + 1] for hh in range(2)]

    def scores_fn(j):
        kt = kt_ref[0, :, pl.ds(koff(j), tq)]
        v = v_ref[0, pl.ds(koff(j), tq), :]
        out = []
        for hh in range(2):
            cs = crow_ref[0, 0, hh:hh + 1, pl.ds(koff(j), tq)]
            out.append((_dot(qms[hh], kt) + (cts[hh] - cs), v))
        return out

    outs = _flash(i, tq, 2, scores_fn, mask)
    o_ref[0] = jnp.where(lane < 64, outs[0], outs[1]).astype(BF16)


def _attn_params():
    return pltpu.CompilerParams(
        dimension_semantics=("arbitrary", "arbitrary", "arbitrary"), vmem_limit_bytes=VMEM_LIMIT)


def _attn_a_call(qa, kta, va, pcol, prow, slopes, lamv, g128, *, tq, lam_init):
    b, s, _ = qa.shape
    grid = (b, 3, s // tq)
    return pl.pallas_call(
        functools.partial(_attn_a_kernel, tq=tq, lam_init=lam_init),
        grid=grid,
        in_specs=[
            pl.BlockSpec((1, tq, 128), lambda bi, c, i: (bi, i, c)),
            pl.BlockSpec((1, 128, s), lambda bi, c, i: (bi, c, 0)),
            pl.BlockSpec((1, s, 128), lambda bi, c, i: (bi, 0, c)),
            pl.BlockSpec((1, tq, 1), lambda bi, c, i: (bi, i, 0)),
            pl.BlockSpec((1, 1, s), lambda bi, c, i: (bi, 0, 0)),
            pl.BlockSpec((1, 8, tq), lambda bi, c, i: (c, 0, 0)),
            pl.BlockSpec((8, 128), lambda bi, c, i: (0, 0)),
            pl.BlockSpec((1, 128), lambda bi, c, i: (0, 0)),
        ],
        out_specs=pl.BlockSpec((1, tq, 128), lambda bi, c, i: (bi, i, c)),
        out_shape=jax.ShapeDtypeStruct((b, s, MIX_A), BF16),
        compiler_params=_attn_params(), name="attn_a",
    )(qa, kta, va, pcol, prow, slopes, lamv, g128)


def _attn_b_call(qb, ktb, vb, *, tq):
    b, s, _ = qb.shape
    grid = (b, 3, s // tq)
    return pl.pallas_call(
        functools.partial(_attn_b_kernel, tq=tq),
        grid=grid,
        in_specs=[
            pl.BlockSpec((1, tq, 2 * B_PAD), lambda bi, c, i: (bi, i, c)),
            pl.BlockSpec((1, 2 * B_PAD, s), lambda bi, c, i: (bi, c, 0)),
            pl.BlockSpec((1, s, 128), lambda bi, c, i: (bi, 0, c)),
        ],
        out_specs=pl.BlockSpec((1, tq, 128), lambda bi, c, i: (bi, i, c)),
        out_shape=jax.ShapeDtypeStruct((b, s, MIX_B), BF16),
        compiler_params=_attn_params(), name="attn_b",
    )(qb, ktb, vb)


def _attn_c_call(qc, ktc, vc, ccol, crow, *, tq):
    b, s, _ = qc.shape
    grid = (b, 2, s // tq)
    return pl.pallas_call(
        functools.partial(_attn_c_kernel, tq=tq),
        grid=grid,
        in_specs=[
            pl.BlockSpec((1, tq, 128), lambda bi, c, i: (bi, i, c)),
            pl.BlockSpec((1, 128, s), lambda bi, c, i: (bi, c, 0)),
            pl.BlockSpec((1, s, 128), lambda bi, c, i: (bi, 0, c)),
            pl.BlockSpec((1, 1, tq, 2), lambda bi, c, i: (bi, c, i, 0)),
            pl.BlockSpec((1, 1, 2, s), lambda bi, c, i: (bi, c, 0, 0)),
        ],
        out_specs=pl.BlockSpec((1, tq, 128), lambda bi, c, i: (bi, i, c)),
        out_shape=jax.ShapeDtypeStruct((b, s, MIX_C), BF16),
        compiler_params=_attn_params(), name="attn_c",
    )(qc, ktc, vc, ccol, crow)


def _layer_norm(v, g, b):
    mu = jnp.mean(v, axis=-1, keepdims=True)
    c = v - mu
    var = jnp.mean(c * c, axis=-1, keepdims=True)
    return c * lax.rsqrt(var + LN_EPS) * g + b


def _route(h, wr1_ref, wr2_ref, rbias_ref):
    tm = h.shape[0]
    h1 = h.astype(BF16)
    h2 = (h - h1.astype(F32)).astype(BF16)
    w1 = wr1_ref[...]
    logits = _dot(h1, w1) + _dot(h2, w1) + _dot(h1, wr2_ref[...]) + rbias_ref[...]
    col = lax.broadcasted_iota(jnp.int32, (tm, ROUTER_COLS), 1)
    colf = col.astype(F32)
    big = float(ROUTER_COLS)
    gmask = col < N_GROUPS
    gmax = jnp.max(jnp.where(gmask, logits, NEG_BIG), axis=1, keepdims=True)
    g_idx = jnp.min(jnp.where(gmask & (logits == gmax), colf, big), axis=1, keepdims=True)
    g_w = 1.0 / jnp.sum(jnp.where(gmask, jnp.exp(logits - gmax), 0.0), axis=1, keepdims=True)
    lo = N_GROUPS + g_idx * EXPERTS_PER_GROUP
    emask = (colf >= lo) & (colf < lo + EXPERTS_PER_GROUP)
    top1 = jnp.max(jnp.where(emask, logits, NEG_BIG), axis=1, keepdims=True)
    idx1 = jnp.min(jnp.where(emask & (logits == top1), colf, big), axis=1, keepdims=True)
    emask2 = emask & (colf != idx1)
    top2 = jnp.max(jnp.where(emask2, logits, NEG_BIG), axis=1, keepdims=True)
    idx2 = jnp.min(jnp.where(emask2 & (logits == top2), colf, big), axis=1, keepdims=True)
    e2 = jnp.exp(top2 - top1)
    w_1 = g_w / (1.0 + e2)
    w_2 = g_w * e2 / (1.0 + e2)
    return colf, idx1, idx2, w_1, w_2


def _post_kernel(x_ref, oa_ref, ob_ref, oc_ref, p_ref, woa_ref, wob_ref, woc_ref, g1_ref, b1_ref,
                 wr1_ref, wr2_ref, rbias_ref, wpg_ref, wpp_ref,
                 h_ref, base_ref, ri_ref, rw_ref, cnt_ref, carry_ref):
    i = pl.program_id(0)
    tm = x_ref.shape[0]

    @pl.when(i == 0)
    def _():
        carry_ref[...] = jnp.zeros_like(carry_ref)

    mix = (_dot(oa_ref[...], woa_ref[...]) + _dot(ob_ref[...], wob_ref[...])
           + _dot(oc_ref[...], woc_ref[...]))
    h = _layer_norm(DEEPNORM_ALPHA * x_ref[...] + mix, g1_ref[...], b1_ref[...])
    _store_row_tiles(h_ref, h)
    gl = _dot(h.astype(BF16), wpg_ref[...])
    ple = (1.0 / (1.0 + jnp.exp(-gl))) * _dot(p_ref[...].astype(BF16), wpp_ref[...])
    base_ref[...] = DEEPNORM_ALPHA * h + ple

    colf, idx1, idx2, w_1, w_2 = _route(h, wr1_ref, wr2_ref, rbias_ref)
    hit1 = colf == idx1
    hit2 = colf == idx2
    onehot = jnp.where(hit1 | hit2, 1.0, 0.0)
    r_i = lax.broadcasted_iota(jnp.int32, (tm, tm), 0)
    c_i = lax.broadcasted_iota(jnp.int32, (tm, tm), 1)
    before = jnp.where(c_i < r_i, 1.0, 0.0).astype(BF16)
    seen = _dot(before, onehot.astype(BF16)) + carry_ref[0:1, :]
    rank1 = jnp.sum(jnp.where(hit1, seen, 0.0), axis=1, keepdims=True)
    rank2 = jnp.sum(jnp.where(hit2, seen, 0.0), axis=1, keepdims=True)
    total = carry_ref[0:1, :] + jnp.sum(onehot, axis=0, keepdims=True)
    carry_ref[...] = jnp.broadcast_to(total, carry_ref.shape)
    cnt_ref[...] = jnp.broadcast_to(total, cnt_ref.shape)

    lane8 = lax.broadcasted_iota(jnp.int32, (tm, 8), 1)
    ri = jnp.where(lane8 == 0, idx1 - N_GROUPS,
                   jnp.where(lane8 == 1, idx2 - N_GROUPS,
                             jnp.where(lane8 == 2, rank1, jnp.where(lane8 == 3, rank2, 0.0))))
    ri_ref[...] = ri.astype(jnp.int32)
    rw_ref[...] = jnp.where(lane8 == 0, w_1, jnp.where(lane8 == 1, w_2, 0.0))


def _post_call(x2, oa, ob, oc, p2, woa, wob, woc, g1, b1, wr1, wr2, rbias, wpg, wpp, *, tm):
    n, d = x2.shape
    tok = lambda w: pl.BlockSpec((tm, w), lambda t: (t, 0))
    full = lambda a: pl.BlockSpec(a.shape, lambda t: (0,) * a.ndim)
    return pl.pallas_call(
        _post_kernel,
        grid=(n // tm,),
        in_specs=[tok(d), tok(MIX_A), tok(MIX_B), tok(MIX_C), tok(PLE_DIM), full(woa), full(wob),
                  full(woc), full(g1), full(b1), full(wr1), full(wr2), full(rbias), full(wpg),
                  full(wpp)],
        out_specs=(pl.BlockSpec((tm * ROW_TILE, 128), lambda t: (t, 0)), tok(d), tok(8), tok(8),
                   pl.BlockSpec((8, ROUTER_COLS), lambda t: (0, 0))),
        out_shape=(jax.ShapeDtypeStruct((n * ROW_TILE, 128), F32), jax.ShapeDtypeStruct((n, d), F32),
                   jax.ShapeDtypeStruct((n, 8), jnp.int32), jax.ShapeDtypeStruct((n, 8), F32),
                   jax.ShapeDtypeStruct((8, ROUTER_COLS), F32)),
        scratch_shapes=[pltpu.VMEM((8, ROUTER_COLS), F32)],
        compiler_params=pltpu.CompilerParams(
            dimension_semantics=("arbitrary",), vmem_limit_bytes=VMEM_LIMIT),
        name="post",
    )(x2, oa, ob, oc, p2, woa, wob, woc, g1, b1, wr1, wr2, rbias, wpg, wpp)


def _row_copy(src_ref, src_row, dst_ref, dst_row, sem):
    src = pl.ds(pl.multiple_of(src_row * ROW_TILE, ROW_TILE), ROW_TILE)
    dst = pl.ds(pl.multiple_of(dst_row * ROW_TILE, ROW_TILE), ROW_TILE)
    return pltpu.make_async_copy(src_ref.at[src], dst_ref.at[dst], sem)


def _slot_kernel(ri_ref, seg_ref, slot_ref):
    tm = ri_ref.shape[0]
    ri = ri_ref[...]
    colf = lax.broadcasted_iota(jnp.int32, (tm, ROUTER_COLS), 1).astype(F32)
    seg = seg_ref[...]
    lane8 = lax.broadcasted_iota(jnp.int32, (tm, 8), 1)
    out = jnp.zeros((tm, 8), F32)
    for k in range(2):
        e = ri[:, k:k + 1].astype(F32)
        start = jnp.sum(jnp.where(colf == e, seg, 0.0), axis=1, keepdims=True)
        out = jnp.where(lane8 == k, start + ri[:, 2 + k:3 + k].astype(F32), out)
    slot_ref[...] = out.astype(jnp.int32)


def _slot_call(ri, seg_row, *, tm):
    n = ri.shape[0]
    return pl.pallas_call(
        _slot_kernel,
        grid=(n // tm,),
        in_specs=[pl.BlockSpec((tm, 8), lambda t: (t, 0)),
                  pl.BlockSpec((1, ROUTER_COLS), lambda t: (0, 0))],
        out_specs=pl.BlockSpec((tm, 8), lambda t: (t, 0)),
        out_shape=jax.ShapeDtypeStruct((n, 8), jnp.int32),
        compiler_params=pltpu.CompilerParams(dimension_semantics=("arbitrary",)),
        name="slots",
    )(ri, seg_row)


def _dispatch_kernel(seg_ref, pc_ref, nv_ref, slot_ref, h_hbm, xs_hbm, zbuf, sem, zsem, *, tb, tg):
    i = pl.program_id(0)
    n_tiles = xs_hbm.shape[0] // (tg * ROW_TILE)

    def unused_tile_zero_copy(jt):
        start = pl.multiple_of(jt * (tg * ROW_TILE), tg * ROW_TILE)
        return pltpu.make_async_copy(zbuf, xs_hbm.at[pl.ds(start, tg * ROW_TILE)], zsem)

    def for_unused_tiles(fn):
        def body(jt, c):
            fn(unused_tile_zero_copy(jt))
            return c
        lax.fori_loop(nv_ref[0], n_tiles, body, 0)

    def tail_zero_copy(e):
        start = pl.multiple_of((seg_ref[e] + pc_ref[e] - tg) * ROW_TILE, tg * ROW_TILE)
        return pltpu.make_async_copy(zbuf, xs_hbm.at[pl.ds(start, tg * ROW_TILE)], zsem)

    @pl.when(i == 0)
    def _():
        zbuf[...] = jnp.zeros_like(zbuf)
        for e in range(N_EXPERTS):
            @pl.when(pc_ref[e] > 0)
            def _():
                tail_zero_copy(e).start()
        for e in range(N_EXPERTS):
            @pl.when(pc_ref[e] > 0)
            def _():
                tail_zero_copy(e).wait()
        for_unused_tiles(lambda cp: cp.start())
        for_unused_tiles(lambda cp: cp.wait())

    def body(g, c):
        t0 = g * DMA_BATCH
        dst = [slot_ref[2 * t0 + u] for u in range(2 * DMA_BATCH)]
        for u in range(2 * DMA_BATCH):
            _row_copy(h_hbm, i * tb + t0 + u // 2, xs_hbm, dst[u], sem).start()
        return c

    lax.fori_loop(0, tb // DMA_BATCH, body, 0)
    n_words = 2 * tb * ROW_TILE
    pltpu.make_async_copy(h_hbm.at[pl.ds(0, n_words)], xs_hbm.at[pl.ds(0, n_words)], sem).wait()


def _dispatch_call(seg, pc, nvalid, slot_flat, h, *, rows, tb, tg):
    n = h.shape[0] // ROW_TILE
    smem_blk = pl.BlockSpec((2 * tb,), lambda i, seg, pc, nv: (i,), memory_space=pltpu.SMEM)
    grid_spec = pltpu.PrefetchScalarGridSpec(
        num_scalar_prefetch=3, grid=(n // tb,),
        in_specs=[smem_blk, pl.BlockSpec(memory_space=pl.ANY)],
        out_specs=pl.BlockSpec(memory_space=pl.ANY),
        scratch_shapes=[pltpu.VMEM((tg * ROW_TILE, 128), F32), pltpu.SemaphoreType.DMA(()),
                        pltpu.SemaphoreType.DMA(())])
    return pl.pallas_call(
        functools.partial(_dispatch_kernel, tb=tb, tg=tg),
        grid_spec=grid_spec,
        out_shape=jax.ShapeDtypeStruct((rows * ROW_TILE, 128), F32),
        compiler_params=pltpu.CompilerParams(
            dimension_semantics=("arbitrary",), vmem_limit_bytes=VMEM_LIMIT),
        name="dispatch",
    )(seg, pc, nvalid, slot_flat, h)


def _experts_kernel(te_ref, nv_ref, xs_ref, wg_ref, wu_ref, wd_ref, ys_ref, wgb_ref, wub_ref,
                    wdb_ref):
    j = pl.program_id(0)

    @pl.when(j < nv_ref[0])
    def _():
        prev = te_ref[jnp.maximum(j - 1, 0)]

        @pl.when((j == 0) | (te_ref[j] != prev))
        def _():
            wgb_ref[...] = wg_ref[0].astype(BF16)
            wub_ref[...] = wu_ref[0].astype(BF16)
            wdb_ref[...] = wd_ref[0].astype(BF16)

        tg = xs_ref.shape[0] // ROW_TILE
        xb = _load_row_tiles(xs_ref, 0, tg).astype(BF16)
        gate = _dot(xb, wgb_ref[...])
        up = _dot(xb, wub_ref[...])
        act = gate * (1.0 / (1.0 + jnp.exp(-gate))) * up
        _store_row_tiles(ys_ref, _dot(act.astype(BF16), wdb_ref[...]))

    @pl.when(j >= nv_ref[0])
    def _():
        ys_ref[...] = jnp.zeros_like(ys_ref)


def _experts_call(tile_e, nvalid, xs, wg, wu, wd, *, tg):
    rows = xs.shape[0] // ROW_TILE
    last = lambda j, te, nv: jnp.minimum(j, nv[0] - 1)
    row_blk = pl.BlockSpec((tg * ROW_TILE, 128), lambda j, te, nv: (last(j, te, nv), 0))
    out_blk = pl.BlockSpec((tg * ROW_TILE, 128), lambda j, te, nv: (j, 0))
    w_blk = lambda a: pl.BlockSpec((1,) + a.shape[1:], lambda j, te, nv: (te[last(j, te, nv)], 0, 0))
    grid_spec = pltpu.PrefetchScalarGridSpec(
        num_scalar_prefetch=2, grid=(rows // tg,),
        in_specs=[row_blk, w_blk(wg), w_blk(wu), w_blk(wd)],
        out_specs=out_blk,
        scratch_shapes=[pltpu.VMEM(wg.shape[1:], BF16), pltpu.VMEM(wu.shape[1:], BF16),
                        pltpu.VMEM(wd.shape[1:], BF16)])
    return pl.pallas_call(
        _experts_kernel,
        grid_spec=grid_spec,
        out_shape=jax.ShapeDtypeStruct((rows * ROW_TILE, 128), F32),
        compiler_params=pltpu.CompilerParams(
            dimension_semantics=("arbitrary",), vmem_limit_bytes=VMEM_LIMIT),
        name="experts",
    )(tile_e, nvalid, xs, wg, wu, wd)


def _combine_kernel(cur_ref, nxt_ref, base_ref, rw_ref, g2_ref, b2_ref, ys_hbm, y_ref, ybuf, sem,
                    *, tc):
    i = pl.program_id(0)
    n_steps = pl.num_programs(0)
    slot = i % 2

    def gathered(slot_):
        return pltpu.make_async_copy(ys_hbm.at[pl.ds(0, 2 * tc * ROW_TILE)], ybuf.at[slot_],
                                     sem.at[slot_])

    def issue(rows_ref, slot_):
        def body(g, c):
            t0 = g * DMA_BATCH
            src = [rows_ref[2 * t0 + u] for u in range(2 * DMA_BATCH)]
            for u in range(2 * DMA_BATCH):
                _row_copy(ys_hbm, src[u], ybuf.at[slot_], (u % 2) * tc + t0 + u // 2,
                          sem.at[slot_]).start()
            return c
        lax.fori_loop(0, tc // DMA_BATCH, body, 0)

    @pl.when(i == 0)
    def _():
        issue(cur_ref, 0)

    @pl.when(i + 1 < n_steps)
    def _():
        issue(nxt_ref, 1 - slot)

    gathered(slot).wait()
    y1 = _load_row_tiles(ybuf.at[slot], 0, tc)
    y2 = _load_row_tiles(ybuf.at[slot], tc, tc)
    rw = rw_ref[...]
    moe = rw[:, 0:1] * y1 + rw[:, 1:2] * y2
    y_ref[...] = _layer_norm(base_ref[...] + moe, g2_ref[...], b2_ref[...])


def _combine_call(slot_flat, base, rw, g2, b2, ys, *, tc):
    n, d = base.shape
    n_steps = n // tc
    cur = pl.BlockSpec((2 * tc,), lambda i: (i,), memory_space=pltpu.SMEM)
    nxt = pl.BlockSpec((2 * tc,), lambda i: (jnp.minimum(i + 1, n_steps - 1),),
                       memory_space=pltpu.SMEM)
    tok = lambda w: pl.BlockSpec((tc, w), lambda i: (i, 0))
    full = lambda a: pl.BlockSpec(a.shape, lambda i: (0,) * a.ndim)
    return pl.pallas_call(
        functools.partial(_combine_kernel, tc=tc),
        grid=(n_steps,),
        in_specs=[cur, nxt, tok(d), tok(8), full(g2), full(b2), pl.BlockSpec(memory_space=pl.ANY)],
        out_specs=tok(d),
        out_shape=jax.ShapeDtypeStruct((n, d), F32),
        scratch_shapes=[pltpu.VMEM((2, 2 * tc * ROW_TILE, 128), F32),
                        pltpu.SemaphoreType.DMA((2,))],
        compiler_params=pltpu.CompilerParams(
            dimension_semantics=("arbitrary",), vmem_limit_bytes=VMEM_LIMIT),
        name="combine",
    )(slot_flat, slot_flat, base, rw, g2, b2, ys)


def _expert_segments(counts, tg, n_tiles):
    cnt = counts[0, N_GROUPS:N_GROUPS + N_EXPERTS].astype(jnp.int32)
    pc = ((cnt + tg - 1) // tg) * tg
    seg_end = jnp.cumsum(pc)
    seg = seg_end - pc
    nvalid = (seg_end[-1] // tg).reshape(1)
    tile_e = jnp.searchsorted(seg_end, jnp.arange(n_tiles, dtype=jnp.int32) * tg, side="right")
    tile_e = jnp.minimum(tile_e, N_EXPERTS - 1).astype(jnp.int32)
    return seg.astype(jnp.int32), pc.astype(jnp.int32), tile_e, nvalid.astype(jnp.int32)


def _cols(w, name):
    lo, hi = _OFF[name]
    return w[:, lo:hi]


def _rot_pairs(w):
    return jnp.concatenate([-w[..., 16:32], w[..., 0:16]], axis=-1)


def _layer_weights(w_in, w_uq, w_ukv, b_forget):
    wmain = jnp.concatenate([_cols(w_in, n) for n in ("qa", "va", "cq", "ckv", "qc", "vc")], axis=1)
    kr = _cols(w_in, "kr")
    fc_pad = jnp.pad(_cols(w_in, "fc"), ((0, 0), (0, FC_ROWS - C_HEADS)))
    wt = jnp.concatenate([_cols(w_in, "ka"), _cols(w_in, "kc"), kr, _rot_pairs(kr), fc_pad], axis=1).T
    wq = w_uq.reshape(B_Q_RANK, B_HEADS, B_NOPE + B_ROPE)
    zpad = jnp.zeros((B_Q_RANK, B_HEADS, B_PAD - B_NOPE - B_ROPE), w_uq.dtype)
    plain = jnp.concatenate([wq, zpad], axis=-1).reshape(B_Q_RANK, B_HEADS * B_PAD)
    rot = jnp.concatenate([jnp.zeros_like(wq[..., :B_NOPE]), _rot_pairs(wq[..., B_NOPE:]), zpad],
                          axis=-1).reshape(B_Q_RANK, B_HEADS * B_PAD)
    wuq = jnp.concatenate([plain, rot], axis=1)
    wkv = w_ukv.reshape(B_KV_RANK, B_HEADS, B_NOPE + B_VDIM)
    wukvk = wkv[..., :B_NOPE].reshape(B_KV_RANK, B_HEADS * B_NOPE).T
    wukvv = wkv[..., B_NOPE:].reshape(B_KV_RANK, B_HEADS * B_VDIM)
    bfc = jnp.pad(b_forget, (0, FC_ROWS - C_HEADS)).reshape(FC_ROWS, 1)
    return (wmain.astype(BF16), wt.astype(BF16), wuq.astype(BF16), wukvk.astype(BF16),
            wukvv.astype(BF16), bfc.astype(F32))


def _rope_inputs(positions):
    half = B_ROPE // 2
    inv = ROPE_THETA ** (-jnp.arange(half, dtype=F32) / half)
    ang = positions.astype(F32)[..., None] * inv
    cos, sin = jnp.cos(ang), jnp.sin(ang)
    b, s, _ = cos.shape
    ones = jnp.ones((b, s, B_NOPE), F32)
    zeros = jnp.zeros((b, s, B_PAD - B_NOPE - B_ROPE), F32)
    cos128 = jnp.concatenate([ones, cos, cos, zeros], axis=-1)
    sin128 = jnp.concatenate([jnp.zeros_like(ones), sin, sin, zeros], axis=-1)
    cost = jnp.swapaxes(jnp.concatenate([cos, cos], axis=-1), 1, 2)
    sint = jnp.swapaxes(jnp.concatenate([sin, sin], axis=-1), 1, 2)
    return cos128, sin128, cost, sint


def kernel(x, p, positions, w_in, w_uq, w_ukv, g_cq, g_ckv, lam_q1, lam_k1, lam_q2, lam_k2, g_diff,
           b_forget, w_out, ln1_g, ln1_b, w_group, b_group, w_erouter, b_erouter, w_gate_e, w_up_e,
           w_down_e, w_ple_gate, w_ple_proj, ln2_g, ln2_b):
    b, s, d = x.shape
    n = b * s
    tq = 512
    tm_prep = 512
    tm_moe = 512
    tg = 512
    rows = 2 * n + N_EXPERTS * tg
    tb = min(2048, n // 2)
    tc = 512
    cos128, sin128, cost, sint = _rope_inputs(positions)
    posf = positions.astype(F32)
    pcol = posf[:, :, None]
    prow = posf[:, None, :]
    slopes = 2.0 ** (-8.0 * jnp.arange(1, A_HEADS + 1, dtype=F32) / A_HEADS)
    slopes = jnp.broadcast_to(
        jnp.pad(slopes.reshape(3, 2), ((0, 0), (0, 6)))[:, :, None], (3, 8, tq))
    row = lambda v: v.reshape(1, -1).astype(F32)

    for i in range(DEPTH):
        wmain, wt, wuq, wukvk, wukvv, bfc = _layer_weights(w_in[i], w_uq[i], w_ukv[i], b_forget[i])
        qa, qb, qc, kta, ktb, ktc, va, vb, vc, cumt = _prep_call(
            x, wmain, wt, wuq, wukvk, wukvv, row(g_cq[i]), row(g_ckv[i]), bfc,
            cos128, sin128, cost, sint, tm=tm_prep)
        lamv = jnp.pad(jnp.stack([lam_q1[i], lam_k1[i], lam_q2[i], lam_k2[i]]).astype(F32),
                       ((0, 4), (0, 128 - A_DIM)))
        g128 = jnp.tile(g_diff[i].astype(F32), 2).reshape(1, 128)
        lam_init = 0.8 - 0.6 * math.exp(-0.3 * i)
        oa = _attn_a_call(qa, kta, va, pcol, prow, slopes, lamv, g128, tq=tq, lam_init=lam_init)
        ob = _attn_b_call(qb, ktb, vb, tq=tq)
        crow = cumt[:, :C_HEADS].reshape(b, 2, 2, s)
        ccol = jnp.swapaxes(crow, 2, 3)
        oc = _attn_c_call(qc, ktc, vc, ccol, crow, tq=tq)

        wo = w_out[i].astype(BF16)
        wr = jnp.concatenate([w_group[i], jnp.moveaxis(w_erouter[i], 0, 1).reshape(d, N_EXPERTS)],
                             axis=1)
        wr = jnp.pad(wr, ((0, 0), (0, ROUTER_COLS - wr.shape[1]))).astype(F32)
        wr1 = wr.astype(BF16)
        wr2 = (wr - wr1.astype(F32)).astype(BF16)
        rbias = jnp.pad(jnp.concatenate([b_group[i], b_erouter[i].reshape(-1)]),
                        (0, ROUTER_COLS - N_GROUPS - N_EXPERTS)).reshape(1, ROUTER_COLS).astype(F32)
        h, base, ri, rw, counts = _post_call(
            x.reshape(n, d), oa.reshape(n, MIX_A), ob.reshape(n, MIX_B), oc.reshape(n, MIX_C),
            p[i].reshape(n, PLE_DIM), wo[:MIX_A], wo[MIX_A:MIX_A + MIX_B], wo[MIX_A + MIX_B:],
            row(ln1_g[i]), row(ln1_b[i]), wr1, wr2, rbias, w_ple_gate[i].astype(BF16),
            w_ple_proj[i].astype(BF16), tm=tm_moe)
        seg, pc, tile_e, nvalid = _expert_segments(counts, tg, rows // tg)
        seg_row = jnp.pad(seg.astype(F32), (0, ROUTER_COLS - N_EXPERTS)).reshape(1, ROUTER_COLS)
        slot_flat = _slot_call(ri, seg_row, tm=tm_moe)[:, 0:2].reshape(-1)
        xs = _dispatch_call(seg, pc, nvalid, slot_flat, h, rows=rows, tb=tb, tg=tg)
        ys = _experts_call(tile_e, nvalid, xs, w_gate_e[i], w_up_e[i], w_down_e[i], tg=tg)
        y = _combine_call(slot_flat, base, rw, row(ln2_g[i]), row(ln2_b[i]), ys, tc=tc)
        x = y.reshape(b, s, d)
    return x
```

```python
import functools
import math

import jax
import jax.numpy as jnp
from jax import lax
from jax.experimental import pallas as pl
from jax.experimental.pallas import tpu as pltpu

F32 = jnp.float32
BF16 = jnp.bfloat16

D_MODEL = 1024
DEPTH = 2
CHUNK = 64
PLE_DIM = 256

A_HEADS = 6
A_DIM = 32
A_VDIM = 64
B_HEADS = 6
B_Q_RANK = 256
B_KV_RANK = 128
B_NOPE = 64
B_ROPE = 32
B_VDIM = 64
ROPE_THETA = 10000.0
C_HEADS = 4
C_DIM = 64

MIX_A = A_HEADS * A_VDIM
MIX_B = B_HEADS * B_VDIM
MIX_C = C_HEADS * C_DIM
B_PAD = 128

N_GROUPS = 4
EXPERTS_PER_GROUP = 8
N_EXPERTS = N_GROUPS * EXPERTS_PER_GROUP
D_EXPERT = 256
ROUTER_COLS = 128

DEEPNORM_ALPHA = (2 * DEPTH) ** 0.25
LN_EPS = 1e-5
RMS_EPS = 1e-6
NEG_BIG = -1e30
LOG2E = math.log2(math.e)

_OFF = {}
_o = 0
for _name, _w in (("qa", 384), ("ka", 384), ("va", 384), ("cq", 256), ("ckv", 128), ("kr", 32),
                  ("qc", 256), ("kc", 256), ("vc", 256), ("fc", 4)):
    _OFF[_name] = (_o, _o + _w)
    _o += _w

T_KA, T_KC, T_KR, T_KRS, T_FC, T_ROWS = 0, 384, 640, 672, 704, 720
FC_ROWS = T_ROWS - T_FC

VMEM_LIMIT = 48 * 1024 * 1024
DMA_BATCH = 4

NT_DIMS = (((1,), (1,)), ((), ()))


def _dot(a, b):
    return jnp.dot(a, b, preferred_element_type=F32)


def _dot_nt(a, b):
    return lax.dot_general(a, b, NT_DIMS, preferred_element_type=F32)


ROW_TILE = D_MODEL // 128


def _store_row_tiles(ref, v):
    rows = v.shape[0]
    for c in range(ROW_TILE):
        ref[pl.ds(c, rows, stride=ROW_TILE), :] = v[:, c * 128:(c + 1) * 128]


def _load_row_tiles(ref, start, rows):
    return jnp.concatenate(
        [ref[pl.ds(start * ROW_TILE + c, rows, stride=ROW_TILE), :] for c in range(ROW_TILE)],
        axis=1)


def _split3(a):
    a1 = a.astype(BF16)
    r1 = a - a1.astype(F32)
    a2 = r1.astype(BF16)
    a3 = (r1 - a2.astype(F32)).astype(BF16)
    return a1, a2, a3


def _prep_kernel(x_ref, wmain_ref, wt_ref, wuq_ref, wukvk_ref, wukvv_ref, gcq_ref, gckv_ref,
                 bf_ref, cos_ref, sin_ref, cost_ref, sint_ref,
                 qa_ref, qb_ref, qc_ref, kta_ref, ktb_ref, ktc_ref, va_ref, vb_ref, vc_ref,
                 cum_ref, carry_ref, *, tm):
    j = pl.program_id(1)

    @pl.when(j == 0)
    def _():
        carry_ref[...] = jnp.zeros_like(carry_ref)

    xb = x_ref[0].astype(BF16)
    z = _dot(xb, wmain_ref[...])
    zt = _dot_nt(wt_ref[...], xb)

    qa_ref[0] = (z[:, 0:384] * (LOG2E / math.sqrt(A_DIM))).astype(BF16)
    va_ref[0] = z[:, 384:768].astype(BF16)
    kta_ref[0] = zt[T_KA:T_KA + 384].astype(BF16)

    qc_ref[0] = (z[:, 1152:1408] * (LOG2E / math.sqrt(C_DIM))).astype(BF16)
    vc_ref[0] = z[:, 1408:1664].astype(BF16)
    ktc_ref[0] = zt[T_KC:T_KC + 256].astype(BF16)

    fct = zt[T_FC:T_ROWS] + bf_ref[...]
    logf = jnp.minimum(fct, 0.0) - jnp.log(1.0 + jnp.exp(-jnp.abs(fct)))
    r_i = lax.broadcasted_iota(jnp.int32, (tm, tm), 0)
    c_i = lax.broadcasted_iota(jnp.int32, (tm, tm), 1)
    tri = jnp.where(r_i <= c_i, 1.0, 0.0).astype(BF16)
    l1, l2, l3 = _split3(logf)
    cum = _dot(l1, tri) + _dot(l2, tri) + _dot(l3, tri) + carry_ref[:, 0:1]
    cum_ref[0] = cum * LOG2E
    carry_ref[...] = jnp.broadcast_to(cum[:, tm - 1:tm], carry_ref.shape)

    cq = z[:, 768:1024]
    cq_n = cq * lax.rsqrt(jnp.mean(cq * cq, axis=-1, keepdims=True) + RMS_EPS) * gcq_ref[...]
    zq = _dot(cq_n.astype(BF16), wuq_ref[...])
    cosb = cos_ref[0]
    sinb = sin_ref[0]
    scale_b = LOG2E / math.sqrt(B_NOPE + B_ROPE)
    for h in range(B_HEADS):
        lo = h * B_PAD
        qh = zq[:, lo:lo + B_PAD] * cosb + zq[:, 768 + lo:768 + lo + B_PAD] * sinb
        qb_ref[0, :, lo:lo + B_PAD] = (qh * scale_b).astype(BF16)

    ckv = z[:, 1024:1152]
    ckv_n = ckv * lax.rsqrt(jnp.mean(ckv * ckv, axis=-1, keepdims=True) + RMS_EPS) * gckv_ref[...]
    ckv_b = ckv_n.astype(BF16)
    vb_ref[0] = _dot(ckv_b, wukvv_ref[...]).astype(BF16)
    knt = _dot_nt(wukvk_ref[...], ckv_b)
    krt = (zt[T_KR:T_KR + 32] * cost_ref[0] + zt[T_KRS:T_KRS + 32] * sint_ref[0]).astype(BF16)
    zpad = jnp.zeros((B_PAD - B_NOPE - B_ROPE, tm), BF16)
    for h in range(B_HEADS):
        lo = h * B_PAD
        ktb_ref[0, lo:lo + B_NOPE, :] = knt[h * B_NOPE:(h + 1) * B_NOPE].astype(BF16)
        ktb_ref[0, lo + B_NOPE:lo + B_NOPE + B_ROPE, :] = krt
        ktb_ref[0, lo + B_NOPE + B_ROPE:lo + B_PAD, :] = zpad


def _prep_call(x, wmain, wt, wuq, wukvk, wukvv, gcq, gckv, bfc, cos128, sin128, cost, sint, *, tm):
    b, s, d = x.shape
    grid = (b, s // tm)
    full = lambda shape: pl.BlockSpec(shape, lambda bi, j: (0,) * len(shape))
    tok = lambda w: pl.BlockSpec((1, tm, w), lambda bi, j: (bi, j, 0))
    tr = lambda r: pl.BlockSpec((1, r, tm), lambda bi, j: (bi, 0, j))
    out_shape = (
        jax.ShapeDtypeStruct((b, s, 384), BF16),
        jax.ShapeDtypeStruct((b, s, 768), BF16),
        jax.ShapeDtypeStruct((b, s, 256), BF16),
        jax.ShapeDtypeStruct((b, 384, s), BF16),
        jax.ShapeDtypeStruct((b, 768, s), BF16),
        jax.ShapeDtypeStruct((b, 256, s), BF16),
        jax.ShapeDtypeStruct((b, s, 384), BF16),
        jax.ShapeDtypeStruct((b, s, 384), BF16),
        jax.ShapeDtypeStruct((b, s, 256), BF16),
        jax.ShapeDtypeStruct((b, FC_ROWS, s), F32),
    )
    out_specs = (tok(384), tok(768), tok(256), tr(384), tr(768), tr(256), tok(384), tok(384),
                 tok(256), tr(FC_ROWS))
    in_specs = [tok(d), full(wmain.shape), full(wt.shape), full(wuq.shape), full(wukvk.shape),
                full(wukvv.shape), full(gcq.shape), full(gckv.shape), full(bfc.shape),
                tok(128), tok(128), tr(32), tr(32)]
    return pl.pallas_call(
        functools.partial(_prep_kernel, tm=tm),
        grid=grid, in_specs=in_specs, out_specs=out_specs, out_shape=out_shape,
        scratch_shapes=[pltpu.VMEM((FC_ROWS, 128), F32)],
        compiler_params=pltpu.CompilerParams(
            dimension_semantics=("arbitrary", "arbitrary"), vmem_limit_bytes=VMEM_LIMIT),
        name="prep",
    )(x, wmain, wt, wuq, wukvk, wukvv, gcq, gckv, bfc, cos128, sin128, cost, sint)


def _flash(i, tq, n_maps, scores_fn, mask):
    def update(state, j, diag):
        new = []
        for (m, l, acc), (s, v) in zip(state, scores_fn(j)):
            if diag:
                s = jnp.where(mask, s, NEG_BIG)
            m_new = jnp.maximum(m, jnp.max(s, axis=1, keepdims=True))
            alpha = jnp.exp2(m - m_new)
            p = jnp.exp2(s - m_new)
            l_new = alpha * l + jnp.sum(p, axis=1, keepdims=True)
            acc_new = alpha * acc + _dot(p.astype(BF16), v)
            new.append((m_new, l_new, acc_new))
        return tuple(new)

    init = tuple((jnp.full((tq, 1), NEG_BIG, F32), jnp.zeros((tq, 1), F32),
                  jnp.zeros((tq, 128), F32)) for _ in range(n_maps))
    state = update(init, i, True)
    state = lax.fori_loop(0, i, lambda j, st: update(st, j, False), state)
    return [acc * (1.0 / l) for (_, l, acc) in state]


def _chunk_mask(tq):
    r = lax.broadcasted_iota(jnp.int32, (tq, tq), 0)
    c = lax.broadcasted_iota(jnp.int32, (tq, tq), 1)
    return (c // CHUNK) <= (r // CHUNK)


def _causal_mask(tq):
    r = lax.broadcasted_iota(jnp.int32, (tq, tq), 0)
    c = lax.broadcasted_iota(jnp.int32, (tq, tq), 1)
    return c <= r


def _attn_a_kernel(q_ref, kt_ref, v_ref, pcol_ref, prow_ref, slope_ref, lam_ref, g_ref, o_ref,
                   *, tq, lam_init):
    i = pl.program_id(2)
    q = q_ref[0]
    lane = lax.broadcasted_iota(jnp.int32, (tq, 128), 1)
    pt = pcol_ref[0]
    mask = _chunk_mask(tq)
    lam_v = lam_ref[...]
    lam = (jnp.exp(jnp.sum(lam_v[0:1] * lam_v[1:2], axis=1, keepdims=True))
           - jnp.exp(jnp.sum(lam_v[2:3] * lam_v[3:4], axis=1, keepdims=True)) + lam_init)

    def koff(j):
        return pl.multiple_of(j * tq, tq)

    qms = [jnp.where((lane >= 32 * mi) & (lane < 32 * mi + 32), q, jnp.zeros_like(q))
           for mi in range(4)]
    nslopes = [-LOG2E * slope_ref[0, hh:hh + 1, :] for hh in range(2)]

    def scores_fn(j):
        kt = kt_ref[0, :, pl.ds(koff(j), tq)]
        ps = prow_ref[0, :, pl.ds(koff(j), tq)]
        v = v_ref[0, pl.ds(koff(j), tq), :]
        dist = jnp.abs(pt - ps)
        out = []
        for hh in range(2):
            bias = nslopes[hh] * dist
            out += [(_dot(qms[2 * hh + mm], kt) + bias, v) for mm in range(2)]
        return out

    o = _flash(i, tq, 4, scores_fn, mask)
    outs = [o[0] - lam * o[1], o[2] - lam * o[3]]
    o = jnp.where(lane < 64, outs[0], outs[1])
    sq = o * o
    ms0 = jnp.sum(jnp.where(lane < 64, sq, 0.0), axis=1, keepdims=True) * (1.0 / A_VDIM)
    ms1 = jnp.sum(jnp.where(lane >= 64, sq, 0.0), axis=1, keepdims=True) * (1.0 / A_VDIM)
    ms = jnp.where(lane < 64, ms0, ms1)
    y = o * lax.rsqrt(ms + RMS_EPS) * g_ref[...] * (1.0 - lam_init)
    o_ref[0] = y.astype(BF16)


def _attn_b_kernel(q_ref, kt_ref, v_ref, o_ref, *, tq):
    i = pl.program_id(2)
    lane = lax.broadcasted_iota(jnp.int32, (tq, 128), 1)
    mask = _chunk_mask(tq)

    def koff(j):
        return pl.multiple_of(j * tq, tq)

    qs = [q_ref[0, :, hh * B_PAD:(hh + 1) * B_PAD] for hh in range(2)]

    def scores_fn(j):
        v = v_ref[0, pl.ds(koff(j), tq), :]
        return [(_dot(qs[hh], kt_ref[0, hh * B_PAD:(hh + 1) * B_PAD, pl.ds(koff(j), tq)]), v)
                for hh in range(2)]

    outs = _flash(i, tq, 2, scores_fn, mask)
    o_ref[0] = jnp.where(lane < 64, outs[0], outs[1]).astype(BF16)


def _attn_c_kernel(q_ref, kt_ref, v_ref, ccol_ref, crow_ref, o_ref, *, tq):
    i = pl.program_id(2)
    q = q_ref[0]
    lane = lax.broadcasted_iota(jnp.int32, (tq, 128), 1)
    mask = _causal_mask(tq)

    def koff(j):
        return pl.multiple_of(j * tq, tq)

    qms = [jnp.where((lane >= 64 * hh) & (lane < 64 * hh + 64), q, jnp.zeros_like(q))
           for hh in range(2)]
    cts = [ccol_ref[0, 0, :, hh:hh + 1] for hh in range(2)]

    def scores_fn(j):
        kt = kt_ref[0, :, pl.ds(koff(j), tq)]
        v = v_ref[0, pl.ds(koff(j), tq), :]
        out = []
        for hh in range(2):
            cs = crow_ref[0, 0, hh:hh + 1, pl.ds(koff(j), tq)]
            out.append((_dot(qms[hh], kt) + (cts[hh] - cs), v))
        return out

    outs = _flash(i, tq, 2, scores_fn, mask)
    o_ref[0] = jnp.where(lane < 64, outs[0], outs[1]).astype(BF16)


def _attn_params():
    return pltpu.CompilerParams(
        dimension_semantics=("arbitrary", "arbitrary", "arbitrary"), vmem_limit_bytes=VMEM_LIMIT)


def _attn_a_call(qa, kta, va, pcol, prow, slopes, lamv, g128, *, tq, lam_init):
    b, s, _ = qa.shape
    grid = (b, 3, s // tq)
    return pl.pallas_call(
        functools.partial(_attn_a_kernel, tq=tq, lam_init=lam_init),
        grid=grid,
        in_specs=[
            pl.BlockSpec((1, tq, 128), lambda bi, c, i: (bi, i, c)),
            pl.BlockSpec((1, 128, s), lambda bi, c, i: (bi, c, 0)),
            pl.BlockSpec((1, s, 128), lambda bi, c, i: (bi, 0, c)),
            pl.BlockSpec((1, tq, 1), lambda bi, c, i: (bi, i, 0)),
            pl.BlockSpec((1, 1, s), lambda bi, c, i: (bi, 0, 0)),
            pl.BlockSpec((1, 8, tq), lambda bi, c, i: (c, 0, 0)),
            pl.BlockSpec((8, 128), lambda bi, c, i: (0, 0)),
            pl.BlockSpec((1, 128), lambda bi, c, i: (0, 0)),
        ],
        out_specs=pl.BlockSpec((1, tq, 128), lambda bi, c, i: (bi, i, c)),
        out_shape=jax.ShapeDtypeStruct((b, s, MIX_A), BF16),
        compiler_params=_attn_params(), name="attn_a",
    )(qa, kta, va, pcol, prow, slopes, lamv, g128)


def _attn_b_call(qb, ktb, vb, *, tq):
    b, s, _ = qb.shape
    grid = (b, 3, s // tq)
    return pl.pallas_call(
        functools.partial(_attn_b_kernel, tq=tq),
        grid=grid,
        in_specs=[
            pl.BlockSpec((1, tq, 2 * B_PAD), lambda bi, c, i: (bi, i, c)),
            pl.BlockSpec((1, 2 * B_PAD, s), lambda bi, c, i: (bi, c, 0)),
            pl.BlockSpec((1, s, 128), lambda bi, c, i: (bi, 0, c)),
        ],
        out_specs=pl.BlockSpec((1, tq, 128), lambda bi, c, i: (bi, i, c)),
        out_shape=jax.ShapeDtypeStruct((b, s, MIX_B), BF16),
        compiler_params=_attn_params(), name="attn_b",
    )(qb, ktb, vb)


def _attn_c_call(qc, ktc, vc, ccol, crow, *, tq):
    b, s, _ = qc.shape
    grid = (b, 2, s // tq)
    return pl.pallas_call(
        functools.partial(_attn_c_kernel, tq=tq),
        grid=grid,
        in_specs=[
            pl.BlockSpec((1, tq, 128), lambda bi, c, i: (bi, i, c)),
            pl.BlockSpec((1, 128, s), lambda bi, c, i: (bi, c, 0)),
            pl.BlockSpec((1, s, 128), lambda bi, c, i: (bi, 0, c)),
            pl.BlockSpec((1, 1, tq, 2), lambda bi, c, i: (bi, c, i, 0)),
            pl.BlockSpec((1, 1, 2, s), lambda bi, c, i: (bi, c, 0, 0)),
        ],
        out_specs=pl.BlockSpec((1, tq, 128), lambda bi, c, i: (bi, i, c)),
        out_shape=jax.ShapeDtypeStruct((b, s, MIX_C), BF16),
        compiler_params=_attn_params(), name="attn_c",
    )(qc, ktc, vc, ccol, crow)


def _layer_norm(v, g, b):
    mu = jnp.mean(v, axis=-1, keepdims=True)
    c = v - mu
    var = jnp.mean(c * c, axis=-1, keepdims=True)
    return c * lax.rsqrt(var + LN_EPS) * g + b


def _route(h, wr1_ref, wr2_ref, rbias_ref):
    tm = h.shape[0]
    h1 = h.astype(BF16)
    h2 = (h - h1.astype(F32)).astype(BF16)
    w1 = wr1_ref[...]
    logits = _dot(h1, w1) + _dot(h2, w1) + _dot(h1, wr2_ref[...]) + rbias_ref[...]
    col = lax.broadcasted_iota(jnp.int32, (tm, ROUTER_COLS), 1)
    colf = col.astype(F32)
    big = float(ROUTER_COLS)
    gmask = col < N_GROUPS
    gmax = jnp.max(jnp.where(gmask, logits, NEG_BIG), axis=1, keepdims=True)
    g_idx = jnp.min(jnp.where(gmask & (logits == gmax), colf, big), axis=1, keepdims=True)
    g_w = 1.0 / jnp.sum(jnp.where(gmask, jnp.exp(logits - gmax), 0.0), axis=1, keepdims=True)
    lo = N_GROUPS + g_idx * EXPERTS_PER_GROUP
    emask = (colf >= lo) & (colf < lo + EXPERTS_PER_GROUP)
    top1 = jnp.max(jnp.where(emask, logits, NEG_BIG), axis=1, keepdims=True)
    idx1 = jnp.min(jnp.where(emask & (logits == top1), colf, big), axis=1, keepdims=True)
    emask2 = emask & (colf != idx1)
    top2 = jnp.max(jnp.where(emask2, logits, NEG_BIG), axis=1, keepdims=True)
    idx2 = jnp.min(jnp.where(emask2 & (logits == top2), colf, big), axis=1, keepdims=True)
    e2 = jnp.exp(top2 - top1)
    w_1 = g_w / (1.0 + e2)
    w_2 = g_w * e2 / (1.0 + e2)
    return colf, idx1, idx2, w_1, w_2


def _post_kernel(x_ref, oa_ref, ob_ref, oc_ref, p_ref, woa_ref, wob_ref, woc_ref, g1_ref, b1_ref,
                 wr1_ref, wr2_ref, rbias_ref, wpg_ref, wpp_ref,
                 h_ref, base_ref, ri_ref, rw_ref, cnt_ref, carry_ref):
    i = pl.program_id(0)
    tm = x_ref.shape[0]

    @pl.when(i == 0)
    def _():
        carry_ref[...] = jnp.zeros_like(carry_ref)

    mix = (_dot(oa_ref[...], woa_ref[...]) + _dot(ob_ref[...], wob_ref[...])
           + _dot(oc_ref[...], woc_ref[...]))
    h = _layer_norm(DEEPNORM_ALPHA * x_ref[...] + mix, g1_ref[...], b1_ref[...])
    _store_row_tiles(h_ref, h)
    gl = _dot(h.astype(BF16), wpg_ref[...])
    ple = (1.0 / (1.0 + jnp.exp(-gl))) * _dot(p_ref[...].astype(BF16), wpp_ref[...])
    base_ref[...] = DEEPNORM_ALPHA * h + ple

    colf, idx1, idx2, w_1, w_2 = _route(h, wr1_ref, wr2_ref, rbias_ref)
    hit1 = colf == idx1
    hit2 = colf == idx2
    onehot = jnp.where(hit1 | hit2, 1.0, 0.0)
    r_i = lax.broadcasted_iota(jnp.int32, (tm, tm), 0)
    c_i = lax.broadcasted_iota(jnp.int32, (tm, tm), 1)
    before = jnp.where(c_i < r_i, 1.0, 0.0).astype(BF16)
    seen = _dot(before, onehot.astype(BF16)) + carry_ref[0:1, :]
    rank1 = jnp.sum(jnp.where(hit1, seen, 0.0), axis=1, keepdims=True)
    rank2 = jnp.sum(jnp.where(hit2, seen, 0.0), axis=1, keepdims=True)
    total = carry_ref[0:1, :] + jnp.sum(onehot, axis=0, keepdims=True)
    carry_ref[...] = jnp.broadcast_to(total, carry_ref.shape)
    cnt_ref[...] = jnp.broadcast_to(total, cnt_ref.shape)

    lane8 = lax.broadcasted_iota(jnp.int32, (tm, 8), 1)
    ri = jnp.where(lane8 == 0, idx1 - N_GROUPS,
                   jnp.where(lane8 == 1, idx2 - N_GROUPS,
                             jnp.where(lane8 == 2, rank1, jnp.where(lane8 == 3, rank2, 0.0))))
    ri_ref[...] = ri.astype(jnp.int32)
    rw_ref[...] = jnp.where(lane8 == 0, w_1, jnp.where(lane8 == 1, w_2, 0.0))


def _post_call(x2, oa, ob, oc, p2, woa, wob, woc, g1, b1, wr1, wr2, rbias, wpg, wpp, *, tm):
    n, d = x2.shape
    tok = lambda w: pl.BlockSpec((tm, w), lambda t: (t, 0))
    full = lambda a: pl.BlockSpec(a.shape, lambda t: (0,) * a.ndim)
    return pl.pallas_call(
        _post_kernel,
        grid=(n // tm,),
        in_specs=[tok(d), tok(MIX_A), tok(MIX_B), tok(MIX_C), tok(PLE_DIM), full(woa), full(wob),
                  full(woc), full(g1), full(b1), full(wr1), full(wr2), full(rbias), full(wpg),
                  full(wpp)],
        out_specs=(pl.BlockSpec((tm * ROW_TILE, 128), lambda t: (t, 0)), tok(d), tok(8), tok(8),
                   pl.BlockSpec((8, ROUTER_COLS), lambda t: (0, 0))),
        out_shape=(jax.ShapeDtypeStruct((n * ROW_TILE, 128), F32), jax.ShapeDtypeStruct((n, d), F32),
                   jax.ShapeDtypeStruct((n, 8), jnp.int32), jax.ShapeDtypeStruct((n, 8), F32),
                   jax.ShapeDtypeStruct((8, ROUTER_COLS), F32)),
        scratch_shapes=[pltpu.VMEM((8, ROUTER_COLS), F32)],
        compiler_params=pltpu.CompilerParams(
            dimension_semantics=("arbitrary",), vmem_limit_bytes=VMEM_LIMIT),
        name="post",
    )(x2, oa, ob, oc, p2, woa, wob, woc, g1, b1, wr1, wr2, rbias, wpg, wpp)


def _row_copy(src_ref, src_row, dst_ref, dst_row, sem):
    src = pl.ds(pl.multiple_of(src_row * ROW_TILE, ROW_TILE), ROW_TILE)
    dst = pl.ds(pl.multiple_of(dst_row * ROW_TILE, ROW_TILE), ROW_TILE)
    return pltpu.make_async_copy(src_ref.at[src], dst_ref.at[dst], sem)


def _slot_kernel(ri_ref, seg_ref, slot_ref):
    tm = ri_ref.shape[0]
    ri = ri_ref[...]
    colf = lax.broadcasted_iota(jnp.int32, (tm, ROUTER_COLS), 1).astype(F32)
    seg = seg_ref[...]
    lane8 = lax.broadcasted_iota(jnp.int32, (tm, 8), 1)
    out = jnp.zeros((tm, 8), F32)
    for k in range(2):
        e = ri[:, k:k + 1].astype(F32)
        start = jnp.sum(jnp.where(colf == e, seg, 0.0), axis=1, keepdims=True)
        out = jnp.where(lane8 == k, start + ri[:, 2 + k:3 + k].astype(F32), out)
    slot_ref[...] = out.astype(jnp.int32)


def _slot_call(ri, seg_row, *, tm):
    n = ri.shape[0]
    return pl.pallas_call(
        _slot_kernel,
        grid=(n // tm,),
        in_specs=[pl.BlockSpec((tm, 8), lambda t: (t, 0)),
                  pl.BlockSpec((1, ROUTER_COLS), lambda t: (0, 0))],
        out_specs=pl.BlockSpec((tm, 8), lambda t: (t, 0)),
        out_shape=jax.ShapeDtypeStruct((n, 8), jnp.int32),
        compiler_params=pltpu.CompilerParams(dimension_semantics=("arbitrary",)),
        name="slots",
    )(ri, seg_row)


def _dispatch_kernel(seg_ref, pc_ref, nv_ref, slot_ref, h_ref, xs_hbm, zbuf, sem, zsem, *, tb, tg):
    i = pl.program_id(0)
    n_tiles = xs_hbm.shape[0] // (tg * ROW_TILE)

    def unused_tile_zero_copy(jt):
        start = pl.multiple_of(jt * (tg * ROW_TILE), tg * ROW_TILE)
        return pltpu.make_async_copy(zbuf, xs_hbm.at[pl.ds(start, tg * ROW_TILE)], zsem)

    def for_unused_tiles(fn):
        def body(jt, c):
            fn(unused_tile_zero_copy(jt))
            return c
        lax.fori_loop(nv_ref[0], n_tiles, body, 0)

    def tail_zero_copy(e):
        start = pl.multiple_of((seg_ref[e] + pc_ref[e] - tg) * ROW_TILE, tg * ROW_TILE)
        return pltpu.make_async_copy(zbuf, xs_hbm.at[pl.ds(start, tg * ROW_TILE)], zsem)

    @pl.when(i == 0)
    def _():
        zbuf[...] = jnp.zeros_like(zbuf)
        for e in range(N_EXPERTS):
            @pl.when(pc_ref[e] > 0)
            def _():
                tail_zero_copy(e).start()
        for e in range(N_EXPERTS):
            @pl.when(pc_ref[e] > 0)
            def _():
                tail_zero_copy(e).wait()
        for_unused_tiles(lambda cp: cp.start())
        for_unused_tiles(lambda cp: cp.wait())

    def body(g, c):
        t0 = g * DMA_BATCH
        dst = [slot_ref[2 * t0 + u] for u in range(2 * DMA_BATCH)]
        for u in range(2 * DMA_BATCH):
            _row_copy(h_ref, t0 + u // 2, xs_hbm, dst[u], sem).start()
        return c

    lax.fori_loop(0, tb // DMA_BATCH, body, 0)
    for _ in range(2):
        pltpu.make_async_copy(h_ref, xs_hbm.at[pl.ds(0, tb * ROW_TILE)], sem).wait()


def _dispatch_call(seg, pc, nvalid, slot_flat, h, *, rows, tb, tg):
    n = h.shape[0] // ROW_TILE
    smem_blk = pl.BlockSpec((2 * tb,), lambda i, seg, pc, nv: (i,), memory_space=pltpu.SMEM)
    grid_spec = pltpu.PrefetchScalarGridSpec(
        num_scalar_prefetch=3, grid=(n // tb,),
        in_specs=[smem_blk,
                  pl.BlockSpec((tb * ROW_TILE, 128), lambda i, seg, pc, nv: (i, 0))],
        out_specs=pl.BlockSpec(memory_space=pl.ANY),
        scratch_shapes=[pltpu.VMEM((tg * ROW_TILE, 128), F32), pltpu.SemaphoreType.DMA(()),
                        pltpu.SemaphoreType.DMA(())])
    return pl.pallas_call(
        functools.partial(_dispatch_kernel, tb=tb, tg=tg),
        grid_spec=grid_spec,
        out_shape=jax.ShapeDtypeStruct((rows * ROW_TILE, 128), F32),
        compiler_params=pltpu.CompilerParams(
            dimension_semantics=("arbitrary",), vmem_limit_bytes=VMEM_LIMIT),
        name="dispatch",
    )(seg, pc, nvalid, slot_flat, h)


def _experts_kernel(te_ref, nv_ref, xs_ref, wg_ref, wu_ref, wd_ref, ys_ref, wgb_ref, wub_ref,
                    wdb_ref):
    j = pl.program_id(0)

    @pl.when(j < nv_ref[0])
    def _():
        prev = te_ref[jnp.maximum(j - 1, 0)]

        @pl.when((j == 0) | (te_ref[j] != prev))
        def _():
            wgb_ref[...] = wg_ref[0].astype(BF16)
            wub_ref[...] = wu_ref[0].astype(BF16)
            wdb_ref[...] = wd_ref[0].astype(BF16)

        tg = xs_ref.shape[0] // ROW_TILE
        xb = _load_row_tiles(xs_ref, 0, tg).astype(BF16)
        gate = _dot(xb, wgb_ref[...])
        up = _dot(xb, wub_ref[...])
        act = gate * (1.0 / (1.0 + jnp.exp(-gate))) * up
        _store_row_tiles(ys_ref, _dot(act.astype(BF16), wdb_ref[...]))

    @pl.when(j >= nv_ref[0])
    def _():
        ys_ref[...] = jnp.zeros_like(ys_ref)


def _experts_call(tile_e, nvalid, xs, wg, wu, wd, *, tg):
    rows = xs.shape[0] // ROW_TILE
    last = lambda j, te, nv: jnp.minimum(j, nv[0] - 1)
    row_blk = pl.BlockSpec((tg * ROW_TILE, 128), lambda j, te, nv: (last(j, te, nv), 0))
    out_blk = pl.BlockSpec((tg * ROW_TILE, 128), lambda j, te, nv: (j, 0))
    w_blk = lambda a: pl.BlockSpec((1,) + a.shape[1:], lambda j, te, nv: (te[last(j, te, nv)], 0, 0))
    grid_spec = pltpu.PrefetchScalarGridSpec(
        num_scalar_prefetch=2, grid=(rows // tg,),
        in_specs=[row_blk, w_blk(wg), w_blk(wu), w_blk(wd)],
        out_specs=out_blk,
        scratch_shapes=[pltpu.VMEM(wg.shape[1:], BF16), pltpu.VMEM(wu.shape[1:], BF16),
                        pltpu.VMEM(wd.shape[1:], BF16)])
    return pl.pallas_call(
        _experts_kernel,
        grid_spec=grid_spec,
        out_shape=jax.ShapeDtypeStruct((rows * ROW_TILE, 128), F32),
        compiler_params=pltpu.CompilerParams(
            dimension_semantics=("arbitrary",), vmem_limit_bytes=VMEM_LIMIT),
        name="experts",
    )(tile_e, nvalid, xs, wg, wu, wd)


def _combine_kernel(cur_ref, nxt_ref, base_ref, rw_ref, g2_ref, b2_ref, ys_hbm, y_ref, ybuf, sem,
                    *, tc):
    i = pl.program_id(0)
    n_steps = pl.num_programs(0)
    slot = i % 2

    def gathered(slot_):
        return pltpu.make_async_copy(ys_hbm.at[pl.ds(0, 2 * tc * ROW_TILE)], ybuf.at[slot_],
                                     sem.at[slot_])

    def issue(rows_ref, slot_):
        def body(g, c):
            t0 = g * DMA_BATCH
            src = [rows_ref[2 * t0 + u] for u in range(2 * DMA_BATCH)]
            for u in range(2 * DMA_BATCH):
                _row_copy(ys_hbm, src[u], ybuf.at[slot_], (u % 2) * tc + t0 + u // 2,
                          sem.at[slot_]).start()
            return c
        lax.fori_loop(0, tc // DMA_BATCH, body, 0)

    @pl.when(i == 0)
    def _():
        issue(cur_ref, 0)

    @pl.when(i + 1 < n_steps)
    def _():
        issue(nxt_ref, 1 - slot)

    gathered(slot).wait()
    y1 = _load_row_tiles(ybuf.at[slot], 0, tc)
    y2 = _load_row_tiles(ybuf.at[slot], tc, tc)
    rw = rw_ref[...]
    moe = rw[:, 0:1] * y1 + rw[:, 1:2] * y2
    y_ref[...] = _layer_norm(base_ref[...] + moe, g2_ref[...], b2_ref[...])


def _combine_call(slot_flat, base, rw, g2, b2, ys, *, tc):
    n, d = base.shape
    n_steps = n // tc
    cur = pl.BlockSpec((2 * tc,), lambda i: (i,), memory_space=pltpu.SMEM)
    nxt = pl.BlockSpec((2 * tc,), lambda i: (jnp.minimum(i + 1, n_steps - 1),),
                       memory_space=pltpu.SMEM)
    tok = lambda w: pl.BlockSpec((tc, w), lambda i: (i, 0))
    full = lambda a: pl.BlockSpec(a.shape, lambda i: (0,) * a.ndim)
    return pl.pallas_call(
        functools.partial(_combine_kernel, tc=tc),
        grid=(n_steps,),
        in_specs=[cur, nxt, tok(d), tok(8), full(g2), full(b2), pl.BlockSpec(memory_space=pl.ANY)],
        out_specs=tok(d),
        out_shape=jax.ShapeDtypeStruct((n, d), F32),
        scratch_shapes=[pltpu.VMEM((2, 2 * tc * ROW_TILE, 128), F32),
                        pltpu.SemaphoreType.DMA((2,))],
        compiler_params=pltpu.CompilerParams(
            dimension_semantics=("arbitrary",), vmem_limit_bytes=VMEM_LIMIT),
        name="combine",
    )(slot_flat, slot_flat, base, rw, g2, b2, ys)


def _expert_segments(counts, tg, n_tiles):
    cnt = counts[0, N_GROUPS:N_GROUPS + N_EXPERTS].astype(jnp.int32)
    pc = ((cnt + tg - 1) // tg) * tg
    seg_end = jnp.cumsum(pc)
    seg = seg_end - pc
    nvalid = (seg_end[-1] // tg).reshape(1)
    tile_e = jnp.searchsorted(seg_end, jnp.arange(n_tiles, dtype=jnp.int32) * tg, side="right")
    tile_e = jnp.minimum(tile_e, N_EXPERTS - 1).astype(jnp.int32)
    return seg.astype(jnp.int32), pc.astype(jnp.int32), tile_e, nvalid.astype(jnp.int32)


def _cols(w, name):
    lo, hi = _OFF[name]
    return w[:, lo:hi]


def _rot_pairs(w):
    return jnp.concatenate([-w[..., 16:32], w[..., 0:16]], axis=-1)


def _layer_weights(w_in, w_uq, w_ukv, b_forget):
    wmain = jnp.concatenate([_cols(w_in, n) for n in ("qa", "va", "cq", "ckv", "qc", "vc")], axis=1)
    kr = _cols(w_in, "kr")
    fc_pad = jnp.pad(_cols(w_in, "fc"), ((0, 0), (0, FC_ROWS - C_HEADS)))
    wt = jnp.concatenate([_cols(w_in, "ka"), _cols(w_in, "kc"), kr, _rot_pairs(kr), fc_pad], axis=1).T
    wq = w_uq.reshape(B_Q_RANK, B_HEADS, B_NOPE + B_ROPE)
    zpad = jnp.zeros((B_Q_RANK, B_HEADS, B_PAD - B_NOPE - B_ROPE), w_uq.dtype)
    plain = jnp.concatenate([wq, zpad], axis=-1).reshape(B_Q_RANK, B_HEADS * B_PAD)
    rot = jnp.concatenate([jnp.zeros_like(wq[..., :B_NOPE]), _rot_pairs(wq[..., B_NOPE:]), zpad],
                          axis=-1).reshape(B_Q_RANK, B_HEADS * B_PAD)
    wuq = jnp.concatenate([plain, rot], axis=1)
    wkv = w_ukv.reshape(B_KV_RANK, B_HEADS, B_NOPE + B_VDIM)
    wukvk = wkv[..., :B_NOPE].reshape(B_KV_RANK, B_HEADS * B_NOPE).T
    wukvv = wkv[..., B_NOPE:].reshape(B_KV_RANK, B_HEADS * B_VDIM)
    bfc = jnp.pad(b_forget, (0, FC_ROWS - C_HEADS)).reshape(FC_ROWS, 1)
    return (wmain.astype(BF16), wt.astype(BF16), wuq.astype(BF16), wukvk.astype(BF16),
            wukvv.astype(BF16), bfc.astype(F32))


def _rope_inputs(positions):
    half = B_ROPE // 2
    inv = ROPE_THETA ** (-jnp.arange(half, dtype=F32) / half)
    ang = positions.astype(F32)[..., None] * inv
    cos, sin = jnp.cos(ang), jnp.sin(ang)
    b, s, _ = cos.shape
    ones = jnp.ones((b, s, B_NOPE), F32)
    zeros = jnp.zeros((b, s, B_PAD - B_NOPE - B_ROPE), F32)
    cos128 = jnp.concatenate([ones, cos, cos, zeros], axis=-1)
    sin128 = jnp.concatenate([jnp.zeros_like(ones), sin, sin, zeros], axis=-1)
    cost = jnp.swapaxes(jnp.concatenate([cos, cos], axis=-1), 1, 2)
    sint = jnp.swapaxes(jnp.concatenate([sin, sin], axis=-1), 1, 2)
    return cos128, sin128, cost, sint


def kernel(x, p, positions, w_in, w_uq, w_ukv, g_cq, g_ckv, lam_q1, lam_k1, lam_q2, lam_k2, g_diff,
           b_forget, w_out, ln1_g, ln1_b, w_group, b_group, w_erouter, b_erouter, w_gate_e, w_up_e,
           w_down_e, w_ple_gate, w_ple_proj, ln2_g, ln2_b):
    b, s, d = x.shape
    n = b * s
    tq = 512
    tm_prep = 512
    tm_moe = 512
    tg = 512
    rows = 2 * n + N_EXPERTS * tg
    tb = 512
    tc = 512
    cos128, sin128, cost, sint = _rope_inputs(positions)
    posf = positions.astype(F32)
    pcol = posf[:, :, None]
    prow = posf[:, None, :]
    slopes = 2.0 ** (-8.0 * jnp.arange(1, A_HEADS + 1, dtype=F32) / A_HEADS)
    slopes = jnp.broadcast_to(
        jnp.pad(slopes.reshape(3, 2), ((0, 0), (0, 6)))[:, :, None], (3, 8, tq))
    row = lambda v: v.reshape(1, -1).astype(F32)

    for i in range(DEPTH):
        wmain, wt, wuq, wukvk, wukvv, bfc = _layer_weights(w_in[i], w_uq[i], w_ukv[i], b_forget[i])
        qa, qb, qc, kta, ktb, ktc, va, vb, vc, cumt = _prep_call(
            x, wmain, wt, wuq, wukvk, wukvv, row(g_cq[i]), row(g_ckv[i]), bfc,
            cos128, sin128, cost, sint, tm=tm_prep)
        lamv = jnp.pad(jnp.stack([lam_q1[i], lam_k1[i], lam_q2[i], lam_k2[i]]).astype(F32),
                       ((0, 4), (0, 128 - A_DIM)))
        g128 = jnp.tile(g_diff[i].astype(F32), 2).reshape(1, 128)
        lam_init = 0.8 - 0.6 * math.exp(-0.3 * i)
        oa = _attn_a_call(qa, kta, va, pcol, prow, slopes, lamv, g128, tq=tq, lam_init=lam_init)
        ob = _attn_b_call(qb, ktb, vb, tq=tq)
        crow = cumt[:, :C_HEADS].reshape(b, 2, 2, s)
        ccol = jnp.swapaxes(crow, 2, 3)
        oc = _attn_c_call(qc, ktc, vc, ccol, crow, tq=tq)

        wo = w_out[i].astype(BF16)
        wr = jnp.concatenate([w_group[i], jnp.moveaxis(w_erouter[i], 0, 1).reshape(d, N_EXPERTS)],
                             axis=1)
        wr = jnp.pad(wr, ((0, 0), (0, ROUTER_COLS - wr.shape[1]))).astype(F32)
        wr1 = wr.astype(BF16)
        wr2 = (wr - wr1.astype(F32)).astype(BF16)
        rbias = jnp.pad(jnp.concatenate([b_group[i], b_erouter[i].reshape(-1)]),
                        (0, ROUTER_COLS - N_GROUPS - N_EXPERTS)).reshape(1, ROUTER_COLS).astype(F32)
        h, base, ri, rw, counts = _post_call(
            x.reshape(n, d), oa.reshape(n, MIX_A), ob.reshape(n, MIX_B), oc.reshape(n, MIX_C),
            p[i].reshape(n, PLE_DIM), wo[:MIX_A], wo[MIX_A:MIX_A + MIX_B], wo[MIX_A + MIX_B:],
            row(ln1_g[i]), row(ln1_b[i]), wr1, wr2, rbias, w_ple_gate[i].astype(BF16),
            w_ple_proj[i].astype(BF16), tm=tm_moe)
        seg, pc, tile_e, nvalid = _expert_segments(counts, tg, rows // tg)
        seg_row = jnp.pad(seg.astype(F32), (0, ROUTER_COLS - N_EXPERTS)).reshape(1, ROUTER_COLS)
        slot_flat = _slot_call(ri, seg_row, tm=tm_moe)[:, 0:2].reshape(-1)
        xs = _dispatch_call(seg, pc, nvalid, slot_flat, h, rows=rows, tb=tb, tg=tg)
        ys = _experts_call(tile_e, nvalid, xs, w_gate_e[i], w_up_e[i], w_down_e[i], tg=tg)
        y = _combine_call(slot_flat, base, rw, row(ln2_g[i]), row(ln2_b[i]), ys, tc=tc)
        x = y.reshape(b, s, d)
    return x
```

```python
import functools
import math

import jax
import jax.numpy as jnp
from jax import lax
from jax.experimental import pallas as pl
from jax.experimental.pallas import tpu as pltpu

F32 = jnp.float32
BF16 = jnp.bfloat16

D_MODEL = 1024
DEPTH = 2
CHUNK = 64
PLE_DIM = 256

A_HEADS = 6
A_DIM = 32
A_VDIM = 64
B_HEADS = 6
B_Q_RANK = 256
B_KV_RANK = 128
B_NOPE = 64
B_ROPE = 32
B_VDIM = 64
ROPE_THETA = 10000.0
C_HEADS = 4
C_DIM = 64

MIX_A = A_HEADS * A_VDIM
MIX_B = B_HEADS * B_VDIM
MIX_C = C_HEADS * C_DIM
B_PAD = 128

N_GROUPS = 4
EXPERTS_PER_GROUP = 8
N_EXPERTS = N_GROUPS * EXPERTS_PER_GROUP
D_EXPERT = 256
ROUTER_COLS = 128

DEEPNORM_ALPHA = (2 * DEPTH) ** 0.25
LN_EPS = 1e-5
RMS_EPS = 1e-6
NEG_BIG = -1e30
LOG2E = math.log2(math.e)

_OFF = {}
_o = 0
for _name, _w in (("qa", 384), ("ka", 384), ("va", 384), ("cq", 256), ("ckv", 128), ("kr", 32),
                  ("qc", 256), ("kc", 256), ("vc", 256), ("fc", 4)):
    _OFF[_name] = (_o, _o + _w)
    _o += _w

T_QA, T_VA, T_QC, T_VC, T_KR, T_KRS, T_FC, T_ROWS = 0, 384, 768, 1024, 1280, 1312, 1344, 1360
FC_ROWS = T_ROWS - T_FC

VMEM_LIMIT = 48 * 1024 * 1024
DMA_BATCH = 4

NT_DIMS = (((1,), (1,)), ((), ()))


def _dot(a, b):
    return jnp.dot(a, b, preferred_element_type=F32)


def _dot_nt(a, b):
    return lax.dot_general(a, b, NT_DIMS, preferred_element_type=F32)


ROW_TILE = D_MODEL // 128


def _store_row_tiles(ref, v):
    rows = v.shape[0]
    for c in range(ROW_TILE):
        ref[pl.ds(c, rows, stride=ROW_TILE), :] = v[:, c * 128:(c + 1) * 128]


def _load_row_tiles(ref, start, rows):
    return jnp.concatenate(
        [ref[pl.ds(start * ROW_TILE + c, rows, stride=ROW_TILE), :] for c in range(ROW_TILE)],
        axis=1)


def _split3(a):
    a1 = a.astype(BF16)
    r1 = a - a1.astype(F32)
    a2 = r1.astype(BF16)
    a3 = (r1 - a2.astype(F32)).astype(BF16)
    return a1, a2, a3


def _prep_kernel(x_ref, wmain_ref, wt_ref, wuq_ref, wukvk_ref, wukvv_ref, gcq_ref, gckv_ref,
                 bf_ref, cos_ref, sin_ref, cost_ref, sint_ref,
                 ka_ref, kb_ref, kc_ref, qta_ref, qtb_ref, qtc_ref, vta_ref, vtb_ref, vtc_ref,
                 cum_ref, carry_ref, *, tm):
    j = pl.program_id(1)

    @pl.when(j == 0)
    def _():
        carry_ref[...] = jnp.zeros_like(carry_ref)

    xb = x_ref[0].astype(BF16)
    z = _dot(xb, wmain_ref[...])
    zt = _dot_nt(wt_ref[...], xb)

    ka_ref[0] = z[:, 0:384].astype(BF16)
    qta_ref[0] = (zt[T_QA:T_QA + 384] * (LOG2E / math.sqrt(A_DIM))).astype(BF16)
    vta_ref[0] = zt[T_VA:T_VA + 384].astype(BF16)
    kc_ref[0] = z[:, 768:1024].astype(BF16)
    qtc_ref[0] = (zt[T_QC:T_QC + 256] * (LOG2E / math.sqrt(C_DIM))).astype(BF16)
    vtc_ref[0] = zt[T_VC:T_VC + 256].astype(BF16)

    fct = zt[T_FC:T_ROWS] + bf_ref[...]
    logf = jnp.minimum(fct, 0.0) - jnp.log(1.0 + jnp.exp(-jnp.abs(fct)))
    r_i = lax.broadcasted_iota(jnp.int32, (tm, tm), 0)
    c_i = lax.broadcasted_iota(jnp.int32, (tm, tm), 1)
    tri = jnp.where(r_i <= c_i, 1.0, 0.0).astype(BF16)
    l1, l2, l3 = _split3(logf)
    cum = _dot(l1, tri) + _dot(l2, tri) + _dot(l3, tri) + carry_ref[:, 0:1]
    cum_ref[0] = cum * LOG2E
    carry_ref[...] = jnp.broadcast_to(cum[:, tm - 1:tm], carry_ref.shape)

    cq = z[:, 384:640]
    cq_n = cq * lax.rsqrt(jnp.mean(cq * cq, axis=-1, keepdims=True) + RMS_EPS) * gcq_ref[...]
    zqt = _dot_nt(wuq_ref[...], cq_n.astype(BF16))
    cosb = cos_ref[0]
    sinb = sin_ref[0]
    scale_b = LOG2E / math.sqrt(B_NOPE + B_ROPE)
    for h in range(B_HEADS):
        lo = h * B_PAD
        qh = zqt[lo:lo + B_PAD] * cosb + zqt[768 + lo:768 + lo + B_PAD] * sinb
        qtb_ref[0, lo:lo + B_PAD, :] = (qh * scale_b).astype(BF16)

    ckv = z[:, 640:768]
    ckv_n = ckv * lax.rsqrt(jnp.mean(ckv * ckv, axis=-1, keepdims=True) + RMS_EPS) * gckv_ref[...]
    ckv_b = ckv_n.astype(BF16)
    vtb_ref[0] = _dot_nt(wukvv_ref[...], ckv_b).astype(BF16)
    knt = _dot_nt(wukvk_ref[...], ckv_b)
    krt = zt[T_KR:T_KR + 32] * cost_ref[0] + zt[T_KRS:T_KRS + 32] * sint_ref[0]
    zpad = jnp.zeros((B_PAD - B_NOPE - B_ROPE, tm), F32)
    pieces = []
    for h in range(B_HEADS):
        pieces += [knt[h * B_NOPE:(h + 1) * B_NOPE], krt, zpad]
    kb_ref[0] = jnp.concatenate(pieces, axis=0).T.astype(BF16)


def _prep_call(x, wmain, wt, wuq, wukvk, wukvv, gcq, gckv, bfc, cos128, sin128, cost, sint, *, tm):
    b, s, d = x.shape
    grid = (b, s // tm)
    full = lambda shape: pl.BlockSpec(shape, lambda bi, j: (0,) * len(shape))
    tok = lambda w: pl.BlockSpec((1, tm, w), lambda bi, j: (bi, j, 0))
    tr = lambda r: pl.BlockSpec((1, r, tm), lambda bi, j: (bi, 0, j))
    out_shape = (
        jax.ShapeDtypeStruct((b, s, 384), BF16),
        jax.ShapeDtypeStruct((b, s, 768), BF16),
        jax.ShapeDtypeStruct((b, s, 256), BF16),
        jax.ShapeDtypeStruct((b, 384, s), BF16),
        jax.ShapeDtypeStruct((b, 768, s), BF16),
        jax.ShapeDtypeStruct((b, 256, s), BF16),
        jax.ShapeDtypeStruct((b, 384, s), BF16),
        jax.ShapeDtypeStruct((b, 384, s), BF16),
        jax.ShapeDtypeStruct((b, 256, s), BF16),
        jax.ShapeDtypeStruct((b, FC_ROWS, s), F32),
    )
    out_specs = (tok(384), tok(768), tok(256), tr(384), tr(768), tr(256), tr(384), tr(384),
                 tr(256), tr(FC_ROWS))
    in_specs = [tok(d), full(wmain.shape), full(wt.shape), full(wuq.shape), full(wukvk.shape),
                full(wukvv.shape), full(gcq.shape), full(gckv.shape), full(bfc.shape),
                tr(128), tr(128), tr(32), tr(32)]
    return pl.pallas_call(
        functools.partial(_prep_kernel, tm=tm),
        grid=grid, in_specs=in_specs, out_specs=out_specs, out_shape=out_shape,
        scratch_shapes=[pltpu.VMEM((FC_ROWS, 128), F32)],
        compiler_params=pltpu.CompilerParams(
            dimension_semantics=("arbitrary", "arbitrary"), vmem_limit_bytes=VMEM_LIMIT),
        name="prep",
    )(x, wmain, wt, wuq, wukvk, wukvv, gcq, gckv, bfc, cos128, sin128, cost, sint)


def _flash(i, tq, n_maps, scores_fn, mask):
    def update(state, j, diag):
        new = []
        for (m, l, acc), (s, v) in zip(state, scores_fn(j)):
            if diag:
                s = jnp.where(mask, s, NEG_BIG)
            m_new = jnp.maximum(m, jnp.max(s, axis=1, keepdims=True))
            alpha = jnp.exp2(m - m_new)
            p = jnp.exp2(s - m_new)
            l_new = alpha * l + jnp.sum(p, axis=1, keepdims=True)
            acc_new = alpha * acc + _dot(p.astype(BF16), v)
            new.append((m_new, l_new, acc_new))
        return tuple(new)

    init = tuple((jnp.full((tq, 1), NEG_BIG, F32), jnp.zeros((tq, 1), F32),
                  jnp.zeros((tq, 128), F32)) for _ in range(n_maps))
    state = update(init, i, True)
    state = lax.fori_loop(0, i, lambda j, st: update(st, j, False), state)
    return [acc * (1.0 / l) for (_, l, acc) in state]


def _chunk_mask(tq):
    r = lax.broadcasted_iota(jnp.int32, (tq, tq), 0)
    c = lax.broadcasted_iota(jnp.int32, (tq, tq), 1)
    return (c // CHUNK) <= (r // CHUNK)


def _causal_mask(tq):
    r = lax.broadcasted_iota(jnp.int32, (tq, tq), 0)
    c = lax.broadcasted_iota(jnp.int32, (tq, tq), 1)
    return c <= r


ONES_ROWS = 16


def _flash_t(i, tq, n_maps, scores_fn, mask):
    ones = jnp.ones((ONES_ROWS, tq), BF16)

    def update(state, j, diag):
        new = []
        for (m, acc), (s, vt) in zip(state, scores_fn(j)):
            if diag:
                s = jnp.where(mask, s, NEG_BIG)
            m_new = jnp.maximum(m, jnp.max(s, axis=0, keepdims=True))
            alpha = jnp.exp2(m - m_new)
            p = jnp.exp2(s - m_new).astype(BF16)
            vext = jnp.concatenate([vt, ones], axis=0)
            new.append((m_new, alpha * acc + _dot(vext, p)))
        return tuple(new)

    init = tuple((jnp.full((1, tq), NEG_BIG, F32), jnp.zeros((128 + ONES_ROWS, tq), F32))
                 for _ in range(n_maps))
    state = update(init, i, True)
    state = lax.fori_loop(0, i, lambda j, st: update(st, j, False), state)
    return [acc[0:128] * (1.0 / acc[128:129]) for (_, acc) in state]


def _chunk_mask_t(tq):
    r = lax.broadcasted_iota(jnp.int32, (tq, tq), 0)
    c = lax.broadcasted_iota(jnp.int32, (tq, tq), 1)
    return (r // CHUNK) <= (c // CHUNK)


def _causal_mask_t(tq):
    r = lax.broadcasted_iota(jnp.int32, (tq, tq), 0)
    c = lax.broadcasted_iota(jnp.int32, (tq, tq), 1)
    return r <= c


def _attn_a_kernel(qt_ref, k_ref, vt_ref, pcol_ref, prow_ref, slope_ref, lam_ref, g_ref, o_ref,
                   *, tq, lam_init):
    i = pl.program_id(2)
    qt = qt_ref[0]
    row = lax.broadcasted_iota(jnp.int32, (128, tq), 0)
    pt = prow_ref[0]
    mask = _chunk_mask_t(tq)
    lam_v = lam_ref[...]
    lam = (jnp.exp(jnp.sum(lam_v[0:1] * lam_v[1:2], axis=1, keepdims=True))
           - jnp.exp(jnp.sum(lam_v[2:3] * lam_v[3:4], axis=1, keepdims=True)) + lam_init)

    def koff(j):
        return pl.multiple_of(j * tq, tq)

    qms = [jnp.where((row >= 32 * mi) & (row < 32 * mi + 32), qt, jnp.zeros_like(qt))
           for mi in range(4)]
    nslopes = [-LOG2E * slope_ref[0, hh:hh + 1, 0:1] for hh in range(2)]

    def scores_fn(j):
        k = k_ref[0, pl.ds(koff(j), tq), :]
        ps = pcol_ref[0, pl.ds(koff(j), tq), :]
        vt = vt_ref[0, :, pl.ds(koff(j), tq)]
        dist = jnp.abs(ps - pt)
        out = []
        for hh in range(2):
            bias = nslopes[hh] * dist
            out += [(_dot(k, qms[2 * hh + mm]) + bias, vt) for mm in range(2)]
        return out

    o = _flash_t(i, tq, 4, scores_fn, mask)
    ot = jnp.where(row < 64, o[0] - lam * o[1], o[2] - lam * o[3])
    sq = ot * ot
    ms0 = jnp.sum(sq[0:64], axis=0, keepdims=True) * (1.0 / A_VDIM)
    ms1 = jnp.sum(sq[64:128], axis=0, keepdims=True) * (1.0 / A_VDIM)
    ms = jnp.where(row < 64, ms0, ms1)
    yt = ot * lax.rsqrt(ms + RMS_EPS) * g_ref[...] * (1.0 - lam_init)
    o_ref[0] = yt.T.astype(BF16)


def _attn_b_kernel(qt_ref, k_ref, vt_ref, o_ref, *, tq):
    i = pl.program_id(2)
    row = lax.broadcasted_iota(jnp.int32, (128, tq), 0)
    mask = _chunk_mask_t(tq)

    def koff(j):
        return pl.multiple_of(j * tq, tq)

    qts = [qt_ref[0, hh * B_PAD:(hh + 1) * B_PAD, :] for hh in range(2)]

    def scores_fn(j):
        vt = vt_ref[0, :, pl.ds(koff(j), tq)]
        return [(_dot(k_ref[0, pl.ds(koff(j), tq), hh * B_PAD:(hh + 1) * B_PAD], qts[hh]), vt)
                for hh in range(2)]

    outs = _flash_t(i, tq, 2, scores_fn, mask)
    o_ref[0] = jnp.where(row < 64, outs[0], outs[1]).T.astype(BF16)


def _attn_c_kernel(qt_ref, k_ref, vt_ref, ccol_ref, crow_ref, o_ref, *, tq):
    i = pl.program_id(2)
    qt = qt_ref[0]
    row = lax.broadcasted_iota(jnp.int32, (128, tq), 0)
    mask = _causal_mask_t(tq)

    def koff(j):
        return pl.multiple_of(j * tq, tq)

    qms = [jnp.where((row >= 64 * hh) & (row < 64 * hh + 64), qt, jnp.zeros_like(qt))
           for hh in range(2)]
    cts = [crow_ref[0, 0, hh:hh + 1, :] for hh in range(2)]

    def scores_fn(j):
        k = k_ref[0, pl.ds(koff(j), tq), :]
        vt = vt_ref[0, :, pl.ds(koff(j), tq)]
        out = []
        for hh in range(2):
            cs = ccol_ref[0, 0, pl.ds(koff(j), tq), hh:hh + 1]
            out.append((_dot(k, qms[hh]) + (cts[hh] - cs), vt))
        return out

    outs = _flash_t(i, tq, 2, scores_fn, mask)
    o_ref[0] = jnp.where(row < 64, outs[0], outs[1]).T.astype(BF16)


def _attn_params():
    return pltpu.CompilerParams(
        dimension_semantics=("arbitrary", "arbitrary", "arbitrary"), vmem_limit_bytes=VMEM_LIMIT)


def _attn_a_call(qta, ka, vta, pcol, prow, slopes, lamv, gcol, *, tq, lam_init):
    b, s, _ = ka.shape
    grid = (b, 3, s // tq)
    return pl.pallas_call(
        functools.partial(_attn_a_kernel, tq=tq, lam_init=lam_init),
        grid=grid,
        in_specs=[
            pl.BlockSpec((1, 128, tq), lambda bi, c, i: (bi, c, i)),
            pl.BlockSpec((1, s, 128), lambda bi, c, i: (bi, 0, c)),
            pl.BlockSpec((1, 128, s), lambda bi, c, i: (bi, c, 0)),
            pl.BlockSpec((1, s, 1), lambda bi, c, i: (bi, 0, 0)),
            pl.BlockSpec((1, 1, tq), lambda bi, c, i: (bi, 0, i)),
            pl.BlockSpec((1, 8, 128), lambda bi, c, i: (c, 0, 0)),
            pl.BlockSpec((8, 128), lambda bi, c, i: (0, 0)),
            pl.BlockSpec((128, 1), lambda bi, c, i: (0, 0)),
        ],
        out_specs=pl.BlockSpec((1, tq, 128), lambda bi, c, i: (bi, i, c)),
        out_shape=jax.ShapeDtypeStruct((b, s, MIX_A), BF16),
        compiler_params=_attn_params(), name="attn_a",
    )(qta, ka, vta, pcol, prow, slopes, lamv, gcol)


def _attn_b_call(qtb, kb, vtb, *, tq):
    b, s, _ = kb.shape
    grid = (b, 3, s // tq)
    return pl.pallas_call(
        functools.partial(_attn_b_kernel, tq=tq),
        grid=grid,
        in_specs=[
            pl.BlockSpec((1, 2 * B_PAD, tq), lambda bi, c, i: (bi, c, i)),
            pl.BlockSpec((1, s, 2 * B_PAD), lambda bi, c, i: (bi, 0, c)),
            pl.BlockSpec((1, 128, s), lambda bi, c, i: (bi, c, 0)),
        ],
        out_specs=pl.BlockSpec((1, tq, 128), lambda bi, c, i: (bi, i, c)),
        out_shape=jax.ShapeDtypeStruct((b, s, MIX_B), BF16),
        compiler_params=_attn_params(), name="attn_b",
    )(qtb, kb, vtb)


def _attn_c_call(qtc, kc, vtc, ccol, crow, *, tq):
    b, s, _ = kc.shape
    grid = (b, 2, s // tq)
    return pl.pallas_call(
        functools.partial(_attn_c_kernel, tq=tq),
        grid=grid,
        in_specs=[
            pl.BlockSpec((1, 128, tq), lambda bi, c, i: (bi, c, i)),
            pl.BlockSpec((1, s, 128), lambda bi, c, i: (bi, 0, c)),
            pl.BlockSpec((1, 128, s), lambda bi, c, i: (bi, c, 0)),
            pl.BlockSpec((1, 1, s, 2), lambda bi, c, i: (bi, c, 0, 0)),
            pl.BlockSpec((1, 1, 2, tq), lambda bi, c, i: (bi, c, 0, i)),
        ],
        out_specs=pl.BlockSpec((1, tq, 128), lambda bi, c, i: (bi, i, c)),
        out_shape=jax.ShapeDtypeStruct((b, s, MIX_C), BF16),
        compiler_params=_attn_params(), name="attn_c",
    )(qtc, kc, vtc, ccol, crow)


def _layer_norm(v, g, b):
    mu = jnp.mean(v, axis=-1, keepdims=True)
    c = v - mu
    var = jnp.mean(c * c, axis=-1, keepdims=True)
    return c * lax.rsqrt(var + LN_EPS) * g + b


def _route(h, wr1_ref, wr2_ref, rbias_ref):
    tm = h.shape[0]
    h1 = h.astype(BF16)
    h2 = (h - h1.astype(F32)).astype(BF16)
    w1 = wr1_ref[...]
    logits = _dot(h1, w1) + _dot(h2, w1) + _dot(h1, wr2_ref[...]) + rbias_ref[...]
    col = lax.broadcasted_iota(jnp.int32, (tm, ROUTER_COLS), 1)
    colf = col.astype(F32)
    big = float(ROUTER_COLS)
    gmask = col < N_GROUPS
    gmax = jnp.max(jnp.where(gmask, logits, NEG_BIG), axis=1, keepdims=True)
    g_idx = jnp.min(jnp.where(gmask & (logits == gmax), colf, big), axis=1, keepdims=True)
    g_w = 1.0 / jnp.sum(jnp.where(gmask, jnp.exp(logits - gmax), 0.0), axis=1, keepdims=True)
    lo = N_GROUPS + g_idx * EXPERTS_PER_GROUP
    emask = (colf >= lo) & (colf < lo + EXPERTS_PER_GROUP)
    top1 = jnp.max(jnp.where(emask, logits, NEG_BIG), axis=1, keepdims=True)
    idx1 = jnp.min(jnp.where(emask & (logits == top1), colf, big), axis=1, keepdims=True)
    emask2 = emask & (colf != idx1)
    top2 = jnp.max(jnp.where(emask2, logits, NEG_BIG), axis=1, keepdims=True)
    idx2 = jnp.min(jnp.where(emask2 & (logits == top2), colf, big), axis=1, keepdims=True)
    e2 = jnp.exp(top2 - top1)
    w_1 = g_w / (1.0 + e2)
    w_2 = g_w * e2 / (1.0 + e2)
    return colf, idx1, idx2, w_1, w_2


def _post_kernel(x_ref, oa_ref, ob_ref, oc_ref, p_ref, woa_ref, wob_ref, woc_ref, g1_ref, b1_ref,
                 wr1_ref, wr2_ref, rbias_ref, wpg_ref, wpp_ref,
                 h_ref, base_ref, ri_ref, rw_ref, cnt_ref, carry_ref):
    i = pl.program_id(0)
    tm = x_ref.shape[0]

    @pl.when(i == 0)
    def _():
        carry_ref[...] = jnp.zeros_like(carry_ref)

    mix = (_dot(oa_ref[...], woa_ref[...]) + _dot(ob_ref[...], wob_ref[...])
           + _dot(oc_ref[...], woc_ref[...]))
    h = _layer_norm(DEEPNORM_ALPHA * x_ref[...] + mix, g1_ref[...], b1_ref[...])
    _store_row_tiles(h_ref, h)
    gl = _dot(h.astype(BF16), wpg_ref[...])
    ple = (1.0 / (1.0 + jnp.exp(-gl))) * _dot(p_ref[...].astype(BF16), wpp_ref[...])
    base_ref[...] = DEEPNORM_ALPHA * h + ple

    colf, idx1, idx2, w_1, w_2 = _route(h, wr1_ref, wr2_ref, rbias_ref)
    hit1 = colf == idx1
    hit2 = colf == idx2
    onehot = jnp.where(hit1 | hit2, 1.0, 0.0)
    r_i = lax.broadcasted_iota(jnp.int32, (tm, tm), 0)
    c_i = lax.broadcasted_iota(jnp.int32, (tm, tm), 1)
    before = jnp.where(c_i < r_i, 1.0, 0.0).astype(BF16)
    seen = _dot(before, onehot.astype(BF16)) + carry_ref[0:1, :]
    rank1 = jnp.sum(jnp.where(hit1, seen, 0.0), axis=1, keepdims=True)
    rank2 = jnp.sum(jnp.where(hit2, seen, 0.0), axis=1, keepdims=True)
    total = carry_ref[0:1, :] + jnp.sum(onehot, axis=0, keepdims=True)
    carry_ref[...] = jnp.broadcast_to(total, carry_ref.shape)
    cnt_ref[...] = jnp.broadcast_to(total, cnt_ref.shape)

    lane8 = lax.broadcasted_iota(jnp.int32, (tm, 8), 1)
    ri = jnp.where(lane8 == 0, idx1 - N_GROUPS,
                   jnp.where(lane8 == 1, idx2 - N_GROUPS,
                             jnp.where(lane8 == 2, rank1, jnp.where(lane8 == 3, rank2, 0.0))))
    ri_ref[...] = ri.astype(jnp.int32)
    rw_ref[...] = jnp.where(lane8 == 0, w_1, jnp.where(lane8 == 1, w_2, 0.0))


def _post_call(x2, oa, ob, oc, p2, woa, wob, woc, g1, b1, wr1, wr2, rbias, wpg, wpp, *, tm):
    n, d = x2.shape
    tok = lambda w: pl.BlockSpec((tm, w), lambda t: (t, 0))
    full = lambda a: pl.BlockSpec(a.shape, lambda t: (0,) * a.ndim)
    return pl.pallas_call(
        _post_kernel,
        grid=(n // tm,),
        in_specs=[tok(d), tok(MIX_A), tok(MIX_B), tok(MIX_C), tok(PLE_DIM), full(woa), full(wob),
                  full(woc), full(g1), full(b1), full(wr1), full(wr2), full(rbias), full(wpg),
                  full(wpp)],
        out_specs=(pl.BlockSpec((tm * ROW_TILE, 128), lambda t: (t, 0)), tok(d), tok(8), tok(8),
                   pl.BlockSpec((8, ROUTER_COLS), lambda t: (0, 0))),
        out_shape=(jax.ShapeDtypeStruct((n * ROW_TILE, 128), F32), jax.ShapeDtypeStruct((n, d), F32),
                   jax.ShapeDtypeStruct((n, 8), jnp.int32), jax.ShapeDtypeStruct((n, 8), F32),
                   jax.ShapeDtypeStruct((8, ROUTER_COLS), F32)),
        scratch_shapes=[pltpu.VMEM((8, ROUTER_COLS), F32)],
        compiler_params=pltpu.CompilerParams(
            dimension_semantics=("arbitrary",), vmem_limit_bytes=VMEM_LIMIT),
        name="post",
    )(x2, oa, ob, oc, p2, woa, wob, woc, g1, b1, wr1, wr2, rbias, wpg, wpp)


def _row_copy(src_ref, src_row, dst_ref, dst_row, sem):
    src = pl.ds(pl.multiple_of(src_row * ROW_TILE, ROW_TILE), ROW_TILE)
    dst = pl.ds(pl.multiple_of(dst_row * ROW_TILE, ROW_TILE), ROW_TILE)
    return pltpu.make_async_copy(src_ref.at[src], dst_ref.at[dst], sem)


def _slot_kernel(ri_ref, seg_ref, slot_ref):
    tm = ri_ref.shape[0]
    ri = ri_ref[...]
    colf = lax.broadcasted_iota(jnp.int32, (tm, ROUTER_COLS), 1).astype(F32)
    seg = seg_ref[...]
    lane8 = lax.broadcasted_iota(jnp.int32, (tm, 8), 1)
    out = jnp.zeros((tm, 8), F32)
    for k in range(2):
        e = ri[:, k:k + 1].astype(F32)
        start = jnp.sum(jnp.where(colf == e, seg, 0.0), axis=1, keepdims=True)
        out = jnp.where(lane8 == k, start + ri[:, 2 + k:3 + k].astype(F32), out)
    slot_ref[...] = out.astype(jnp.int32)


def _slot_call(ri, seg_row, *, tm):
    n = ri.shape[0]
    return pl.pallas_call(
        _slot_kernel,
        grid=(n // tm,),
        in_specs=[pl.BlockSpec((tm, 8), lambda t: (t, 0)),
                  pl.BlockSpec((1, ROUTER_COLS), lambda t: (0, 0))],
        out_specs=pl.BlockSpec((tm, 8), lambda t: (t, 0)),
        out_shape=jax.ShapeDtypeStruct((n, 8), jnp.int32),
        compiler_params=pltpu.CompilerParams(dimension_semantics=("arbitrary",)),
        name="slots",
    )(ri, seg_row)


def _dispatch_kernel(seg_ref, pc_ref, nv_ref, slot_ref, h_ref, xs_hbm, zbuf, sem, zsem, *, tb, tg):
    i = pl.program_id(0)
    n_tiles = xs_hbm.shape[0] // (tg * ROW_TILE)

    def unused_tile_zero_copy(jt):
        start = pl.multiple_of(jt * (tg * ROW_TILE), tg * ROW_TILE)
        return pltpu.make_async_copy(zbuf, xs_hbm.at[pl.ds(start, tg * ROW_TILE)], zsem)

    def for_unused_tiles(fn):
        def body(jt, c):
            fn(unused_tile_zero_copy(jt))
            return c
        lax.fori_loop(nv_ref[0], n_tiles, body, 0)

    def tail_zero_copy(e):
        start = pl.multiple_of((seg_ref[e] + pc_ref[e] - tg) * ROW_TILE, tg * ROW_TILE)
        return pltpu.make_async_copy(zbuf, xs_hbm.at[pl.ds(start, tg * ROW_TILE)], zsem)

    @pl.when(i == 0)
    def _():
        zbuf[...] = jnp.zeros_like(zbuf)
        for e in range(N_EXPERTS):
            @pl.when(pc_ref[e] > 0)
            def _():
                tail_zero_copy(e).start()
        for e in range(N_EXPERTS):
            @pl.when(pc_ref[e] > 0)
            def _():
                tail_zero_copy(e).wait()
        for_unused_tiles(lambda cp: cp.start())
        for_unused_tiles(lambda cp: cp.wait())

    def body(g, c):
        t0 = g * DMA_BATCH
        dst = [slot_ref[2 * t0 + u] for u in range(2 * DMA_BATCH)]
        for u in range(2 * DMA_BATCH):
            _row_copy(h_ref, t0 + u // 2, xs_hbm, dst[u], sem).start()
        return c

    lax.fori_loop(0, tb // DMA_BATCH, body, 0)
    for _ in range(2):
        pltpu.make_async_copy(h_ref, xs_hbm.at[pl.ds(0, tb * ROW_TILE)], sem).wait()


def _dispatch_call(seg, pc, nvalid, slot_flat, h, *, rows, tb, tg):
    n = h.shape[0] // ROW_TILE
    smem_blk = pl.BlockSpec((2 * tb,), lambda i, seg, pc, nv: (i,), memory_space=pltpu.SMEM)
    grid_spec = pltpu.PrefetchScalarGridSpec(
        num_scalar_prefetch=3, grid=(n // tb,),
        in_specs=[smem_blk,
                  pl.BlockSpec((tb * ROW_TILE, 128), lambda i, seg, pc, nv: (i, 0))],
        out_specs=pl.BlockSpec(memory_space=pl.ANY),
        scratch_shapes=[pltpu.VMEM((tg * ROW_TILE, 128), F32), pltpu.SemaphoreType.DMA(()),
                        pltpu.SemaphoreType.DMA(())])
    return pl.pallas_call(
        functools.partial(_dispatch_kernel, tb=tb, tg=tg),
        grid_spec=grid_spec,
        out_shape=jax.ShapeDtypeStruct((rows * ROW_TILE, 128), F32),
        compiler_params=pltpu.CompilerParams(
            dimension_semantics=("arbitrary",), vmem_limit_bytes=VMEM_LIMIT),
        name="dispatch",
    )(seg, pc, nvalid, slot_flat, h)


def _experts_kernel(te_ref, nv_ref, xs_ref, wg_ref, wu_ref, wd_ref, ys_ref, wgb_ref, wub_ref,
                    wdb_ref):
    j = pl.program_id(0)

    @pl.when(j < nv_ref[0])
    def _():
        prev = te_ref[jnp.maximum(j - 1, 0)]

        @pl.when((j == 0) | (te_ref[j] != prev))
        def _():
            wgb_ref[...] = wg_ref[0].astype(BF16)
            wub_ref[...] = wu_ref[0].astype(BF16)
            wdb_ref[...] = wd_ref[0].astype(BF16)

        tg = xs_ref.shape[0] // ROW_TILE
        xb = _load_row_tiles(xs_ref, 0, tg).astype(BF16)
        gate = _dot(xb, wgb_ref[...])
        up = _dot(xb, wub_ref[...])
        act = gate * (1.0 / (1.0 + jnp.exp(-gate))) * up
        _store_row_tiles(ys_ref, _dot(act.astype(BF16), wdb_ref[...]))

    @pl.when(j >= nv_ref[0])
    def _():
        ys_ref[...] = jnp.zeros_like(ys_ref)


def _experts_call(tile_e, nvalid, xs, wg, wu, wd, *, tg):
    rows = xs.shape[0] // ROW_TILE
    last = lambda j, te, nv: jnp.minimum(j, nv[0] - 1)
    row_blk = pl.BlockSpec((tg * ROW_TILE, 128), lambda j, te, nv: (last(j, te, nv), 0))
    out_blk = pl.BlockSpec((tg * ROW_TILE, 128), lambda j, te, nv: (j, 0))
    w_blk = lambda a: pl.BlockSpec((1,) + a.shape[1:], lambda j, te, nv: (te[last(j, te, nv)], 0, 0))
    grid_spec = pltpu.PrefetchScalarGridSpec(
        num_scalar_prefetch=2, grid=(rows // tg,),
        in_specs=[row_blk, w_blk(wg), w_blk(wu), w_blk(wd)],
        out_specs=out_blk,
        scratch_shapes=[pltpu.VMEM(wg.shape[1:], BF16), pltpu.VMEM(wu.shape[1:], BF16),
                        pltpu.VMEM(wd.shape[1:], BF16)])
    return pl.pallas_call(
        _experts_kernel,
        grid_spec=grid_spec,
        out_shape=jax.ShapeDtypeStruct((rows * ROW_TILE, 128), F32),
        compiler_params=pltpu.CompilerParams(
            dimension_semantics=("arbitrary",), vmem_limit_bytes=VMEM_LIMIT),
        name="experts",
    )(tile_e, nvalid, xs, wg, wu, wd)


def _combine_kernel(cur_ref, nxt_ref, base_ref, rw_ref, g2_ref, b2_ref, ys_hbm, y_ref, ybuf, sem,
                    *, tc):
    i = pl.program_id(0)
    n_steps = pl.num_programs(0)
    slot = i % 2

    def gathered(slot_):
        return pltpu.make_async_copy(ys_hbm.at[pl.ds(0, 2 * tc * ROW_TILE)], ybuf.at[slot_],
                                     sem.at[slot_])

    def issue(rows_ref, slot_):
        def body(g, c):
            t0 = g * DMA_BATCH
            src = [rows_ref[2 * t0 + u] for u in range(2 * DMA_BATCH)]
            for u in range(2 * DMA_BATCH):
                _row_copy(ys_hbm, src[u], ybuf.at[slot_], (u % 2) * tc + t0 + u // 2,
                          sem.at[slot_]).start()
            return c
        lax.fori_loop(0, tc // DMA_BATCH, body, 0)

    @pl.when(i == 0)
    def _():
        issue(cur_ref, 0)

    @pl.when(i + 1 < n_steps)
    def _():
        issue(nxt_ref, 1 - slot)

    gathered(slot).wait()
    y1 = _load_row_tiles(ybuf.at[slot], 0, tc)
    y2 = _load_row_tiles(ybuf.at[slot], tc, tc)
    rw = rw_ref[...]
    moe = rw[:, 0:1] * y1 + rw[:, 1:2] * y2
    y_ref[...] = _layer_norm(base_ref[...] + moe, g2_ref[...], b2_ref[...])


def _combine_call(slot_flat, base, rw, g2, b2, ys, *, tc):
    n, d = base.shape
    n_steps = n // tc
    cur = pl.BlockSpec((2 * tc,), lambda i: (i,), memory_space=pltpu.SMEM)
    nxt = pl.BlockSpec((2 * tc,), lambda i: (jnp.minimum(i + 1, n_steps - 1),),
                       memory_space=pltpu.SMEM)
    tok = lambda w: pl.BlockSpec((tc, w), lambda i: (i, 0))
    full = lambda a: pl.BlockSpec(a.shape, lambda i: (0,) * a.ndim)
    return pl.pallas_call(
        functools.partial(_combine_kernel, tc=tc),
        grid=(n_steps,),
        in_specs=[cur, nxt, tok(d), tok(8), full(g2), full(b2), pl.BlockSpec(memory_space=pl.ANY)],
        out_specs=tok(d),
        out_shape=jax.ShapeDtypeStruct((n, d), F32),
        scratch_shapes=[pltpu.VMEM((2, 2 * tc * ROW_TILE, 128), F32),
                        pltpu.SemaphoreType.DMA((2,))],
        compiler_params=pltpu.CompilerParams(
            dimension_semantics=("arbitrary",), vmem_limit_bytes=VMEM_LIMIT),
        name="combine",
    )(slot_flat, slot_flat, base, rw, g2, b2, ys)


def _expert_segments(counts, tg, n_tiles):
    cnt = counts[0, N_GROUPS:N_GROUPS + N_EXPERTS].astype(jnp.int32)
    pc = ((cnt + tg - 1) // tg) * tg
    seg_end = jnp.cumsum(pc)
    seg = seg_end - pc
    nvalid = (seg_end[-1] // tg).reshape(1)
    tile_e = jnp.searchsorted(seg_end, jnp.arange(n_tiles, dtype=jnp.int32) * tg, side="right")
    tile_e = jnp.minimum(tile_e, N_EXPERTS - 1).astype(jnp.int32)
    return seg.astype(jnp.int32), pc.astype(jnp.int32), tile_e, nvalid.astype(jnp.int32)


def _cols(w, name):
    lo, hi = _OFF[name]
    return w[:, lo:hi]


def _rot_pairs(w):
    return jnp.concatenate([-w[..., 16:32], w[..., 0:16]], axis=-1)


def _layer_weights(w_in, w_uq, w_ukv, b_forget):
    wmain = jnp.concatenate([_cols(w_in, n) for n in ("ka", "cq", "ckv", "kc")], axis=1)
    kr = _cols(w_in, "kr")
    fc_pad = jnp.pad(_cols(w_in, "fc"), ((0, 0), (0, FC_ROWS - C_HEADS)))
    wt = jnp.concatenate([_cols(w_in, "qa"), _cols(w_in, "va"), _cols(w_in, "qc"), _cols(w_in, "vc"),
                          kr, _rot_pairs(kr), fc_pad], axis=1).T
    wq = w_uq.reshape(B_Q_RANK, B_HEADS, B_NOPE + B_ROPE)
    zpad = jnp.zeros((B_Q_RANK, B_HEADS, B_PAD - B_NOPE - B_ROPE), w_uq.dtype)
    plain = jnp.concatenate([wq, zpad], axis=-1).reshape(B_Q_RANK, B_HEADS * B_PAD)
    rot = jnp.concatenate([jnp.zeros_like(wq[..., :B_NOPE]), _rot_pairs(wq[..., B_NOPE:]), zpad],
                          axis=-1).reshape(B_Q_RANK, B_HEADS * B_PAD)
    wuq = jnp.concatenate([plain, rot], axis=1).T
    wkv = w_ukv.reshape(B_KV_RANK, B_HEADS, B_NOPE + B_VDIM)
    wukvk = wkv[..., :B_NOPE].reshape(B_KV_RANK, B_HEADS * B_NOPE).T
    wukvv = wkv[..., B_NOPE:].reshape(B_KV_RANK, B_HEADS * B_VDIM).T
    bfc = jnp.pad(b_forget, (0, FC_ROWS - C_HEADS)).reshape(FC_ROWS, 1)
    return (wmain.astype(BF16), wt.astype(BF16), wuq.astype(BF16), wukvk.astype(BF16),
            wukvv.astype(BF16), bfc.astype(F32))


def _rope_inputs(positions):
    half = B_ROPE // 2
    inv = ROPE_THETA ** (-jnp.arange(half, dtype=F32) / half)
    ang = positions.astype(F32)[..., None] * inv
    cos, sin = jnp.cos(ang), jnp.sin(ang)
    b, s, _ = cos.shape
    ones = jnp.ones((b, s, B_NOPE), F32)
    zeros = jnp.zeros((b, s, B_PAD - B_NOPE - B_ROPE), F32)
    cos128 = jnp.concatenate([ones, cos, cos, zeros], axis=-1)
    sin128 = jnp.concatenate([jnp.zeros_like(ones), sin, sin, zeros], axis=-1)
    cost = jnp.swapaxes(jnp.concatenate([cos, cos], axis=-1), 1, 2)
    sint = jnp.swapaxes(jnp.concatenate([sin, sin], axis=-1), 1, 2)
    return jnp.swapaxes(cos128, 1, 2), jnp.swapaxes(sin128, 1, 2), cost, sint


def kernel(x, p, positions, w_in, w_uq, w_ukv, g_cq, g_ckv, lam_q1, lam_k1, lam_q2, lam_k2, g_diff,
           b_forget, w_out, ln1_g, ln1_b, w_group, b_group, w_erouter, b_erouter, w_gate_e, w_up_e,
           w_down_e, w_ple_gate, w_ple_proj, ln2_g, ln2_b):
    b, s, d = x.shape
    n = b * s
    tq = 512
    tm_prep = 512
    tm_moe = 512
    tg = 512
    rows = 2 * n + N_EXPERTS * tg
    tb = 512
    tc = 512
    cos128, sin128, cost, sint = _rope_inputs(positions)
    posf = positions.astype(F32)
    pcol = posf[:, :, None]
    prow = posf[:, None, :]
    slopes = 2.0 ** (-8.0 * jnp.arange(1, A_HEADS + 1, dtype=F32) / A_HEADS)
    slopes = jnp.broadcast_to(
        jnp.pad(slopes.reshape(3, 2), ((0, 0), (0, 6)))[:, :, None], (3, 8, 128))
    row = lambda v: v.reshape(1, -1).astype(F32)

    for i in range(DEPTH):
        wmain, wt, wuq, wukvk, wukvv, bfc = _layer_weights(w_in[i], w_uq[i], w_ukv[i], b_forget[i])
        ka, kb, kc, qta, qtb, qtc, vta, vtb, vtc, cumt = _prep_call(
            x, wmain, wt, wuq, wukvk, wukvv, row(g_cq[i]), row(g_ckv[i]), bfc,
            cos128, sin128, cost, sint, tm=tm_prep)
        lamv = jnp.pad(jnp.stack([lam_q1[i], lam_k1[i], lam_q2[i], lam_k2[i]]).astype(F32),
                       ((0, 4), (0, 128 - A_DIM)))
        gcol = jnp.tile(g_diff[i].astype(F32), 2).reshape(128, 1)
        lam_init = 0.8 - 0.6 * math.exp(-0.3 * i)
        oa = _attn_a_call(qta, ka, vta, pcol, prow, slopes, lamv, gcol, tq=tq, lam_init=lam_init)
        ob = _attn_b_call(qtb, kb, vtb, tq=tq)
        crow = cumt[:, :C_HEADS].reshape(b, 2, 2, s)
        ccol = jnp.swapaxes(crow, 2, 3)
        oc = _attn_c_call(qtc, kc, vtc, ccol, crow, tq=tq)

        wo = w_out[i].astype(BF16)
        wr = jnp.concatenate([w_group[i], jnp.moveaxis(w_erouter[i], 0, 1).reshape(d, N_EXPERTS)],
                             axis=1)
        wr = jnp.pad(wr, ((0, 0), (0, ROUTER_COLS - wr.shape[1]))).astype(F32)
        wr1 = wr.astype(BF16)
        wr2 = (wr - wr1.astype(F32)).astype(BF16)
        rbias = jnp.pad(jnp.concatenate([b_group[i], b_erouter[i].reshape(-1)]),
                        (0, ROUTER_COLS - N_GROUPS - N_EXPERTS)).reshape(1, ROUTER_COLS).astype(F32)
        h, base, ri, rw, counts = _post_call(
            x.reshape(n, d), oa.reshape(n, MIX_A), ob.reshape(n, MIX_B), oc.reshape(n, MIX_C),
            p[i].reshape(n, PLE_DIM), wo[:MIX_A], wo[MIX_A:MIX_A + MIX_B], wo[MIX_A + MIX_B:],
            row(ln1_g[i]), row(ln1_b[i]), wr1, wr2, rbias, w_ple_gate[i].astype(BF16),
            w_ple_proj[i].astype(BF16), tm=tm_moe)
        seg, pc, tile_e, nvalid = _expert_segments(counts, tg, rows // tg)
        seg_row = jnp.pad(seg.astype(F32), (0, ROUTER_COLS - N_EXPERTS)).reshape(1, ROUTER_COLS)
        slot_flat = _slot_call(ri, seg_row, tm=tm_moe)[:, 0:2].reshape(-1)
        xs = _dispatch_call(seg, pc, nvalid, slot_flat, h, rows=rows, tb=tb, tg=tg)
        ys = _experts_call(tile_e, nvalid, xs, w_gate_e[i], w_up_e[i], w_down_e[i], tg=tg)
        y = _combine_call(slot_flat, base, rw, row(ln2_g[i]), row(ln2_b[i]), ys, tc=tc)
        x = y.reshape(b, s, d)
    return x
```

```python
import functools
import math

import jax
import jax.numpy as jnp
from jax import lax
from jax.experimental import pallas as pl
from jax.experimental.pallas import tpu as pltpu

F32 = jnp.float32
BF16 = jnp.bfloat16

D_MODEL = 1024
DEPTH = 2
CHUNK = 64
PLE_DIM = 256

A_HEADS = 6
A_DIM = 32
A_VDIM = 64
B_HEADS = 6
B_Q_RANK = 256
B_KV_RANK = 128
B_NOPE = 64
B_ROPE = 32
B_VDIM = 64
ROPE_THETA = 10000.0
C_HEADS = 4
C_DIM = 64

MIX_A = A_HEADS * A_VDIM
MIX_B = B_HEADS * B_VDIM
MIX_C = C_HEADS * C_DIM
B_PAD = 128

N_GROUPS = 4
EXPERTS_PER_GROUP = 8
N_EXPERTS = N_GROUPS * EXPERTS_PER_GROUP
D_EXPERT = 256
ROUTER_COLS = 128

DEEPNORM_ALPHA = (2 * DEPTH) ** 0.25
LN_EPS = 1e-5
RMS_EPS = 1e-6
NEG_BIG = -1e30
LOG2E = math.log2(math.e)

_OFF = {}
_o = 0
for _name, _w in (("qa", 384), ("ka", 384), ("va", 384), ("cq", 256), ("ckv", 128), ("kr", 32),
                  ("qc", 256), ("kc", 256), ("vc", 256), ("fc", 4)):
    _OFF[_name] = (_o, _o + _w)
    _o += _w

T_QA, T_VA, T_QC, T_VC, T_KR, T_KRS, T_FC, T_ROWS = 0, 384, 768, 1024, 1280, 1312, 1344, 1360
FC_ROWS = T_ROWS - T_FC

VMEM_LIMIT = 48 * 1024 * 1024
DMA_BATCH = 4

NT_DIMS = (((1,), (1,)), ((), ()))


def _dot(a, b):
    return jnp.dot(a, b, preferred_element_type=F32)


def _dot_nt(a, b):
    return lax.dot_general(a, b, NT_DIMS, preferred_element_type=F32)


ROW_TILE = D_MODEL // 128


def _store_row_tiles(ref, v):
    rows = v.shape[0]
    for c in range(ROW_TILE):
        ref[pl.ds(c, rows, stride=ROW_TILE), :] = v[:, c * 128:(c + 1) * 128]


def _load_row_tiles(ref, start, rows):
    return jnp.concatenate(
        [ref[pl.ds(start * ROW_TILE + c, rows, stride=ROW_TILE), :] for c in range(ROW_TILE)],
        axis=1)


def _split3(a):
    a1 = a.astype(BF16)
    r1 = a - a1.astype(F32)
    a2 = r1.astype(BF16)
    a3 = (r1 - a2.astype(F32)).astype(BF16)
    return a1, a2, a3


def _prep_kernel(x_ref, wmain_ref, wt_ref, wuq_ref, wukvk_ref, wukvv_ref, gcq_ref, gckv_ref,
                 bf_ref, cos_ref, sin_ref, cost_ref, sint_ref,
                 ka_ref, kb_ref, kc_ref, qta_ref, qtb_ref, qtc_ref, vta_ref, vtb_ref, vtc_ref,
                 cum_ref, carry_ref, *, tm):
    j = pl.program_id(1)

    @pl.when(j == 0)
    def _():
        carry_ref[...] = jnp.zeros_like(carry_ref)

    xb = x_ref[0].astype(BF16)
    z = _dot(xb, wmain_ref[...])
    zt = _dot_nt(wt_ref[...], xb)

    ka_ref[0] = z[:, 0:384].astype(BF16)
    qta_ref[0] = (zt[T_QA:T_QA + 384] * (LOG2E / math.sqrt(A_DIM))).astype(BF16)
    vta_ref[0] = zt[T_VA:T_VA + 384].astype(BF16)
    kc_ref[0] = z[:, 768:1024].astype(BF16)
    qtc_ref[0] = (zt[T_QC:T_QC + 256] * (LOG2E / math.sqrt(C_DIM))).astype(BF16)
    vtc_ref[0] = zt[T_VC:T_VC + 256].astype(BF16)

    fct = zt[T_FC:T_ROWS] + bf_ref[...]
    logf = jnp.minimum(fct, 0.0) - jnp.log(1.0 + jnp.exp(-jnp.abs(fct)))
    r_i = lax.broadcasted_iota(jnp.int32, (tm, tm), 0)
    c_i = lax.broadcasted_iota(jnp.int32, (tm, tm), 1)
    tri = jnp.where(r_i <= c_i, 1.0, 0.0).astype(BF16)
    l1, l2, l3 = _split3(logf)
    cum = _dot(l1, tri) + _dot(l2, tri) + _dot(l3, tri) + carry_ref[:, 0:1]
    cum_ref[0] = cum * LOG2E
    carry_ref[...] = jnp.broadcast_to(cum[:, tm - 1:tm], carry_ref.shape)

    cq = z[:, 384:640]
    cq_n = cq * lax.rsqrt(jnp.mean(cq * cq, axis=-1, keepdims=True) + RMS_EPS) * gcq_ref[...]
    zqt = _dot_nt(wuq_ref[...], cq_n.astype(BF16))
    cosb = cos_ref[0]
    sinb = sin_ref[0]
    scale_b = LOG2E / math.sqrt(B_NOPE + B_ROPE)
    for h in range(B_HEADS):
        lo = h * B_PAD
        qh = zqt[lo:lo + B_PAD] * cosb + zqt[768 + lo:768 + lo + B_PAD] * sinb
        qtb_ref[0, lo:lo + B_PAD, :] = (qh * scale_b).astype(BF16)

    ckv = z[:, 640:768]
    ckv_n = ckv * lax.rsqrt(jnp.mean(ckv * ckv, axis=-1, keepdims=True) + RMS_EPS) * gckv_ref[...]
    ckv_b = ckv_n.astype(BF16)
    vtb_ref[0] = _dot_nt(wukvv_ref[...], ckv_b).astype(BF16)
    knt = _dot_nt(wukvk_ref[...], ckv_b)
    krt = zt[T_KR:T_KR + 32] * cost_ref[0] + zt[T_KRS:T_KRS + 32] * sint_ref[0]
    zpad = jnp.zeros((B_PAD - B_NOPE - B_ROPE, tm), F32)
    pieces = []
    for h in range(B_HEADS):
        pieces += [knt[h * B_NOPE:(h + 1) * B_NOPE], krt, zpad]
    kb_ref[0] = jnp.concatenate(pieces, axis=0).T.astype(BF16)


def _prep_call(x, wmain, wt, wuq, wukvk, wukvv, gcq, gckv, bfc, cos128, sin128, cost, sint, *, tm):
    b, s, d = x.shape
    grid = (b, s // tm)
    full = lambda shape: pl.BlockSpec(shape, lambda bi, j: (0,) * len(shape))
    tok = lambda w: pl.BlockSpec((1, tm, w), lambda bi, j: (bi, j, 0))
    tr = lambda r: pl.BlockSpec((1, r, tm), lambda bi, j: (bi, 0, j))
    out_shape = (
        jax.ShapeDtypeStruct((b, s, 384), BF16),
        jax.ShapeDtypeStruct((b, s, 768), BF16),
        jax.ShapeDtypeStruct((b, s, 256), BF16),
        jax.ShapeDtypeStruct((b, 384, s), BF16),
        jax.ShapeDtypeStruct((b, 768, s), BF16),
        jax.ShapeDtypeStruct((b, 256, s), BF16),
        jax.ShapeDtypeStruct((b, 384, s), BF16),
        jax.ShapeDtypeStruct((b, 384, s), BF16),
        jax.ShapeDtypeStruct((b, 256, s), BF16),
        jax.ShapeDtypeStruct((b, FC_ROWS, s), F32),
    )
    out_specs = (tok(384), tok(768), tok(256), tr(384), tr(768), tr(256), tr(384), tr(384),
                 tr(256), tr(FC_ROWS))
    in_specs = [tok(d), full(wmain.shape), full(wt.shape), full(wuq.shape), full(wukvk.shape),
                full(wukvv.shape), full(gcq.shape), full(gckv.shape), full(bfc.shape),
                tr(128), tr(128), tr(32), tr(32)]
    return pl.pallas_call(
        functools.partial(_prep_kernel, tm=tm),
        grid=grid, in_specs=in_specs, out_specs=out_specs, out_shape=out_shape,
        scratch_shapes=[pltpu.VMEM((FC_ROWS, 128), F32)],
        compiler_params=pltpu.CompilerParams(
            dimension_semantics=("arbitrary", "arbitrary"), vmem_limit_bytes=VMEM_LIMIT),
        name="prep",
    )(x, wmain, wt, wuq, wukvk, wukvv, gcq, gckv, bfc, cos128, sin128, cost, sint)


def _flash(i, tq, n_maps, scores_fn, mask):
    def update(state, j, diag):
        new = []
        for (m, l, acc), (s, v) in zip(state, scores_fn(j)):
            if diag:
                s = jnp.where(mask, s, NEG_BIG)
            m_new = jnp.maximum(m, jnp.max(s, axis=1, keepdims=True))
            alpha = jnp.exp2(m - m_new)
            p = jnp.exp2(s - m_new)
            l_new = alpha * l + jnp.sum(p, axis=1, keepdims=True)
            acc_new = alpha * acc + _dot(p.astype(BF16), v)
            new.append((m_new, l_new, acc_new))
        return tuple(new)

    init = tuple((jnp.full((tq, 1), NEG_BIG, F32), jnp.zeros((tq, 1), F32),
                  jnp.zeros((tq, 128), F32)) for _ in range(n_maps))
    state = update(init, i, True)
    state = lax.fori_loop(0, i, lambda j, st: update(st, j, False), state)
    return [acc * (1.0 / l) for (_, l, acc) in state]


def _chunk_mask(tq):
    r = lax.broadcasted_iota(jnp.int32, (tq, tq), 0)
    c = lax.broadcasted_iota(jnp.int32, (tq, tq), 1)
    return (c // CHUNK) <= (r // CHUNK)


def _causal_mask(tq):
    r = lax.broadcasted_iota(jnp.int32, (tq, tq), 0)
    c = lax.broadcasted_iota(jnp.int32, (tq, tq), 1)
    return c <= r


ONES_ROWS = 16


def _flash_t(i, tq, n_maps, scores_fn, mask):
    ones = jnp.ones((ONES_ROWS, tq), BF16)

    def update(state, j, diag):
        new = []
        for (m, acc), (s, vt) in zip(state, scores_fn(j)):
            if diag:
                s = jnp.where(mask, s, NEG_BIG)
            m_new = jnp.maximum(m, jnp.max(s, axis=0, keepdims=True))
            alpha = jnp.exp2(m - m_new)
            p = jnp.exp2(s - m_new).astype(BF16)
            vext = jnp.concatenate([vt, ones], axis=0)
            new.append((m_new, alpha * acc + _dot(vext, p)))
        return tuple(new)

    init = tuple((jnp.full((1, tq), NEG_BIG, F32), jnp.zeros((128 + ONES_ROWS, tq), F32))
                 for _ in range(n_maps))
    state = update(init, i, True)
    state = lax.fori_loop(0, i, lambda j, st: update(st, j, False), state)
    return [acc[0:128] * (1.0 / acc[128:129]) for (_, acc) in state]


def _chunk_mask_t(tq):
    r = lax.broadcasted_iota(jnp.int32, (tq, tq), 0)
    c = lax.broadcasted_iota(jnp.int32, (tq, tq), 1)
    return (r // CHUNK) <= (c // CHUNK)


def _causal_mask_t(tq):
    r = lax.broadcasted_iota(jnp.int32, (tq, tq), 0)
    c = lax.broadcasted_iota(jnp.int32, (tq, tq), 1)
    return r <= c


def _attn_a_kernel(qt_ref, k_ref, vt_ref, pcol_ref, prow_ref, slope_ref, lam_ref, g_ref, o_ref,
                   *, tq, lam_init, pairs):
    i = pl.program_id(2)
    row = lax.broadcasted_iota(jnp.int32, (128, tq), 0)
    pt = prow_ref[0]
    mask = _chunk_mask_t(tq)
    lam_v = lam_ref[...]
    lam = (jnp.exp(jnp.sum(lam_v[0:1] * lam_v[1:2], axis=1, keepdims=True))
           - jnp.exp(jnp.sum(lam_v[2:3] * lam_v[3:4], axis=1, keepdims=True)) + lam_init)

    def koff(j):
        return pl.multiple_of(j * tq, tq)

    qms, nslopes = [], []
    for pr in range(pairs):
        qt = qt_ref[0, pr * 128:(pr + 1) * 128, :]
        qms += [jnp.where((row >= 32 * mi) & (row < 32 * mi + 32), qt, jnp.zeros_like(qt))
                for mi in range(4)]
        nslopes += [-LOG2E * slope_ref[pr, hh:hh + 1, 0:1] for hh in range(2)]

    def scores_fn(j):
        ps = pcol_ref[0, pl.ds(koff(j), tq), :]
        dist = jnp.abs(ps - pt)
        out = []
        for pr in range(pairs):
            k = k_ref[0, pl.ds(koff(j), tq), pr * 128:(pr + 1) * 128]
            vt = vt_ref[0, pr * 128:(pr + 1) * 128, pl.ds(koff(j), tq)]
            for hh in range(2):
                bias = nslopes[2 * pr + hh] * dist
                out += [(_dot(k, qms[4 * pr + 2 * hh + mm]) + bias, vt) for mm in range(2)]
        return out

    o = _flash_t(i, tq, 4 * pairs, scores_fn, mask)
    for pr in range(pairs):
        o0, o1, o2, o3 = o[4 * pr:4 * pr + 4]
        ot = jnp.where(row < 64, o0 - lam * o1, o2 - lam * o3)
        sq = ot * ot
        ms0 = jnp.sum(sq[0:64], axis=0, keepdims=True) * (1.0 / A_VDIM)
        ms1 = jnp.sum(sq[64:128], axis=0, keepdims=True) * (1.0 / A_VDIM)
        ms = jnp.where(row < 64, ms0, ms1)
        yt = ot * lax.rsqrt(ms + RMS_EPS) * g_ref[...] * (1.0 - lam_init)
        o_ref[0, :, pr * 128:(pr + 1) * 128] = yt.T.astype(BF16)


def _attn_b_kernel(qt_ref, k_ref, vt_ref, o_ref, *, tq, pairs):
    i = pl.program_id(2)
    row = lax.broadcasted_iota(jnp.int32, (128, tq), 0)
    mask = _chunk_mask_t(tq)
    heads = 2 * pairs

    def koff(j):
        return pl.multiple_of(j * tq, tq)

    qts = [qt_ref[0, h * B_PAD:(h + 1) * B_PAD, :] for h in range(heads)]

    def scores_fn(j):
        vts = [vt_ref[0, pr * 128:(pr + 1) * 128, pl.ds(koff(j), tq)] for pr in range(pairs)]
        return [(_dot(k_ref[0, pl.ds(koff(j), tq), h * B_PAD:(h + 1) * B_PAD], qts[h]), vts[h // 2])
                for h in range(heads)]

    outs = _flash_t(i, tq, heads, scores_fn, mask)
    for pr in range(pairs):
        o_ref[0, :, pr * 128:(pr + 1) * 128] = jnp.where(
            row < 64, outs[2 * pr], outs[2 * pr + 1]).T.astype(BF16)


def _attn_c_kernel(qt_ref, k_ref, vt_ref, ccol_ref, crow_ref, o_ref, *, tq):
    i = pl.program_id(2)
    row = lax.broadcasted_iota(jnp.int32, (128, tq), 0)
    mask = _causal_mask_t(tq)
    pairs = C_HEADS // 2

    def koff(j):
        return pl.multiple_of(j * tq, tq)

    qms = []
    for pr in range(pairs):
        qt = qt_ref[0, pr * 128:(pr + 1) * 128, :]
        qms += [jnp.where((row >= 64 * hh) & (row < 64 * hh + 64), qt, jnp.zeros_like(qt))
                for hh in range(2)]
    cts = [crow_ref[0, h // 2, h % 2:h % 2 + 1, :] for h in range(C_HEADS)]

    def scores_fn(j):
        out = []
        for pr in range(pairs):
            k = k_ref[0, pl.ds(koff(j), tq), pr * 128:(pr + 1) * 128]
            vt = vt_ref[0, pr * 128:(pr + 1) * 128, pl.ds(koff(j), tq)]
            for hh in range(2):
                cs = ccol_ref[0, pr, pl.ds(koff(j), tq), hh:hh + 1]
                out.append((_dot(k, qms[2 * pr + hh]) + (cts[2 * pr + hh] - cs), vt))
        return out

    outs = _flash_t(i, tq, C_HEADS, scores_fn, mask)
    for pr in range(pairs):
        o_ref[0, :, pr * 128:(pr + 1) * 128] = jnp.where(
            row < 64, outs[2 * pr], outs[2 * pr + 1]).T.astype(BF16)


def _attn_params():
    return pltpu.CompilerParams(
        dimension_semantics=("arbitrary", "arbitrary", "arbitrary"), vmem_limit_bytes=VMEM_LIMIT)


def _attn_a_call(qta, ka, vta, pcol, prow, slopes, lamv, gcol, *, tq, lam_init, pairs):
    b, s, _ = ka.shape
    grid = (b, A_HEADS // (2 * pairs), s // tq)
    w = pairs * 128
    return pl.pallas_call(
        functools.partial(_attn_a_kernel, tq=tq, lam_init=lam_init, pairs=pairs),
        grid=grid,
        in_specs=[
            pl.BlockSpec((1, w, tq), lambda bi, c, i: (bi, c, i)),
            pl.BlockSpec((1, s, w), lambda bi, c, i: (bi, 0, c)),
            pl.BlockSpec((1, w, s), lambda bi, c, i: (bi, c, 0)),
            pl.BlockSpec((1, s, 1), lambda bi, c, i: (bi, 0, 0)),
            pl.BlockSpec((1, 1, tq), lambda bi, c, i: (bi, 0, i)),
            pl.BlockSpec((pairs, 8, 128), lambda bi, c, i: (c, 0, 0)),
            pl.BlockSpec((8, 128), lambda bi, c, i: (0, 0)),
            pl.BlockSpec((128, 1), lambda bi, c, i: (0, 0)),
        ],
        out_specs=pl.BlockSpec((1, tq, w), lambda bi, c, i: (bi, i, c)),
        out_shape=jax.ShapeDtypeStruct((b, s, MIX_A), BF16),
        compiler_params=_attn_params(), name="attn_a",
    )(qta, ka, vta, pcol, prow, slopes, lamv, gcol)


def _attn_b_call(qtb, kb, vtb, *, tq, pairs):
    b, s, _ = kb.shape
    grid = (b, B_HEADS // (2 * pairs), s // tq)
    return pl.pallas_call(
        functools.partial(_attn_b_kernel, tq=tq, pairs=pairs),
        grid=grid,
        in_specs=[
            pl.BlockSpec((1, 2 * pairs * B_PAD, tq), lambda bi, c, i: (bi, c, i)),
            pl.BlockSpec((1, s, 2 * pairs * B_PAD), lambda bi, c, i: (bi, 0, c)),
            pl.BlockSpec((1, pairs * 128, s), lambda bi, c, i: (bi, c, 0)),
        ],
        out_specs=pl.BlockSpec((1, tq, pairs * 128), lambda bi, c, i: (bi, i, c)),
        out_shape=jax.ShapeDtypeStruct((b, s, MIX_B), BF16),
        compiler_params=_attn_params(), name="attn_b",
    )(qtb, kb, vtb)


def _attn_c_call(qtc, kc, vtc, ccol, crow, *, tq):
    b, s, _ = kc.shape
    grid = (b, 1, s // tq)
    return pl.pallas_call(
        functools.partial(_attn_c_kernel, tq=tq),
        grid=grid,
        in_specs=[
            pl.BlockSpec((1, MIX_C, tq), lambda bi, c, i: (bi, 0, i)),
            pl.BlockSpec((1, s, MIX_C), lambda bi, c, i: (bi, 0, 0)),
            pl.BlockSpec((1, MIX_C, s), lambda bi, c, i: (bi, 0, 0)),
            pl.BlockSpec((1, 2, s, 2), lambda bi, c, i: (bi, 0, 0, 0)),
            pl.BlockSpec((1, 2, 2, tq), lambda bi, c, i: (bi, 0, 0, i)),
        ],
        out_specs=pl.BlockSpec((1, tq, MIX_C), lambda bi, c, i: (bi, i, 0)),
        out_shape=jax.ShapeDtypeStruct((b, s, MIX_C), BF16),
        compiler_params=_attn_params(), name="attn_c",
    )(qtc, kc, vtc, ccol, crow)


def _layer_norm(v, g, b):
    mu = jnp.mean(v, axis=-1, keepdims=True)
    c = v - mu
    var = jnp.mean(c * c, axis=-1, keepdims=True)
    return c * lax.rsqrt(var + LN_EPS) * g + b


def _route(h, wr1_ref, wr2_ref, rbias_ref):
    tm = h.shape[0]
    h1 = h.astype(BF16)
    h2 = (h - h1.astype(F32)).astype(BF16)
    w1 = wr1_ref[...]
    logits = _dot(h1, w1) + _dot(h2, w1) + _dot(h1, wr2_ref[...]) + rbias_ref[...]
    col = lax.broadcasted_iota(jnp.int32, (tm, ROUTER_COLS), 1)
    colf = col.astype(F32)
    big = float(ROUTER_COLS)
    gmask = col < N_GROUPS
    gmax = jnp.max(jnp.where(gmask, logits, NEG_BIG), axis=1, keepdims=True)
    g_idx = jnp.min(jnp.where(gmask & (logits == gmax), colf, big), axis=1, keepdims=True)
    g_w = 1.0 / jnp.sum(jnp.where(gmask, jnp.exp(logits - gmax), 0.0), axis=1, keepdims=True)
    lo = N_GROUPS + g_idx * EXPERTS_PER_GROUP
    emask = (colf >= lo) & (colf < lo + EXPERTS_PER_GROUP)
    top1 = jnp.max(jnp.where(emask, logits, NEG_BIG), axis=1, keepdims=True)
    idx1 = jnp.min(jnp.where(emask & (logits == top1), colf, big), axis=1, keepdims=True)
    emask2 = emask & (colf != idx1)
    top2 = jnp.max(jnp.where(emask2, logits, NEG_BIG), axis=1, keepdims=True)
    idx2 = jnp.min(jnp.where(emask2 & (logits == top2), colf, big), axis=1, keepdims=True)
    e2 = jnp.exp(top2 - top1)
    w_1 = g_w / (1.0 + e2)
    w_2 = g_w * e2 / (1.0 + e2)
    return colf, idx1, idx2, w_1, w_2


def _post_kernel(x_ref, oa_ref, ob_ref, oc_ref, p_ref, woa_ref, wob_ref, woc_ref, g1_ref, b1_ref,
                 wr1_ref, wr2_ref, rbias_ref, wpg_ref, wpp_ref,
                 h_ref, base_ref, ri_ref, rw_ref, cnt_ref, carry_ref):
    i = pl.program_id(0)
    tm = x_ref.shape[0]

    @pl.when(i == 0)
    def _():
        carry_ref[...] = jnp.zeros_like(carry_ref)

    mix = (_dot(oa_ref[...], woa_ref[...]) + _dot(ob_ref[...], wob_ref[...])
           + _dot(oc_ref[...], woc_ref[...]))
    h = _layer_norm(DEEPNORM_ALPHA * x_ref[...] + mix, g1_ref[...], b1_ref[...])
    _store_row_tiles(h_ref, h)
    gl = _dot(h.astype(BF16), wpg_ref[...])
    ple = (1.0 / (1.0 + jnp.exp(-gl))) * _dot(p_ref[...].astype(BF16), wpp_ref[...])
    base_ref[...] = DEEPNORM_ALPHA * h + ple

    colf, idx1, idx2, w_1, w_2 = _route(h, wr1_ref, wr2_ref, rbias_ref)
    hit1 = colf == idx1
    hit2 = colf == idx2
    onehot = jnp.where(hit1 | hit2, 1.0, 0.0)
    r_i = lax.broadcasted_iota(jnp.int32, (tm, tm), 0)
    c_i = lax.broadcasted_iota(jnp.int32, (tm, tm), 1)
    before = jnp.where(c_i < r_i, 1.0, 0.0).astype(BF16)
    seen = _dot(before, onehot.astype(BF16)) + carry_ref[0:1, :]
    rank1 = jnp.sum(jnp.where(hit1, seen, 0.0), axis=1, keepdims=True)
    rank2 = jnp.sum(jnp.where(hit2, seen, 0.0), axis=1, keepdims=True)
    total = carry_ref[0:1, :] + jnp.sum(onehot, axis=0, keepdims=True)
    carry_ref[...] = jnp.broadcast_to(total, carry_ref.shape)
    cnt_ref[...] = jnp.broadcast_to(total, cnt_ref.shape)

    lane8 = lax.broadcasted_iota(jnp.int32, (tm, 8), 1)
    ri = jnp.where(lane8 == 0, idx1 - N_GROUPS,
                   jnp.where(lane8 == 1, idx2 - N_GROUPS,
                             jnp.where(lane8 == 2, rank1, jnp.where(lane8 == 3, rank2, 0.0))))
    ri_ref[...] = ri.astype(jnp.int32)
    rw_ref[...] = jnp.where(lane8 == 0, w_1, jnp.where(lane8 == 1, w_2, 0.0))


def _post_call(x2, oa, ob, oc, p2, woa, wob, woc, g1, b1, wr1, wr2, rbias, wpg, wpp, *, tm):
    n, d = x2.shape
    tok = lambda w: pl.BlockSpec((tm, w), lambda t: (t, 0))
    full = lambda a: pl.BlockSpec(a.shape, lambda t: (0,) * a.ndim)
    return pl.pallas_call(
        _post_kernel,
        grid=(n // tm,),
        in_specs=[tok(d), tok(MIX_A), tok(MIX_B), tok(MIX_C), tok(PLE_DIM), full(woa), full(wob),
                  full(woc), full(g1), full(b1), full(wr1), full(wr2), full(rbias), full(wpg),
                  full(wpp)],
        out_specs=(pl.BlockSpec((tm * ROW_TILE, 128), lambda t: (t, 0)), tok(d), tok(8), tok(8),
                   pl.BlockSpec((8, ROUTER_COLS), lambda t: (0, 0))),
        out_shape=(jax.ShapeDtypeStruct((n * ROW_TILE, 128), F32), jax.ShapeDtypeStruct((n, d), F32),
                   jax.ShapeDtypeStruct((n, 8), jnp.int32), jax.ShapeDtypeStruct((n, 8), F32),
                   jax.ShapeDtypeStruct((8, ROUTER_COLS), F32)),
        scratch_shapes=[pltpu.VMEM((8, ROUTER_COLS), F32)],
        compiler_params=pltpu.CompilerParams(
            dimension_semantics=("arbitrary",), vmem_limit_bytes=VMEM_LIMIT),
        name="post",
    )(x2, oa, ob, oc, p2, woa, wob, woc, g1, b1, wr1, wr2, rbias, wpg, wpp)


def _row_copy(src_ref, src_row, dst_ref, dst_row, sem):
    src = pl.ds(pl.multiple_of(src_row * ROW_TILE, ROW_TILE), ROW_TILE)
    dst = pl.ds(pl.multiple_of(dst_row * ROW_TILE, ROW_TILE), ROW_TILE)
    return pltpu.make_async_copy(src_ref.at[src], dst_ref.at[dst], sem)


def _slot_kernel(ri_ref, seg_ref, slot_ref):
    tm = ri_ref.shape[0]
    ri = ri_ref[...]
    colf = lax.broadcasted_iota(jnp.int32, (tm, ROUTER_COLS), 1).astype(F32)
    seg = seg_ref[...]
    lane8 = lax.broadcasted_iota(jnp.int32, (tm, 8), 1)
    out = jnp.zeros((tm, 8), F32)
    for k in range(2):
        e = ri[:, k:k + 1].astype(F32)
        start = jnp.sum(jnp.where(colf == e, seg, 0.0), axis=1, keepdims=True)
        out = jnp.where(lane8 == k, start + ri[:, 2 + k:3 + k].astype(F32), out)
    slot_ref[...] = out.astype(jnp.int32)


def _slot_call(ri, seg_row, *, tm):
    n = ri.shape[0]
    return pl.pallas_call(
        _slot_kernel,
        grid=(n // tm,),
        in_specs=[pl.BlockSpec((tm, 8), lambda t: (t, 0)),
                  pl.BlockSpec((1, ROUTER_COLS), lambda t: (0, 0))],
        out_specs=pl.BlockSpec((tm, 8), lambda t: (t, 0)),
        out_shape=jax.ShapeDtypeStruct((n, 8), jnp.int32),
        compiler_params=pltpu.CompilerParams(dimension_semantics=("arbitrary",)),
        name="slots",
    )(ri, seg_row)


def _dispatch_kernel(seg_ref, pc_ref, nv_ref, slot_ref, h_ref, xs_hbm, zbuf, sem, zsem, *, tb, tg):
    i = pl.program_id(0)
    n_tiles = xs_hbm.shape[0] // (tg * ROW_TILE)

    def unused_tile_zero_copy(jt):
        start = pl.multiple_of(jt * (tg * ROW_TILE), tg * ROW_TILE)
        return pltpu.make_async_copy(zbuf, xs_hbm.at[pl.ds(start, tg * ROW_TILE)], zsem)

    def for_unused_tiles(fn):
        def body(jt, c):
            fn(unused_tile_zero_copy(jt))
            return c
        lax.fori_loop(nv_ref[0], n_tiles, body, 0)

    def tail_zero_copy(e):
        start = pl.multiple_of((seg_ref[e] + pc_ref[e] - tg) * ROW_TILE, tg * ROW_TILE)
        return pltpu.make_async_copy(zbuf, xs_hbm.at[pl.ds(start, tg * ROW_TILE)], zsem)

    @pl.when(i == 0)
    def _():
        zbuf[...] = jnp.zeros_like(zbuf)
        for e in range(N_EXPERTS):
            @pl.when(pc_ref[e] > 0)
            def _():
                tail_zero_copy(e).start()
        for e in range(N_EXPERTS):
            @pl.when(pc_ref[e] > 0)
            def _():
                tail_zero_copy(e).wait()
        for_unused_tiles(lambda cp: cp.start())
        for_unused_tiles(lambda cp: cp.wait())

    def body(g, c):
        t0 = g * DMA_BATCH
        dst = [slot_ref[2 * t0 + u] for u in range(2 * DMA_BATCH)]
        for u in range(2 * DMA_BATCH):
            _row_copy(h_ref, t0 + u // 2, xs_hbm, dst[u], sem).start()
        return c

    lax.fori_loop(0, tb // DMA_BATCH, body, 0)
    for _ in range(2):
        pltpu.make_async_copy(h_ref, xs_hbm.at[pl.ds(0, tb * ROW_TILE)], sem).wait()


def _dispatch_call(seg, pc, nvalid, slot_flat, h, *, rows, tb, tg):
    n = h.shape[0] // ROW_TILE
    smem_blk = pl.BlockSpec((2 * tb,), lambda i, seg, pc, nv: (i,), memory_space=pltpu.SMEM)
    grid_spec = pltpu.PrefetchScalarGridSpec(
        num_scalar_prefetch=3, grid=(n // tb,),
        in_specs=[smem_blk,
                  pl.BlockSpec((tb * ROW_TILE, 128), lambda i, seg, pc, nv: (i, 0))],
        out_specs=pl.BlockSpec(memory_space=pl.ANY),
        scratch_shapes=[pltpu.VMEM((tg * ROW_TILE, 128), F32), pltpu.SemaphoreType.DMA(()),
                        pltpu.SemaphoreType.DMA(())])
    return pl.pallas_call(
        functools.partial(_dispatch_kernel, tb=tb, tg=tg),
        grid_spec=grid_spec,
        out_shape=jax.ShapeDtypeStruct((rows * ROW_TILE, 128), F32),
        compiler_params=pltpu.CompilerParams(
            dimension_semantics=("arbitrary",), vmem_limit_bytes=VMEM_LIMIT),
        name="dispatch",
    )(seg, pc, nvalid, slot_flat, h)


def _experts_kernel(te_ref, nv_ref, xs_ref, wg_ref, wu_ref, wd_ref, ys_ref, wgb_ref, wub_ref,
                    wdb_ref):
    j = pl.program_id(0)

    @pl.when(j < nv_ref[0])
    def _():
        prev = te_ref[jnp.maximum(j - 1, 0)]

        @pl.when((j == 0) | (te_ref[j] != prev))
        def _():
            wgb_ref[...] = wg_ref[0, 0].astype(BF16)
            wub_ref[...] = wu_ref[0, 0].astype(BF16)
            wdb_ref[...] = wd_ref[0, 0].astype(BF16)

        tg = xs_ref.shape[0] // ROW_TILE
        xb = _load_row_tiles(xs_ref, 0, tg).astype(BF16)
        gate = _dot(xb, wgb_ref[...])
        up = _dot(xb, wub_ref[...])
        act = gate * (1.0 / (1.0 + jnp.exp(-gate))) * up
        _store_row_tiles(ys_ref, _dot(act.astype(BF16), wdb_ref[...]))

    @pl.when(j >= nv_ref[0])
    def _():
        ys_ref[...] = jnp.zeros_like(ys_ref)


def _experts_call(tile_e, nvalid, xs, wg, wu, wd, *, tg, layer):
    rows = xs.shape[0] // ROW_TILE
    last = lambda j, te, nv: jnp.minimum(j, nv[0] - 1)
    row_blk = pl.BlockSpec((tg * ROW_TILE, 128), lambda j, te, nv: (last(j, te, nv), 0))
    out_blk = pl.BlockSpec((tg * ROW_TILE, 128), lambda j, te, nv: (j, 0))
    w_blk = lambda a: pl.BlockSpec((1, 1) + a.shape[2:],
                                   lambda j, te, nv: (layer, te[last(j, te, nv)], 0, 0))
    grid_spec = pltpu.PrefetchScalarGridSpec(
        num_scalar_prefetch=2, grid=(rows // tg,),
        in_specs=[row_blk, w_blk(wg), w_blk(wu), w_blk(wd)],
        out_specs=out_blk,
        scratch_shapes=[pltpu.VMEM(wg.shape[2:], BF16), pltpu.VMEM(wu.shape[2:], BF16),
                        pltpu.VMEM(wd.shape[2:], BF16)])
    return pl.pallas_call(
        _experts_kernel,
        grid_spec=grid_spec,
        out_shape=jax.ShapeDtypeStruct((rows * ROW_TILE, 128), F32),
        compiler_params=pltpu.CompilerParams(
            dimension_semantics=("arbitrary",), vmem_limit_bytes=VMEM_LIMIT),
        name="experts",
    )(tile_e, nvalid, xs, wg, wu, wd)


def _combine_kernel(cur_ref, nxt_ref, base_ref, rw_ref, g2_ref, b2_ref, ys_hbm, y_ref, ybuf, sem,
                    *, tc):
    i = pl.program_id(0)
    n_steps = pl.num_programs(0)
    slot = i % 2

    def gathered(slot_):
        return pltpu.make_async_copy(ys_hbm.at[pl.ds(0, 2 * tc * ROW_TILE)], ybuf.at[slot_],
                                     sem.at[slot_])

    def issue(rows_ref, slot_):
        def body(g, c):
            t0 = g * DMA_BATCH
            src = [rows_ref[2 * t0 + u] for u in range(2 * DMA_BATCH)]
            for u in range(2 * DMA_BATCH):
                _row_copy(ys_hbm, src[u], ybuf.at[slot_], (u % 2) * tc + t0 + u // 2,
                          sem.at[slot_]).start()
            return c
        lax.fori_loop(0, tc // DMA_BATCH, body, 0)

    @pl.when(i == 0)
    def _():
        issue(cur_ref, 0)

    @pl.when(i + 1 < n_steps)
    def _():
        issue(nxt_ref, 1 - slot)

    gathered(slot).wait()
    y1 = _load_row_tiles(ybuf.at[slot], 0, tc)
    y2 = _load_row_tiles(ybuf.at[slot], tc, tc)
    rw = rw_ref[...]
    moe = rw[:, 0:1] * y1 + rw[:, 1:2] * y2
    y_ref[...] = _layer_norm(base_ref[...] + moe, g2_ref[...], b2_ref[...])


def _combine_call(slot_flat, base, rw, g2, b2, ys, *, tc):
    n, d = base.shape
    n_steps = n // tc
    cur = pl.BlockSpec((2 * tc,), lambda i: (i,), memory_space=pltpu.SMEM)
    nxt = pl.BlockSpec((2 * tc,), lambda i: (jnp.minimum(i + 1, n_steps - 1),),
                       memory_space=pltpu.SMEM)
    tok = lambda w: pl.BlockSpec((tc, w), lambda i: (i, 0))
    full = lambda a: pl.BlockSpec(a.shape, lambda i: (0,) * a.ndim)
    return pl.pallas_call(
        functools.partial(_combine_kernel, tc=tc),
        grid=(n_steps,),
        in_specs=[cur, nxt, tok(d), tok(8), full(g2), full(b2), pl.BlockSpec(memory_space=pl.ANY)],
        out_specs=tok(d),
        out_shape=jax.ShapeDtypeStruct((n, d), F32),
        scratch_shapes=[pltpu.VMEM((2, 2 * tc * ROW_TILE, 128), F32),
                        pltpu.SemaphoreType.DMA((2,))],
        compiler_params=pltpu.CompilerParams(
            dimension_semantics=("arbitrary",), vmem_limit_bytes=VMEM_LIMIT),
        name="combine",
    )(slot_flat, slot_flat, base, rw, g2, b2, ys)


def _expert_segments(counts, tg, n_tiles):
    cnt = counts[0, N_GROUPS:N_GROUPS + N_EXPERTS].astype(jnp.int32)
    pc = ((cnt + tg - 1) // tg) * tg
    seg_end = jnp.cumsum(pc)
    seg = seg_end - pc
    nvalid = (seg_end[-1] // tg).reshape(1)
    tile_e = jnp.searchsorted(seg_end, jnp.arange(n_tiles, dtype=jnp.int32) * tg, side="right")
    tile_e = jnp.minimum(tile_e, N_EXPERTS - 1).astype(jnp.int32)
    return seg.astype(jnp.int32), pc.astype(jnp.int32), tile_e, nvalid.astype(jnp.int32)


def _cols(w, name):
    lo, hi = _OFF[name]
    return w[:, lo:hi]


def _rot_pairs(w):
    return jnp.concatenate([-w[..., 16:32], w[..., 0:16]], axis=-1)


def _layer_weights(w_in, w_uq, w_ukv, b_forget):
    wmain = jnp.concatenate([_cols(w_in, n) for n in ("ka", "cq", "ckv", "kc")], axis=1)
    kr = _cols(w_in, "kr")
    fc_pad = jnp.pad(_cols(w_in, "fc"), ((0, 0), (0, FC_ROWS - C_HEADS)))
    wt = jnp.concatenate([_cols(w_in, "qa"), _cols(w_in, "va"), _cols(w_in, "qc"), _cols(w_in, "vc"),
                          kr, _rot_pairs(kr), fc_pad], axis=1).T
    wq = w_uq.reshape(B_Q_RANK, B_HEADS, B_NOPE + B_ROPE)
    zpad = jnp.zeros((B_Q_RANK, B_HEADS, B_PAD - B_NOPE - B_ROPE), w_uq.dtype)
    plain = jnp.concatenate([wq, zpad], axis=-1).reshape(B_Q_RANK, B_HEADS * B_PAD)
    rot = jnp.concatenate([jnp.zeros_like(wq[..., :B_NOPE]), _rot_pairs(wq[..., B_NOPE:]), zpad],
                          axis=-1).reshape(B_Q_RANK, B_HEADS * B_PAD)
    wuq = jnp.concatenate([plain, rot], axis=1).T
    wkv = w_ukv.reshape(B_KV_RANK, B_HEADS, B_NOPE + B_VDIM)
    wukvk = wkv[..., :B_NOPE].reshape(B_KV_RANK, B_HEADS * B_NOPE).T
    wukvv = wkv[..., B_NOPE:].reshape(B_KV_RANK, B_HEADS * B_VDIM).T
    bfc = jnp.pad(b_forget, (0, FC_ROWS - C_HEADS)).reshape(FC_ROWS, 1)
    return (wmain.astype(BF16), wt.astype(BF16), wuq.astype(BF16), wukvk.astype(BF16),
            wukvv.astype(BF16), bfc.astype(F32))


def _rope_inputs(positions):
    half = B_ROPE // 2
    inv = ROPE_THETA ** (-jnp.arange(half, dtype=F32) / half)
    ang = positions.astype(F32)[..., None] * inv
    cos, sin = jnp.cos(ang), jnp.sin(ang)
    b, s, _ = cos.shape
    ones = jnp.ones((b, s, B_NOPE), F32)
    zeros = jnp.zeros((b, s, B_PAD - B_NOPE - B_ROPE), F32)
    cos128 = jnp.concatenate([ones, cos, cos, zeros], axis=-1)
    sin128 = jnp.concatenate([jnp.zeros_like(ones), sin, sin, zeros], axis=-1)
    cost = jnp.swapaxes(jnp.concatenate([cos, cos], axis=-1), 1, 2)
    sint = jnp.swapaxes(jnp.concatenate([sin, sin], axis=-1), 1, 2)
    return jnp.swapaxes(cos128, 1, 2), jnp.swapaxes(sin128, 1, 2), cost, sint


def kernel(x, p, positions, w_in, w_uq, w_ukv, g_cq, g_ckv, lam_q1, lam_k1, lam_q2, lam_k2, g_diff,
           b_forget, w_out, ln1_g, ln1_b, w_group, b_group, w_erouter, b_erouter, w_gate_e, w_up_e,
           w_down_e, w_ple_gate, w_ple_proj, ln2_g, ln2_b):
    b, s, d = x.shape
    n = b * s
    tq = 512
    tm_prep = 512
    tm_moe = 512
    tg = 512
    rows = 2 * n + N_EXPERTS * tg
    tb = 512
    tc = 512
    cos128, sin128, cost, sint = _rope_inputs(positions)
    posf = positions.astype(F32)
    pcol = posf[:, :, None]
    prow = posf[:, None, :]
    slopes = 2.0 ** (-8.0 * jnp.arange(1, A_HEADS + 1, dtype=F32) / A_HEADS)
    slopes = jnp.broadcast_to(
        jnp.pad(slopes.reshape(3, 2), ((0, 0), (0, 6)))[:, :, None], (3, 8, 128))
    row = lambda v: v.reshape(1, -1).astype(F32)

    for i in range(DEPTH):
        wmain, wt, wuq, wukvk, wukvv, bfc = _layer_weights(w_in[i], w_uq[i], w_ukv[i], b_forget[i])
        ka, kb, kc, qta, qtb, qtc, vta, vtb, vtc, cumt = _prep_call(
            x, wmain, wt, wuq, wukvk, wukvv, row(g_cq[i]), row(g_ckv[i]), bfc,
            cos128, sin128, cost, sint, tm=tm_prep)
        lamv = jnp.pad(jnp.stack([lam_q1[i], lam_k1[i], lam_q2[i], lam_k2[i]]).astype(F32),
                       ((0, 4), (0, 128 - A_DIM)))
        gcol = jnp.tile(g_diff[i].astype(F32), 2).reshape(128, 1)
        lam_init = 0.8 - 0.6 * math.exp(-0.3 * i)
        oa = _attn_a_call(qta, ka, vta, pcol, prow, slopes, lamv, gcol, tq=tq, lam_init=lam_init,
                          pairs=1)
        ob = _attn_b_call(qtb, kb, vtb, tq=tq, pairs=3)
        crow = cumt[:, :C_HEADS].reshape(b, 2, 2, s)
        ccol = jnp.swapaxes(crow, 2, 3)
        oc = _attn_c_call(qtc, kc, vtc, ccol, crow, tq=tq)

        wo = w_out[i].astype(BF16)
        wr = jnp.concatenate([w_group[i], jnp.moveaxis(w_erouter[i], 0, 1).reshape(d, N_EXPERTS)],
                             axis=1)
        wr = jnp.pad(wr, ((0, 0), (0, ROUTER_COLS - wr.shape[1]))).astype(F32)
        wr1 = wr.astype(BF16)
        wr2 = (wr - wr1.astype(F32)).astype(BF16)
        rbias = jnp.pad(jnp.concatenate([b_group[i], b_erouter[i].reshape(-1)]),
                        (0, ROUTER_COLS - N_GROUPS - N_EXPERTS)).reshape(1, ROUTER_COLS).astype(F32)
        h, base, ri, rw, counts = _post_call(
            x.reshape(n, d), oa.reshape(n, MIX_A), ob.reshape(n, MIX_B), oc.reshape(n, MIX_C),
            p[i].reshape(n, PLE_DIM), wo[:MIX_A], wo[MIX_A:MIX_A + MIX_B], wo[MIX_A + MIX_B:],
            row(ln1_g[i]), row(ln1_b[i]), wr1, wr2, rbias, w_ple_gate[i].astype(BF16),
            w_ple_proj[i].astype(BF16), tm=tm_moe)
        seg, pc, tile_e, nvalid = _expert_segments(counts, tg, rows // tg)
        seg_row = jnp.pad(seg.astype(F32), (0, ROUTER_COLS - N_EXPERTS)).reshape(1, ROUTER_COLS)
        slot_flat = _slot_call(ri, seg_row, tm=tm_moe)[:, 0:2].reshape(-1)
        xs = _dispatch_call(seg, pc, nvalid, slot_flat, h, rows=rows, tb=tb, tg=tg)
        ys = _experts_call(tile_e, nvalid, xs, w_gate_e, w_up_e, w_down_e, tg=tg, layer=i)
        y = _combine_call(slot_flat, base, rw, row(ln2_g[i]), row(ln2_b[i]), ys, tc=tc)
        x = y.reshape(b, s, d)
    return x
```

```python
import functools
import math

import jax
import jax.numpy as jnp
from jax import lax
from jax.experimental import pallas as pl
from jax.experimental.pallas import tpu as pltpu

F32 = jnp.float32
BF16 = jnp.bfloat16

D_MODEL = 1024
DEPTH = 2
CHUNK = 64
PLE_DIM = 256

A_HEADS = 6
A_DIM = 32
A_VDIM = 64
B_HEADS = 6
B_Q_RANK = 256
B_KV_RANK = 128
B_NOPE = 64
B_ROPE = 32
B_VDIM = 64
ROPE_THETA = 10000.0
C_HEADS = 4
C_DIM = 64

MIX_A = A_HEADS * A_VDIM
MIX_B = B_HEADS * B_VDIM
MIX_C = C_HEADS * C_DIM
B_PAD = 128

N_GROUPS = 4
EXPERTS_PER_GROUP = 8
N_EXPERTS = N_GROUPS * EXPERTS_PER_GROUP
D_EXPERT = 256
ROUTER_COLS = 128

DEEPNORM_ALPHA = (2 * DEPTH) ** 0.25
LN_EPS = 1e-5
RMS_EPS = 1e-6
NEG_BIG = -1e30
LOG2E = math.log2(math.e)

_OFF = {}
_o = 0
for _name, _w in (("qa", 384), ("ka", 384), ("va", 384), ("cq", 256), ("ckv", 128), ("kr", 32),
                  ("qc", 256), ("kc", 256), ("vc", 256), ("fc", 4)):
    _OFF[_name] = (_o, _o + _w)
    _o += _w

T_QA, T_VA, T_QC, T_VC, T_KR, T_KRS, T_FC, T_ROWS = 0, 384, 768, 1024, 1280, 1312, 1344, 1360
FC_ROWS = T_ROWS - T_FC

VMEM_LIMIT = 48 * 1024 * 1024
DMA_BATCH = 4

NT_DIMS = (((1,), (1,)), ((), ()))


def _dot(a, b):
    return jnp.dot(a, b, preferred_element_type=F32)


def _dot_nt(a, b):
    return lax.dot_general(a, b, NT_DIMS, preferred_element_type=F32)


U32 = jnp.uint32
HALF = D_MODEL // 2
ROW_TILE = HALF // 128


def _pack_rows(v):
    hi = lax.bitcast_convert_type(v[:, :HALF].astype(BF16).astype(F32), U32)
    lo = lax.bitcast_convert_type(v[:, HALF:].astype(BF16).astype(F32), U32)
    return hi | lax.shift_right_logical(lo, jnp.full_like(lo, 16))


def _unpack_rows(w):
    first = lax.bitcast_convert_type(w & jnp.full_like(w, 0xFFFF0000), F32)
    second = lax.bitcast_convert_type(lax.shift_left(w, jnp.full_like(w, 16)), F32)
    return first, second


def _store_row_tiles(ref, v):
    rows = v.shape[0]
    w = _pack_rows(v)
    for c in range(ROW_TILE):
        ref[pl.ds(c, rows, stride=ROW_TILE), :] = w[:, c * 128:(c + 1) * 128]


def _load_row_tiles(ref, start, rows):
    w = jnp.concatenate(
        [ref[pl.ds(start * ROW_TILE + c, rows, stride=ROW_TILE), :] for c in range(ROW_TILE)],
        axis=1)
    return _unpack_rows(w)


def _split3(a):
    a1 = a.astype(BF16)
    r1 = a - a1.astype(F32)
    a2 = r1.astype(BF16)
    a3 = (r1 - a2.astype(F32)).astype(BF16)
    return a1, a2, a3


def _prep_kernel(x_ref, wmain_ref, wt_ref, wuq_ref, wukvk_ref, wukvv_ref, gcq_ref, gckv_ref,
                 bf_ref, cos_ref, sin_ref, cost_ref, sint_ref,
                 ka_ref, kb_ref, kc_ref, qta_ref, qtb_ref, qtc_ref, vta_ref, vtb_ref, vtc_ref,
                 cum_ref, carry_ref, *, tm):
    j = pl.program_id(1)

    @pl.when(j == 0)
    def _():
        carry_ref[...] = jnp.zeros_like(carry_ref)

    xb = x_ref[0].astype(BF16)
    z = _dot(xb, wmain_ref[...])
    zt = _dot_nt(wt_ref[...], xb)

    ka_ref[0] = z[:, 0:384].astype(BF16)
    qta_ref[0] = (zt[T_QA:T_QA + 384] * (LOG2E / math.sqrt(A_DIM))).astype(BF16)
    vta_ref[0] = zt[T_VA:T_VA + 384].astype(BF16)
    kc_ref[0] = z[:, 768:1024].astype(BF16)
    qtc_ref[0] = (zt[T_QC:T_QC + 256] * (LOG2E / math.sqrt(C_DIM))).astype(BF16)
    vtc_ref[0] = zt[T_VC:T_VC + 256].astype(BF16)

    fct = zt[T_FC:T_ROWS] + bf_ref[...]
    logf = jnp.minimum(fct, 0.0) - jnp.log(1.0 + jnp.exp(-jnp.abs(fct)))
    r_i = lax.broadcasted_iota(jnp.int32, (tm, tm), 0)
    c_i = lax.broadcasted_iota(jnp.int32, (tm, tm), 1)
    tri = jnp.where(r_i <= c_i, 1.0, 0.0).astype(BF16)
    l1, l2, l3 = _split3(logf)
    cum = _dot(l1, tri) + _dot(l2, tri) + _dot(l3, tri) + carry_ref[:, 0:1]
    cum_ref[0] = cum * LOG2E
    carry_ref[...] = jnp.broadcast_to(cum[:, tm - 1:tm], carry_ref.shape)

    cq = z[:, 384:640]
    cq_n = cq * lax.rsqrt(jnp.mean(cq * cq, axis=-1, keepdims=True) + RMS_EPS) * gcq_ref[...]
    zqt = _dot_nt(wuq_ref[...], cq_n.astype(BF16))
    cosb = cos_ref[0]
    sinb = sin_ref[0]
    scale_b = LOG2E / math.sqrt(B_NOPE + B_ROPE)
    for h in range(B_HEADS):
        lo = h * B_PAD
        qh = zqt[lo:lo + B_PAD] * cosb + zqt[768 + lo:768 + lo + B_PAD] * sinb
        qtb_ref[0, lo:lo + B_PAD, :] = (qh * scale_b).astype(BF16)

    ckv = z[:, 640:768]
    ckv_n = ckv * lax.rsqrt(jnp.mean(ckv * ckv, axis=-1, keepdims=True) + RMS_EPS) * gckv_ref[...]
    ckv_b = ckv_n.astype(BF16)
    vtb_ref[0] = _dot_nt(wukvv_ref[...], ckv_b).astype(BF16)
    knt = _dot_nt(wukvk_ref[...], ckv_b)
    krt = zt[T_KR:T_KR + 32] * cost_ref[0] + zt[T_KRS:T_KRS + 32] * sint_ref[0]
    zpad = jnp.zeros((B_PAD - B_NOPE - B_ROPE, tm), F32)
    pieces = []
    for h in range(B_HEADS):
        pieces += [knt[h * B_NOPE:(h + 1) * B_NOPE], krt, zpad]
    kb_ref[0] = jnp.concatenate(pieces, axis=0).T.astype(BF16)


def _prep_call(x, wmain, wt, wuq, wukvk, wukvv, gcq, gckv, bfc, cos128, sin128, cost, sint, *, tm):
    b, s, d = x.shape
    grid = (b, s // tm)
    full = lambda shape: pl.BlockSpec(shape, lambda bi, j: (0,) * len(shape))
    tok = lambda w: pl.BlockSpec((1, tm, w), lambda bi, j: (bi, j, 0))
    tr = lambda r: pl.BlockSpec((1, r, tm), lambda bi, j: (bi, 0, j))
    out_shape = (
        jax.ShapeDtypeStruct((b, s, 384), BF16),
        jax.ShapeDtypeStruct((b, s, 768), BF16),
        jax.ShapeDtypeStruct((b, s, 256), BF16),
        jax.ShapeDtypeStruct((b, 384, s), BF16),
        jax.ShapeDtypeStruct((b, 768, s), BF16),
        jax.ShapeDtypeStruct((b, 256, s), BF16),
        jax.ShapeDtypeStruct((b, 384, s), BF16),
        jax.ShapeDtypeStruct((b, 384, s), BF16),
        jax.ShapeDtypeStruct((b, 256, s), BF16),
        jax.ShapeDtypeStruct((b, FC_ROWS, s), F32),
    )
    out_specs = (tok(384), tok(768), tok(256), tr(384), tr(768), tr(256), tr(384), tr(384),
                 tr(256), tr(FC_ROWS))
    in_specs = [tok(d), full(wmain.shape), full(wt.shape), full(wuq.shape), full(wukvk.shape),
                full(wukvv.shape), full(gcq.shape), full(gckv.shape), full(bfc.shape),
                tr(128), tr(128), tr(32), tr(32)]
    return pl.pallas_call(
        functools.partial(_prep_kernel, tm=tm),
        grid=grid, in_specs=in_specs, out_specs=out_specs, out_shape=out_shape,
        scratch_shapes=[pltpu.VMEM((FC_ROWS, 128), F32)],
        compiler_params=pltpu.CompilerParams(
            dimension_semantics=("arbitrary", "arbitrary"), vmem_limit_bytes=VMEM_LIMIT),
        name="prep",
    )(x, wmain, wt, wuq, wukvk, wukvv, gcq, gckv, bfc, cos128, sin128, cost, sint)


def _flash(i, tq, n_maps, scores_fn, mask):
    def update(state, j, diag):
        new = []
        for (m, l, acc), (s, v) in zip(state, scores_fn(j)):
            if diag:
                s = jnp.where(mask, s, NEG_BIG)
            m_new = jnp.maximum(m, jnp.max(s, axis=1, keepdims=True))
            alpha = jnp.exp2(m - m_new)
            p = jnp.exp2(s - m_new)
            l_new = alpha * l + jnp.sum(p, axis=1, keepdims=True)
            acc_new = alpha * acc + _dot(p.astype(BF16), v)
            new.append((m_new, l_new, acc_new))
        return tuple(new)

    init = tuple((jnp.full((tq, 1), NEG_BIG, F32), jnp.zeros((tq, 1), F32),
                  jnp.zeros((tq, 128), F32)) for _ in range(n_maps))
    state = update(init, i, True)
    state = lax.fori_loop(0, i, lambda j, st: update(st, j, False), state)
    return [acc * (1.0 / l) for (_, l, acc) in state]


def _chunk_mask(tq):
    r = lax.broadcasted_iota(jnp.int32, (tq, tq), 0)
    c = lax.broadcasted_iota(jnp.int32, (tq, tq), 1)
    return (c // CHUNK) <= (r // CHUNK)


def _causal_mask(tq):
    r = lax.broadcasted_iota(jnp.int32, (tq, tq), 0)
    c = lax.broadcasted_iota(jnp.int32, (tq, tq), 1)
    return c <= r


ONES_ROWS = 16


def _flash_t(i, tq, n_maps, scores_fn, mask):
    ones = jnp.ones((ONES_ROWS, tq), BF16)

    def update(state, j, diag):
        new = []
        for (m, acc), (s, vt) in zip(state, scores_fn(j)):
            if diag:
                s = jnp.where(mask, s, NEG_BIG)
            m_new = jnp.maximum(m, jnp.max(s, axis=0, keepdims=True))
            alpha = jnp.exp2(m - m_new)
            p = jnp.exp2(s - m_new).astype(BF16)
            vext = jnp.concatenate([vt, ones], axis=0)
            new.append((m_new, alpha * acc + _dot(vext, p)))
        return tuple(new)

    init = tuple((jnp.full((1, tq), NEG_BIG, F32), jnp.zeros((128 + ONES_ROWS, tq), F32))
                 for _ in range(n_maps))
    state = update(init, i, True)
    state = lax.fori_loop(0, i, lambda j, st: update(st, j, False), state)
    return [acc[0:128] * (1.0 / acc[128:129]) for (_, acc) in state]


def _chunk_mask_t(tq):
    r = lax.broadcasted_iota(jnp.int32, (tq, tq), 0)
    c = lax.broadcasted_iota(jnp.int32, (tq, tq), 1)
    return (r // CHUNK) <= (c // CHUNK)


def _causal_mask_t(tq):
    r = lax.broadcasted_iota(jnp.int32, (tq, tq), 0)
    c = lax.broadcasted_iota(jnp.int32, (tq, tq), 1)
    return r <= c


def _attn_a_kernel(qt_ref, k_ref, vt_ref, pcol_ref, prow_ref, slope_ref, lam_ref, g_ref, o_ref,
                   *, tq, lam_init, pairs):
    i = pl.program_id(2)
    row = lax.broadcasted_iota(jnp.int32, (128, tq), 0)
    pt = prow_ref[0]
    mask = _chunk_mask_t(tq)
    lam_v = lam_ref[...]
    lam = (jnp.exp(jnp.sum(lam_v[0:1] * lam_v[1:2], axis=1, keepdims=True))
           - jnp.exp(jnp.sum(lam_v[2:3] * lam_v[3:4], axis=1, keepdims=True)) + lam_init)

    def koff(j):
        return pl.multiple_of(j * tq, tq)

    qms, nslopes = [], []
    for pr in range(pairs):
        qt = qt_ref[0, pr * 128:(pr + 1) * 128, :]
        qms += [jnp.where((row >= 32 * mi) & (row < 32 * mi + 32), qt, jnp.zeros_like(qt))
                for mi in range(4)]
        nslopes += [-LOG2E * slope_ref[pr, hh:hh + 1, 0:1] for hh in range(2)]

    def scores_fn(j):
        ps = pcol_ref[0, pl.ds(koff(j), tq), :]
        dist = jnp.abs(ps - pt)
        out = []
        for pr in range(pairs):
            k = k_ref[0, pl.ds(koff(j), tq), pr * 128:(pr + 1) * 128]
            vt = vt_ref[0, pr * 128:(pr + 1) * 128, pl.ds(koff(j), tq)]
            for hh in range(2):
                bias = nslopes[2 * pr + hh] * dist
                out += [(_dot(k, qms[4 * pr + 2 * hh + mm]) + bias, vt) for mm in range(2)]
        return out

    o = _flash_t(i, tq, 4 * pairs, scores_fn, mask)
    for pr in range(pairs):
        o0, o1, o2, o3 = o[4 * pr:4 * pr + 4]
        ot = jnp.where(row < 64, o0 - lam * o1, o2 - lam * o3)
        sq = ot * ot
        ms0 = jnp.sum(sq[0:64], axis=0, keepdims=True) * (1.0 / A_VDIM)
        ms1 = jnp.sum(sq[64:128], axis=0, keepdims=True) * (1.0 / A_VDIM)
        ms = jnp.where(row < 64, ms0, ms1)
        yt = ot * lax.rsqrt(ms + RMS_EPS) * g_ref[...] * (1.0 - lam_init)
        o_ref[0, :, pr * 128:(pr + 1) * 128] = yt.T.astype(BF16)


def _attn_b_kernel(qt_ref, k_ref, vt_ref, o_ref, *, tq, pairs):
    i = pl.program_id(2)
    row = lax.broadcasted_iota(jnp.int32, (128, tq), 0)
    mask = _chunk_mask_t(tq)
    heads = 2 * pairs

    def koff(j):
        return pl.multiple_of(j * tq, tq)

    qts = [qt_ref[0, h * B_PAD:(h + 1) * B_PAD, :] for h in range(heads)]

    def scores_fn(j):
        vts = [vt_ref[0, pr * 128:(pr + 1) * 128, pl.ds(koff(j), tq)] for pr in range(pairs)]
        return [(_dot(k_ref[0, pl.ds(koff(j), tq), h * B_PAD:(h + 1) * B_PAD], qts[h]), vts[h // 2])
                for h in range(heads)]

    outs = _flash_t(i, tq, heads, scores_fn, mask)
    for pr in range(pairs):
        o_ref[0, :, pr * 128:(pr + 1) * 128] = jnp.where(
            row < 64, outs[2 * pr], outs[2 * pr + 1]).T.astype(BF16)


def _attn_c_kernel(qt_ref, k_ref, vt_ref, ccol_ref, crow_ref, o_ref, *, tq):
    i = pl.program_id(2)
    row = lax.broadcasted_iota(jnp.int32, (128, tq), 0)
    mask = _causal_mask_t(tq)
    pairs = C_HEADS // 2

    def koff(j):
        return pl.multiple_of(j * tq, tq)

    qms = []
    for pr in range(pairs):
        qt = qt_ref[0, pr * 128:(pr + 1) * 128, :]
        qms += [jnp.where((row >= 64 * hh) & (row < 64 * hh + 64), qt, jnp.zeros_like(qt))
                for hh in range(2)]
    cts = [crow_ref[0, h // 2, h % 2:h % 2 + 1, :] for h in range(C_HEADS)]

    def scores_fn(j):
        out = []
        for pr in range(pairs):
            k = k_ref[0, pl.ds(koff(j), tq), pr * 128:(pr + 1) * 128]
            vt = vt_ref[0, pr * 128:(pr + 1) * 128, pl.ds(koff(j), tq)]
            for hh in range(2):
                cs = ccol_ref[0, pr, pl.ds(koff(j), tq), hh:hh + 1]
                out.append((_dot(k, qms[2 * pr + hh]) + (cts[2 * pr + hh] - cs), vt))
        return out

    outs = _flash_t(i, tq, C_HEADS, scores_fn, mask)
    for pr in range(pairs):
        o_ref[0, :, pr * 128:(pr + 1) * 128] = jnp.where(
            row < 64, outs[2 * pr], outs[2 * pr + 1]).T.astype(BF16)


def _attn_params():
    return pltpu.CompilerParams(
        dimension_semantics=("arbitrary", "arbitrary", "arbitrary"), vmem_limit_bytes=VMEM_LIMIT)


def _attn_a_call(qta, ka, vta, pcol, prow, slopes, lamv, gcol, *, tq, lam_init, pairs):
    b, s, _ = ka.shape
    grid = (b, A_HEADS // (2 * pairs), s // tq)
    w = pairs * 128
    return pl.pallas_call(
        functools.partial(_attn_a_kernel, tq=tq, lam_init=lam_init, pairs=pairs),
        grid=grid,
        in_specs=[
            pl.BlockSpec((1, w, tq), lambda bi, c, i: (bi, c, i)),
            pl.BlockSpec((1, s, w), lambda bi, c, i: (bi, 0, c)),
            pl.BlockSpec((1, w, s), lambda bi, c, i: (bi, c, 0)),
            pl.BlockSpec((1, s, 1), lambda bi, c, i: (bi, 0, 0)),
            pl.BlockSpec((1, 1, tq), lambda bi, c, i: (bi, 0, i)),
            pl.BlockSpec((pairs, 8, 128), lambda bi, c, i: (c, 0, 0)),
            pl.BlockSpec((8, 128), lambda bi, c, i: (0, 0)),
            pl.BlockSpec((128, 1), lambda bi, c, i: (0, 0)),
        ],
        out_specs=pl.BlockSpec((1, tq, w), lambda bi, c, i: (bi, i, c)),
        out_shape=jax.ShapeDtypeStruct((b, s, MIX_A), BF16),
        compiler_params=_attn_params(), name="attn_a",
    )(qta, ka, vta, pcol, prow, slopes, lamv, gcol)


def _attn_b_call(qtb, kb, vtb, *, tq, pairs):
    b, s, _ = kb.shape
    grid = (b, B_HEADS // (2 * pairs), s // tq)
    return pl.pallas_call(
        functools.partial(_attn_b_kernel, tq=tq, pairs=pairs),
        grid=grid,
        in_specs=[
            pl.BlockSpec((1, 2 * pairs * B_PAD, tq), lambda bi, c, i: (bi, c, i)),
            pl.BlockSpec((1, s, 2 * pairs * B_PAD), lambda bi, c, i: (bi, 0, c)),
            pl.BlockSpec((1, pairs * 128, s), lambda bi, c, i: (bi, c, 0)),
        ],
        out_specs=pl.BlockSpec((1, tq, pairs * 128), lambda bi, c, i: (bi, i, c)),
        out_shape=jax.ShapeDtypeStruct((b, s, MIX_B), BF16),
        compiler_params=_attn_params(), name="attn_b",
    )(qtb, kb, vtb)


def _attn_c_call(qtc, kc, vtc, ccol, crow, *, tq):
    b, s, _ = kc.shape
    grid = (b, 1, s // tq)
    return pl.pallas_call(
        functools.partial(_attn_c_kernel, tq=tq),
        grid=grid,
        in_specs=[
            pl.BlockSpec((1, MIX_C, tq), lambda bi, c, i: (bi, 0, i)),
            pl.BlockSpec((1, s, MIX_C), lambda bi, c, i: (bi, 0, 0)),
            pl.BlockSpec((1, MIX_C, s), lambda bi, c, i: (bi, 0, 0)),
            pl.BlockSpec((1, 2, s, 2), lambda bi, c, i: (bi, 0, 0, 0)),
            pl.BlockSpec((1, 2, 2, tq), lambda bi, c, i: (bi, 0, 0, i)),
        ],
        out_specs=pl.BlockSpec((1, tq, MIX_C), lambda bi, c, i: (bi, i, 0)),
        out_shape=jax.ShapeDtypeStruct((b, s, MIX_C), BF16),
        compiler_params=_attn_params(), name="attn_c",
    )(qtc, kc, vtc, ccol, crow)


def _layer_norm(v, g, b):
    mu = jnp.mean(v, axis=-1, keepdims=True)
    c = v - mu
    var = jnp.mean(c * c, axis=-1, keepdims=True)
    return c * lax.rsqrt(var + LN_EPS) * g + b


def _route(h, wr1_ref, wr2_ref, rbias_ref):
    tm = h.shape[0]
    h1 = h.astype(BF16)
    h2 = (h - h1.astype(F32)).astype(BF16)
    w1 = wr1_ref[...]
    logits = _dot(h1, w1) + _dot(h2, w1) + _dot(h1, wr2_ref[...]) + rbias_ref[...]
    col = lax.broadcasted_iota(jnp.int32, (tm, ROUTER_COLS), 1)
    colf = col.astype(F32)
    big = float(ROUTER_COLS)
    gmask = col < N_GROUPS
    gmax = jnp.max(jnp.where(gmask, logits, NEG_BIG), axis=1, keepdims=True)
    g_idx = jnp.min(jnp.where(gmask & (logits == gmax), colf, big), axis=1, keepdims=True)
    g_w = 1.0 / jnp.sum(jnp.where(gmask, jnp.exp(logits - gmax), 0.0), axis=1, keepdims=True)
    lo = N_GROUPS + g_idx * EXPERTS_PER_GROUP
    emask = (colf >= lo) & (colf < lo + EXPERTS_PER_GROUP)
    top1 = jnp.max(jnp.where(emask, logits, NEG_BIG), axis=1, keepdims=True)
    idx1 = jnp.min(jnp.where(emask & (logits == top1), colf, big), axis=1, keepdims=True)
    emask2 = emask & (colf != idx1)
    top2 = jnp.max(jnp.where(emask2, logits, NEG_BIG), axis=1, keepdims=True)
    idx2 = jnp.min(jnp.where(emask2 & (logits == top2), colf, big), axis=1, keepdims=True)
    e2 = jnp.exp(top2 - top1)
    w_1 = g_w / (1.0 + e2)
    w_2 = g_w * e2 / (1.0 + e2)
    return colf, idx1, idx2, w_1, w_2


def _post_kernel(x_ref, oa_ref, ob_ref, oc_ref, p_ref, woa_ref, wob_ref, woc_ref, g1_ref, b1_ref,
                 wr1_ref, wr2_ref, rbias_ref, wpg_ref, wpp_ref,
                 h_ref, base_ref, ri_ref, rw_ref, cnt_ref, carry_ref):
    i = pl.program_id(0)
    tm = x_ref.shape[0]

    @pl.when(i == 0)
    def _():
        carry_ref[...] = jnp.zeros_like(carry_ref)

    mix = (_dot(oa_ref[...], woa_ref[...]) + _dot(ob_ref[...], wob_ref[...])
           + _dot(oc_ref[...], woc_ref[...]))
    h = _layer_norm(DEEPNORM_ALPHA * x_ref[...] + mix, g1_ref[...], b1_ref[...])
    _store_row_tiles(h_ref, h)
    gl = _dot(h.astype(BF16), wpg_ref[...])
    ple = (1.0 / (1.0 + jnp.exp(-gl))) * _dot(p_ref[...].astype(BF16), wpp_ref[...])
    base_ref[...] = DEEPNORM_ALPHA * h + ple

    colf, idx1, idx2, w_1, w_2 = _route(h, wr1_ref, wr2_ref, rbias_ref)
    hit1 = colf == idx1
    hit2 = colf == idx2
    onehot = jnp.where(hit1 | hit2, 1.0, 0.0)
    r_i = lax.broadcasted_iota(jnp.int32, (tm, tm), 0)
    c_i = lax.broadcasted_iota(jnp.int32, (tm, tm), 1)
    before = jnp.where(c_i < r_i, 1.0, 0.0).astype(BF16)
    seen = _dot(before, onehot.astype(BF16)) + carry_ref[0:1, :]
    rank1 = jnp.sum(jnp.where(hit1, seen, 0.0), axis=1, keepdims=True)
    rank2 = jnp.sum(jnp.where(hit2, seen, 0.0), axis=1, keepdims=True)
    total = carry_ref[0:1, :] + jnp.sum(onehot, axis=0, keepdims=True)
    carry_ref[...] = jnp.broadcast_to(total, carry_ref.shape)
    cnt_ref[...] = jnp.broadcast_to(total, cnt_ref.shape)

    lane8 = lax.broadcasted_iota(jnp.int32, (tm, 8), 1)
    ri = jnp.where(lane8 == 0, idx1 - N_GROUPS,
                   jnp.where(lane8 == 1, idx2 - N_GROUPS,
                             jnp.where(lane8 == 2, rank1, jnp.where(lane8 == 3, rank2, 0.0))))
    ri_ref[...] = ri.astype(jnp.int32)
    rw_ref[...] = jnp.where(lane8 == 0, w_1, jnp.where(lane8 == 1, w_2, 0.0))


def _post_call(x2, oa, ob, oc, p2, woa, wob, woc, g1, b1, wr1, wr2, rbias, wpg, wpp, *, tm):
    n, d = x2.shape
    tok = lambda w: pl.BlockSpec((tm, w), lambda t: (t, 0))
    full = lambda a: pl.BlockSpec(a.shape, lambda t: (0,) * a.ndim)
    return pl.pallas_call(
        _post_kernel,
        grid=(n // tm,),
        in_specs=[tok(d), tok(MIX_A), tok(MIX_B), tok(MIX_C), tok(PLE_DIM), full(woa), full(wob),
                  full(woc), full(g1), full(b1), full(wr1), full(wr2), full(rbias), full(wpg),
                  full(wpp)],
        out_specs=(pl.BlockSpec((tm * ROW_TILE, 128), lambda t: (t, 0)), tok(d), tok(8), tok(8),
                   pl.BlockSpec((8, ROUTER_COLS), lambda t: (0, 0))),
        out_shape=(jax.ShapeDtypeStruct((n * ROW_TILE, 128), U32), jax.ShapeDtypeStruct((n, d), F32),
                   jax.ShapeDtypeStruct((n, 8), jnp.int32), jax.ShapeDtypeStruct((n, 8), F32),
                   jax.ShapeDtypeStruct((8, ROUTER_COLS), F32)),
        scratch_shapes=[pltpu.VMEM((8, ROUTER_COLS), F32)],
        compiler_params=pltpu.CompilerParams(
            dimension_semantics=("arbitrary",), vmem_limit_bytes=VMEM_LIMIT),
        name="post",
    )(x2, oa, ob, oc, p2, woa, wob, woc, g1, b1, wr1, wr2, rbias, wpg, wpp)


def _row_copy(src_ref, src_row, dst_ref, dst_row, sem):
    src = pl.ds(pl.multiple_of(src_row * ROW_TILE, ROW_TILE), ROW_TILE)
    dst = pl.ds(pl.multiple_of(dst_row * ROW_TILE, ROW_TILE), ROW_TILE)
    return pltpu.make_async_copy(src_ref.at[src], dst_ref.at[dst], sem)


def _slot_kernel(ri_ref, seg_ref, slot_ref):
    tm = ri_ref.shape[0]
    ri = ri_ref[...]
    colf = lax.broadcasted_iota(jnp.int32, (tm, ROUTER_COLS), 1).astype(F32)
    seg = seg_ref[...]
    lane8 = lax.broadcasted_iota(jnp.int32, (tm, 8), 1)
    out = jnp.zeros((tm, 8), F32)
    for k in range(2):
        e = ri[:, k:k + 1].astype(F32)
        start = jnp.sum(jnp.where(colf == e, seg, 0.0), axis=1, keepdims=True)
        out = jnp.where(lane8 == k, start + ri[:, 2 + k:3 + k].astype(F32), out)
    slot_ref[...] = out.astype(jnp.int32)


def _slot_call(ri, seg_row, *, tm):
    n = ri.shape[0]
    return pl.pallas_call(
        _slot_kernel,
        grid=(n // tm,),
        in_specs=[pl.BlockSpec((tm, 8), lambda t: (t, 0)),
                  pl.BlockSpec((1, ROUTER_COLS), lambda t: (0, 0))],
        out_specs=pl.BlockSpec((tm, 8), lambda t: (t, 0)),
        out_shape=jax.ShapeDtypeStruct((n, 8), jnp.int32),
        compiler_params=pltpu.CompilerParams(dimension_semantics=("arbitrary",)),
        name="slots",
    )(ri, seg_row)


def _dispatch_kernel(seg_ref, pc_ref, nv_ref, slot_ref, h_ref, xs_hbm, zbuf, sem, zsem, *, tb, tg):
    i = pl.program_id(0)
    n_tiles = xs_hbm.shape[0] // (tg * ROW_TILE)

    def unused_tile_zero_copy(jt):
        start = pl.multiple_of(jt * (tg * ROW_TILE), tg * ROW_TILE)
        return pltpu.make_async_copy(zbuf, xs_hbm.at[pl.ds(start, tg * ROW_TILE)], zsem)

    def for_unused_tiles(fn):
        def body(jt, c):
            fn(unused_tile_zero_copy(jt))
            return c
        lax.fori_loop(nv_ref[0], n_tiles, body, 0)

    def tail_zero_copy(e):
        start = pl.multiple_of((seg_ref[e] + pc_ref[e] - tg) * ROW_TILE, tg * ROW_TILE)
        return pltpu.make_async_copy(zbuf, xs_hbm.at[pl.ds(start, tg * ROW_TILE)], zsem)

    @pl.when(i == 0)
    def _():
        zbuf[...] = jnp.zeros_like(zbuf)
        for e in range(N_EXPERTS):
            @pl.when(pc_ref[e] > 0)
            def _():
                tail_zero_copy(e).start()
        for e in range(N_EXPERTS):
            @pl.when(pc_ref[e] > 0)
            def _():
                tail_zero_copy(e).wait()
        for_unused_tiles(lambda cp: cp.start())
        for_unused_tiles(lambda cp: cp.wait())

    def body(g, c):
        t0 = g * DMA_BATCH
        dst = [slot_ref[2 * t0 + u] for u in range(2 * DMA_BATCH)]
        for u in range(2 * DMA_BATCH):
            _row_copy(h_ref, t0 + u // 2, xs_hbm, dst[u], sem).start()
        return c

    lax.fori_loop(0, tb // DMA_BATCH, body, 0)
    for _ in range(2):
        pltpu.make_async_copy(h_ref, xs_hbm.at[pl.ds(0, tb * ROW_TILE)], sem).wait()


def _dispatch_call(seg, pc, nvalid, slot_flat, h, *, rows, tb, tg):
    n = h.shape[0] // ROW_TILE
    smem_blk = pl.BlockSpec((2 * tb,), lambda i, seg, pc, nv: (i,), memory_space=pltpu.SMEM)
    grid_spec = pltpu.PrefetchScalarGridSpec(
        num_scalar_prefetch=3, grid=(n // tb,),
        in_specs=[smem_blk,
                  pl.BlockSpec((tb * ROW_TILE, 128), lambda i, seg, pc, nv: (i, 0))],
        out_specs=pl.BlockSpec(memory_space=pl.ANY),
        scratch_shapes=[pltpu.VMEM((tg * ROW_TILE, 128), U32), pltpu.SemaphoreType.DMA(()),
                        pltpu.SemaphoreType.DMA(())])
    return pl.pallas_call(
        functools.partial(_dispatch_kernel, tb=tb, tg=tg),
        grid_spec=grid_spec,
        out_shape=jax.ShapeDtypeStruct((rows * ROW_TILE, 128), U32),
        compiler_params=pltpu.CompilerParams(
            dimension_semantics=("arbitrary",), vmem_limit_bytes=VMEM_LIMIT),
        name="dispatch",
    )(seg, pc, nvalid, slot_flat, h)


def _experts_kernel(te_ref, nv_ref, xs_ref, wg_ref, wu_ref, wd_ref, ys_ref, wgb_ref, wub_ref,
                    wdb_ref):
    j = pl.program_id(0)

    @pl.when(j < nv_ref[0])
    def _():
        prev = te_ref[jnp.maximum(j - 1, 0)]

        @pl.when((j == 0) | (te_ref[j] != prev))
        def _():
            wgb_ref[...] = wg_ref[0, 0].astype(BF16)
            wub_ref[...] = wu_ref[0, 0].astype(BF16)
            wdb_ref[...] = wd_ref[0, 0].astype(BF16)

        tg = xs_ref.shape[0] // ROW_TILE
        x1, x2 = (t.astype(BF16) for t in _load_row_tiles(xs_ref, 0, tg))
        gate = _dot(x1, wgb_ref[0:HALF, :]) + _dot(x2, wgb_ref[HALF:D_MODEL, :])
        up = _dot(x1, wub_ref[0:HALF, :]) + _dot(x2, wub_ref[HALF:D_MODEL, :])
        act = gate * (1.0 / (1.0 + jnp.exp(-gate))) * up
        _store_row_tiles(ys_ref, _dot(act.astype(BF16), wdb_ref[...]))

    @pl.when(j >= nv_ref[0])
    def _():
        ys_ref[...] = jnp.zeros_like(ys_ref)


def _experts_call(tile_e, nvalid, xs, wg, wu, wd, *, tg, layer):
    rows = xs.shape[0] // ROW_TILE
    last = lambda j, te, nv: jnp.minimum(j, nv[0] - 1)
    row_blk = pl.BlockSpec((tg * ROW_TILE, 128), lambda j, te, nv: (last(j, te, nv), 0))
    out_blk = pl.BlockSpec((tg * ROW_TILE, 128), lambda j, te, nv: (j, 0))
    w_blk = lambda a: pl.BlockSpec((1, 1) + a.shape[2:],
                                   lambda j, te, nv: (layer, te[last(j, te, nv)], 0, 0))
    grid_spec = pltpu.PrefetchScalarGridSpec(
        num_scalar_prefetch=2, grid=(rows // tg,),
        in_specs=[row_blk, w_blk(wg), w_blk(wu), w_blk(wd)],
        out_specs=out_blk,
        scratch_shapes=[pltpu.VMEM(wg.shape[2:], BF16), pltpu.VMEM(wu.shape[2:], BF16),
                        pltpu.VMEM(wd.shape[2:], BF16)])
    return pl.pallas_call(
        _experts_kernel,
        grid_spec=grid_spec,
        out_shape=jax.ShapeDtypeStruct((rows * ROW_TILE, 128), U32),
        compiler_params=pltpu.CompilerParams(
            dimension_semantics=("arbitrary",), vmem_limit_bytes=VMEM_LIMIT),
        name="experts",
    )(tile_e, nvalid, xs, wg, wu, wd)


def _combine_kernel(cur_ref, nxt_ref, base_ref, rw_ref, g2_ref, b2_ref, ys_hbm, y_ref, ybuf, sem,
                    *, tc):
    i = pl.program_id(0)
    n_steps = pl.num_programs(0)
    slot = i % 2

    def gathered(slot_):
        return pltpu.make_async_copy(ys_hbm.at[pl.ds(0, 2 * tc * ROW_TILE)], ybuf.at[slot_],
                                     sem.at[slot_])

    def issue(rows_ref, slot_):
        def body(g, c):
            t0 = g * DMA_BATCH
            src = [rows_ref[2 * t0 + u] for u in range(2 * DMA_BATCH)]
            for u in range(2 * DMA_BATCH):
                _row_copy(ys_hbm, src[u], ybuf.at[slot_], (u % 2) * tc + t0 + u // 2,
                          sem.at[slot_]).start()
            return c
        lax.fori_loop(0, tc // DMA_BATCH, body, 0)

    @pl.when(i == 0)
    def _():
        issue(cur_ref, 0)

    @pl.when(i + 1 < n_steps)
    def _():
        issue(nxt_ref, 1 - slot)

    gathered(slot).wait()
    y1a, y1b = _load_row_tiles(ybuf.at[slot], 0, tc)
    y2a, y2b = _load_row_tiles(ybuf.at[slot], tc, tc)
    rw = rw_ref[...]
    w1, w2 = rw[:, 0:1], rw[:, 1:2]
    moe = jnp.concatenate([w1 * y1a + w2 * y2a, w1 * y1b + w2 * y2b], axis=1)
    y_ref[...] = _layer_norm(base_ref[...] + moe, g2_ref[...], b2_ref[...])


def _combine_call(slot_flat, base, rw, g2, b2, ys, *, tc):
    n, d = base.shape
    n_steps = n // tc
    cur = pl.BlockSpec((2 * tc,), lambda i: (i,), memory_space=pltpu.SMEM)
    nxt = pl.BlockSpec((2 * tc,), lambda i: (jnp.minimum(i + 1, n_steps - 1),),
                       memory_space=pltpu.SMEM)
    tok = lambda w: pl.BlockSpec((tc, w), lambda i: (i, 0))
    full = lambda a: pl.BlockSpec(a.shape, lambda i: (0,) * a.ndim)
    return pl.pallas_call(
        functools.partial(_combine_kernel, tc=tc),
        grid=(n_steps,),
        in_specs=[cur, nxt, tok(d), tok(8), full(g2), full(b2), pl.BlockSpec(memory_space=pl.ANY)],
        out_specs=tok(d),
        out_shape=jax.ShapeDtypeStruct((n, d), F32),
        scratch_shapes=[pltpu.VMEM((2, 2 * tc * ROW_TILE, 128), U32),
                        pltpu.SemaphoreType.DMA((2,))],
        compiler_params=pltpu.CompilerParams(
            dimension_semantics=("arbitrary",), vmem_limit_bytes=VMEM_LIMIT),
        name="combine",
    )(slot_flat, slot_flat, base, rw, g2, b2, ys)


def _expert_segments(counts, tg, n_tiles):
    cnt = counts[0, N_GROUPS:N_GROUPS + N_EXPERTS].astype(jnp.int32)
    pc = ((cnt + tg - 1) // tg) * tg
    seg_end = jnp.cumsum(pc)
    seg = seg_end - pc
    nvalid = (seg_end[-1] // tg).reshape(1)
    tile_e = jnp.searchsorted(seg_end, jnp.arange(n_tiles, dtype=jnp.int32) * tg, side="right")
    tile_e = jnp.minimum(tile_e, N_EXPERTS - 1).astype(jnp.int32)
    return seg.astype(jnp.int32), pc.astype(jnp.int32), tile_e, nvalid.astype(jnp.int32)


def _cols(w, name):
    lo, hi = _OFF[name]
    return w[:, lo:hi]


def _rot_pairs(w):
    return jnp.concatenate([-w[..., 16:32], w[..., 0:16]], axis=-1)


def _layer_weights(w_in, w_uq, w_ukv, b_forget):
    wmain = jnp.concatenate([_cols(w_in, n) for n in ("ka", "cq", "ckv", "kc")], axis=1)
    kr = _cols(w_in, "kr")
    fc_pad = jnp.pad(_cols(w_in, "fc"), ((0, 0), (0, FC_ROWS - C_HEADS)))
    wt = jnp.concatenate([_cols(w_in, "qa"), _cols(w_in, "va"), _cols(w_in, "qc"), _cols(w_in, "vc"),
                          kr, _rot_pairs(kr), fc_pad], axis=1).T
    wq = w_uq.reshape(B_Q_RANK, B_HEADS, B_NOPE + B_ROPE)
    zpad = jnp.zeros((B_Q_RANK, B_HEADS, B_PAD - B_NOPE - B_ROPE), w_uq.dtype)
    plain = jnp.concatenate([wq, zpad], axis=-1).reshape(B_Q_RANK, B_HEADS * B_PAD)
    rot = jnp.concatenate([jnp.zeros_like(wq[..., :B_NOPE]), _rot_pairs(wq[..., B_NOPE:]), zpad],
                          axis=-1).reshape(B_Q_RANK, B_HEADS * B_PAD)
    wuq = jnp.concatenate([plain, rot], axis=1).T
    wkv = w_ukv.reshape(B_KV_RANK, B_HEADS, B_NOPE + B_VDIM)
    wukvk = wkv[..., :B_NOPE].reshape(B_KV_RANK, B_HEADS * B_NOPE).T
    wukvv = wkv[..., B_NOPE:].reshape(B_KV_RANK, B_HEADS * B_VDIM).T
    bfc = jnp.pad(b_forget, (0, FC_ROWS - C_HEADS)).reshape(FC_ROWS, 1)
    return (wmain.astype(BF16), wt.astype(BF16), wuq.astype(BF16), wukvk.astype(BF16),
            wukvv.astype(BF16), bfc.astype(F32))


def _rope_inputs(positions):
    half = B_ROPE // 2
    inv = ROPE_THETA ** (-jnp.arange(half, dtype=F32) / half)
    ang = positions.astype(F32)[..., None] * inv
    cos, sin = jnp.cos(ang), jnp.sin(ang)
    b, s, _ = cos.shape
    ones = jnp.ones((b, s, B_NOPE), F32)
    zeros = jnp.zeros((b, s, B_PAD - B_NOPE - B_ROPE), F32)
    cos128 = jnp.concatenate([ones, cos, cos, zeros], axis=-1)
    sin128 = jnp.concatenate([jnp.zeros_like(ones), sin, sin, zeros], axis=-1)
    cost = jnp.swapaxes(jnp.concatenate([cos, cos], axis=-1), 1, 2)
    sint = jnp.swapaxes(jnp.concatenate([sin, sin], axis=-1), 1, 2)
    return jnp.swapaxes(cos128, 1, 2), jnp.swapaxes(sin128, 1, 2), cost, sint


def kernel(x, p, positions, w_in, w_uq, w_ukv, g_cq, g_ckv, lam_q1, lam_k1, lam_q2, lam_k2, g_diff,
           b_forget, w_out, ln1_g, ln1_b, w_group, b_group, w_erouter, b_erouter, w_gate_e, w_up_e,
           w_down_e, w_ple_gate, w_ple_proj, ln2_g, ln2_b):
    b, s, d = x.shape
    n = b * s
    tq = 512
    tm_prep = 512
    tm_moe = 512
    tg = 512
    rows = 2 * n + N_EXPERTS * tg
    tb = 512
    tc = 512
    cos128, sin128, cost, sint = _rope_inputs(positions)
    posf = positions.astype(F32)
    pcol = posf[:, :, None]
    prow = posf[:, None, :]
    slopes = 2.0 ** (-8.0 * jnp.arange(1, A_HEADS + 1, dtype=F32) / A_HEADS)
    slopes = jnp.broadcast_to(
        jnp.pad(slopes.reshape(3, 2), ((0, 0), (0, 6)))[:, :, None], (3, 8, 128))
    row = lambda v: v.reshape(1, -1).astype(F32)

    for i in range(DEPTH):
        wmain, wt, wuq, wukvk, wukvv, bfc = _layer_weights(w_in[i], w_uq[i], w_ukv[i], b_forget[i])
        ka, kb, kc, qta, qtb, qtc, vta, vtb, vtc, cumt = _prep_call(
            x, wmain, wt, wuq, wukvk, wukvv, row(g_cq[i]), row(g_ckv[i]), bfc,
            cos128, sin128, cost, sint, tm=tm_prep)
        lamv = jnp.pad(jnp.stack([lam_q1[i], lam_k1[i], lam_q2[i], lam_k2[i]]).astype(F32),
                       ((0, 4), (0, 128 - A_DIM)))
        gcol = jnp.tile(g_diff[i].astype(F32), 2).reshape(128, 1)
        lam_init = 0.8 - 0.6 * math.exp(-0.3 * i)
        oa = _attn_a_call(qta, ka, vta, pcol, prow, slopes, lamv, gcol, tq=tq, lam_init=lam_init,
                          pairs=1)
        ob = _attn_b_call(qtb, kb, vtb, tq=tq, pairs=3)
        crow = cumt[:, :C_HEADS].reshape(b, 2, 2, s)
        ccol = jnp.swapaxes(crow, 2, 3)
        oc = _attn_c_call(qtc, kc, vtc, ccol, crow, tq=tq)

        wo = w_out[i].astype(BF16)
        wr = jnp.concatenate([w_group[i], jnp.moveaxis(w_erouter[i], 0, 1).reshape(d, N_EXPERTS)],
                             axis=1)
        wr = jnp.pad(wr, ((0, 0), (0, ROUTER_COLS - wr.shape[1]))).astype(F32)
        wr1 = wr.astype(BF16)
        wr2 = (wr - wr1.astype(F32)).astype(BF16)
        rbias = jnp.pad(jnp.concatenate([b_group[i], b_erouter[i].reshape(-1)]),
                        (0, ROUTER_COLS - N_GROUPS - N_EXPERTS)).reshape(1, ROUTER_COLS).astype(F32)
        h, base, ri, rw, counts = _post_call(
            x.reshape(n, d), oa.reshape(n, MIX_A), ob.reshape(n, MIX_B), oc.reshape(n, MIX_C),
            p[i].reshape(n, PLE_DIM), wo[:MIX_A], wo[MIX_A:MIX_A + MIX_B], wo[MIX_A + MIX_B:],
            row(ln1_g[i]), row(ln1_b[i]), wr1, wr2, rbias, w_ple_gate[i].astype(BF16),
            w_ple_proj[i].astype(BF16), tm=tm_moe)
        seg, pc, tile_e, nvalid = _expert_segments(counts, tg, rows // tg)
        seg_row = jnp.pad(seg.astype(F32), (0, ROUTER_COLS - N_EXPERTS)).reshape(1, ROUTER_COLS)
        slot_flat = _slot_call(ri, seg_row, tm=tm_moe)[:, 0:2].reshape(-1)
        xs = _dispatch_call(seg, pc, nvalid, slot_flat, h, rows=rows, tb=tb, tg=tg)
        ys = _experts_call(tile_e, nvalid, xs, w_gate_e, w_up_e, w_down_e, tg=tg, layer=i)
        y = _combine_call(slot_flat, base, rw, row(ln2_g[i]), row(ln2_b[i]), ys, tc=tc)
        x = y.reshape(b, s, d)
    return x
```

```python
import functools
import math

import jax
import jax.numpy as jnp
from jax import lax
from jax.experimental import pallas as pl
from jax.experimental.pallas import tpu as pltpu

F32 = jnp.float32
BF16 = jnp.bfloat16

D_MODEL = 1024
DEPTH = 2
CHUNK = 64
PLE_DIM = 256

A_HEADS = 6
A_DIM = 32
A_VDIM = 64
B_HEADS = 6
B_Q_RANK = 256
B_KV_RANK = 128
B_NOPE = 64
B_ROPE = 32
B_VDIM = 64
ROPE_THETA = 10000.0
C_HEADS = 4
C_DIM = 64

MIX_A = A_HEADS * A_VDIM
MIX_B = B_HEADS * B_VDIM
MIX_C = C_HEADS * C_DIM
B_PAD = 128

N_GROUPS = 4
EXPERTS_PER_GROUP = 8
N_EXPERTS = N_GROUPS * EXPERTS_PER_GROUP
D_EXPERT = 256
ROUTER_COLS = 128

DEEPNORM_ALPHA = (2 * DEPTH) ** 0.25
LN_EPS = 1e-5
RMS_EPS = 1e-6
NEG_BIG = -1e30
LOG2E = math.log2(math.e)

_OFF = {}
_o = 0
for _name, _w in (("qa", 384), ("ka", 384), ("va", 384), ("cq", 256), ("ckv", 128), ("kr", 32),
                  ("qc", 256), ("kc", 256), ("vc", 256), ("fc", 4)):
    _OFF[_name] = (_o, _o + _w)
    _o += _w

T_QA, T_VA, T_QC, T_VC, T_KR, T_KRS, T_FC, T_ROWS = 0, 384, 768, 1024, 1280, 1312, 1344, 1360
FC_ROWS = T_ROWS - T_FC

VMEM_LIMIT = 48 * 1024 * 1024
DMA_BATCH = 4

NT_DIMS = (((1,), (1,)), ((), ()))


def _dot(a, b):
    return jnp.dot(a, b, preferred_element_type=F32)


def _dot_nt(a, b):
    return lax.dot_general(a, b, NT_DIMS, preferred_element_type=F32)


U32 = jnp.uint32
HALF = D_MODEL // 2
ROW_TILE = HALF // 128


def _pack_rows(v):
    hi = lax.bitcast_convert_type(v[:, :HALF].astype(BF16).astype(F32), U32)
    lo = lax.bitcast_convert_type(v[:, HALF:].astype(BF16).astype(F32), U32)
    return hi | lax.shift_right_logical(lo, jnp.full_like(lo, 16))


def _unpack_rows(w):
    first = lax.bitcast_convert_type(w & jnp.full_like(w, 0xFFFF0000), F32)
    second = lax.bitcast_convert_type(lax.shift_left(w, jnp.full_like(w, 16)), F32)
    return first, second


def _store_row_tiles(ref, v):
    rows = v.shape[0]
    w = _pack_rows(v)
    for c in range(ROW_TILE):
        ref[pl.ds(c, rows, stride=ROW_TILE), :] = w[:, c * 128:(c + 1) * 128]


def _load_row_tiles(ref, start, rows):
    w = jnp.concatenate(
        [ref[pl.ds(start * ROW_TILE + c, rows, stride=ROW_TILE), :] for c in range(ROW_TILE)],
        axis=1)
    return _unpack_rows(w)


def _split3(a):
    a1 = a.astype(BF16)
    r1 = a - a1.astype(F32)
    a2 = r1.astype(BF16)
    a3 = (r1 - a2.astype(F32)).astype(BF16)
    return a1, a2, a3


def _prep_kernel(x_ref, wmain_ref, wt_ref, wuq_ref, wukvk_ref, wukvv_ref, gcq_ref, gckv_ref,
                 bf_ref, cos_ref, sin_ref, cost_ref, sint_ref,
                 ka_ref, kb_ref, kc_ref, qta_ref, qtb_ref, qtc_ref, vta_ref, vtb_ref, vtc_ref,
                 cum_ref, carry_ref, *, tm):
    j = pl.program_id(1)

    @pl.when(j == 0)
    def _():
        carry_ref[...] = jnp.zeros_like(carry_ref)

    xb = x_ref[0].astype(BF16)
    z = _dot(xb, wmain_ref[...])
    zt = _dot_nt(wt_ref[...], xb)

    ka_ref[0] = z[:, 0:384].astype(BF16)
    qta_ref[0] = (zt[T_QA:T_QA + 384] * (LOG2E / math.sqrt(A_DIM))).astype(BF16)
    vta_ref[0] = zt[T_VA:T_VA + 384].astype(BF16)
    kc_ref[0] = z[:, 768:1024].astype(BF16)
    qtc_ref[0] = (zt[T_QC:T_QC + 256] * (LOG2E / math.sqrt(C_DIM))).astype(BF16)
    vtc_ref[0] = zt[T_VC:T_VC + 256].astype(BF16)

    fct = zt[T_FC:T_ROWS] + bf_ref[...]
    logf = jnp.minimum(fct, 0.0) - jnp.log(1.0 + jnp.exp(-jnp.abs(fct)))
    r_i = lax.broadcasted_iota(jnp.int32, (tm, tm), 0)
    c_i = lax.broadcasted_iota(jnp.int32, (tm, tm), 1)
    tri = jnp.where(r_i <= c_i, 1.0, 0.0).astype(BF16)
    l1, l2, l3 = _split3(logf)
    cum = _dot(l1, tri) + _dot(l2, tri) + _dot(l3, tri) + carry_ref[:, 0:1]
    cum_ref[0] = cum * LOG2E
    carry_ref[...] = jnp.broadcast_to(cum[:, tm - 1:tm], carry_ref.shape)

    cq = z[:, 384:640]
    cq_n = cq * lax.rsqrt(jnp.mean(cq * cq, axis=-1, keepdims=True) + RMS_EPS) * gcq_ref[...]
    zqt = _dot_nt(wuq_ref[...], cq_n.astype(BF16))
    cosb = cos_ref[0]
    sinb = sin_ref[0]
    scale_b = LOG2E / math.sqrt(B_NOPE + B_ROPE)
    for h in range(B_HEADS):
        lo = h * B_PAD
        qh = zqt[lo:lo + B_PAD] * cosb + zqt[768 + lo:768 + lo + B_PAD] * sinb
        qtb_ref[0, lo:lo + B_PAD, :] = (qh * scale_b).astype(BF16)

    ckv = z[:, 640:768]
    ckv_n = ckv * lax.rsqrt(jnp.mean(ckv * ckv, axis=-1, keepdims=True) + RMS_EPS) * gckv_ref[...]
    ckv_b = ckv_n.astype(BF16)
    vtb_ref[0] = _dot_nt(wukvv_ref[...], ckv_b).astype(BF16)
    knt = _dot_nt(wukvk_ref[...], ckv_b)
    krt = zt[T_KR:T_KR + 32] * cost_ref[0] + zt[T_KRS:T_KRS + 32] * sint_ref[0]
    zpad = jnp.zeros((B_PAD - B_NOPE - B_ROPE, tm), F32)
    pieces = []
    for h in range(B_HEADS):
        pieces += [knt[h * B_NOPE:(h + 1) * B_NOPE], krt, zpad]
    kb_ref[0] = jnp.concatenate(pieces, axis=0).T.astype(BF16)


def _prep_call(x, wmain, wt, wuq, wukvk, wukvv, gcq, gckv, bfc, cos128, sin128, cost, sint, *, tm):
    b, s, d = x.shape
    grid = (b, s // tm)
    full = lambda shape: pl.BlockSpec(shape, lambda bi, j: (0,) * len(shape))
    tok = lambda w: pl.BlockSpec((1, tm, w), lambda bi, j: (bi, j, 0))
    tr = lambda r: pl.BlockSpec((1, r, tm), lambda bi, j: (bi, 0, j))
    out_shape = (
        jax.ShapeDtypeStruct((b, s, 384), BF16),
        jax.ShapeDtypeStruct((b, s, 768), BF16),
        jax.ShapeDtypeStruct((b, s, 256), BF16),
        jax.ShapeDtypeStruct((b, 384, s), BF16),
        jax.ShapeDtypeStruct((b, 768, s), BF16),
        jax.ShapeDtypeStruct((b, 256, s), BF16),
        jax.ShapeDtypeStruct((b, 384, s), BF16),
        jax.ShapeDtypeStruct((b, 384, s), BF16),
        jax.ShapeDtypeStruct((b, 256, s), BF16),
        jax.ShapeDtypeStruct((b, FC_ROWS, s), F32),
    )
    out_specs = (tok(384), tok(768), tok(256), tr(384), tr(768), tr(256), tr(384), tr(384),
                 tr(256), tr(FC_ROWS))
    in_specs = [tok(d), full(wmain.shape), full(wt.shape), full(wuq.shape), full(wukvk.shape),
                full(wukvv.shape), full(gcq.shape), full(gckv.shape), full(bfc.shape),
                tr(128), tr(128), tr(32), tr(32)]
    return pl.pallas_call(
        functools.partial(_prep_kernel, tm=tm),
        grid=grid, in_specs=in_specs, out_specs=out_specs, out_shape=out_shape,
        scratch_shapes=[pltpu.VMEM((FC_ROWS, 128), F32)],
        compiler_params=pltpu.CompilerParams(
            dimension_semantics=("arbitrary", "arbitrary"), vmem_limit_bytes=VMEM_LIMIT),
        name="prep",
    )(x, wmain, wt, wuq, wukvk, wukvv, gcq, gckv, bfc, cos128, sin128, cost, sint)


def _flash(i, tq, n_maps, scores_fn, mask):
    def update(state, j, diag):
        new = []
        for (m, l, acc), (s, v) in zip(state, scores_fn(j)):
            if diag:
                s = jnp.where(mask, s, NEG_BIG)
            m_new = jnp.maximum(m, jnp.max(s, axis=1, keepdims=True))
            alpha = jnp.exp2(m - m_new)
            p = jnp.exp2(s - m_new)
            l_new = alpha * l + jnp.sum(p, axis=1, keepdims=True)
            acc_new = alpha * acc + _dot(p.astype(BF16), v)
            new.append((m_new, l_new, acc_new))
        return tuple(new)

    init = tuple((jnp.full((tq, 1), NEG_BIG, F32), jnp.zeros((tq, 1), F32),
                  jnp.zeros((tq, 128), F32)) for _ in range(n_maps))
    state = update(init, i, True)
    state = lax.fori_loop(0, i, lambda j, st: update(st, j, False), state)
    return [acc * (1.0 / l) for (_, l, acc) in state]


def _chunk_mask(tq):
    r = lax.broadcasted_iota(jnp.int32, (tq, tq), 0)
    c = lax.broadcasted_iota(jnp.int32, (tq, tq), 1)
    return (c // CHUNK) <= (r // CHUNK)


def _causal_mask(tq):
    r = lax.broadcasted_iota(jnp.int32, (tq, tq), 0)
    c = lax.broadcasted_iota(jnp.int32, (tq, tq), 1)
    return c <= r


ONES_ROWS = 16


def _flash_t(i, tq, n_maps, scores_fn, mask):
    ones = jnp.ones((ONES_ROWS, tq), BF16)

    def update(state, j, diag):
        new = []
        for (m, acc), (s, vt) in zip(state, scores_fn(j)):
            if diag:
                s = jnp.where(mask, s, NEG_BIG)
            m_new = jnp.maximum(m, jnp.max(s, axis=0, keepdims=True))
            alpha = jnp.exp2(m - m_new)
            p = jnp.exp2(s - m_new).astype(BF16)
            vext = jnp.concatenate([vt, ones], axis=0)
            new.append((m_new, alpha * acc + _dot(vext, p)))
        return tuple(new)

    init = tuple((jnp.full((1, tq), NEG_BIG, F32), jnp.zeros((128 + ONES_ROWS, tq), F32))
                 for _ in range(n_maps))
    state = update(init, i, True)
    state = lax.fori_loop(0, i, lambda j, st: update(st, j, False), state)
    return [acc[0:128] * (1.0 / acc[128:129]) for (_, acc) in state]


def _chunk_mask_t(tq):
    r = lax.broadcasted_iota(jnp.int32, (tq, tq), 0)
    c = lax.broadcasted_iota(jnp.int32, (tq, tq), 1)
    return (r // CHUNK) <= (c // CHUNK)


def _causal_mask_t(tq):
    r = lax.broadcasted_iota(jnp.int32, (tq, tq), 0)
    c = lax.broadcasted_iota(jnp.int32, (tq, tq), 1)
    return r <= c


def _attn_a_kernel(qt_ref, k_ref, vt_ref, pcol_ref, prow_ref, slope_ref, lam_ref, g_ref, o_ref,
                   *, tq, lam_init, pairs):
    i = pl.program_id(2)
    row = lax.broadcasted_iota(jnp.int32, (128, tq), 0)
    pt = prow_ref[0]
    mask = _chunk_mask_t(tq)
    lam_v = lam_ref[...]
    lam = (jnp.exp(jnp.sum(lam_v[0:1] * lam_v[1:2], axis=1, keepdims=True))
           - jnp.exp(jnp.sum(lam_v[2:3] * lam_v[3:4], axis=1, keepdims=True)) + lam_init)

    def koff(j):
        return pl.multiple_of(j * tq, tq)

    qms, nslopes = [], []
    for pr in range(pairs):
        qt = qt_ref[0, pr * 128:(pr + 1) * 128, :]
        qms += [jnp.where((row >= 32 * mi) & (row < 32 * mi + 32), qt, jnp.zeros_like(qt))
                for mi in range(4)]
        nslopes += [-LOG2E * slope_ref[pr, hh:hh + 1, 0:1] for hh in range(2)]

    def scores_fn(j):
        ps = pcol_ref[0, pl.ds(koff(j), tq), :]
        dist = jnp.abs(ps - pt)
        out = []
        for pr in range(pairs):
            k = k_ref[0, pl.ds(koff(j), tq), pr * 128:(pr + 1) * 128]
            vt = vt_ref[0, pr * 128:(pr + 1) * 128, pl.ds(koff(j), tq)]
            for hh in range(2):
                bias = nslopes[2 * pr + hh] * dist
                out += [(_dot(k, qms[4 * pr + 2 * hh + mm]) + bias, vt) for mm in range(2)]
        return out

    o = _flash_t(i, tq, 4 * pairs, scores_fn, mask)
    for pr in range(pairs):
        o0, o1, o2, o3 = o[4 * pr:4 * pr + 4]
        ot = jnp.where(row < 64, o0 - lam * o1, o2 - lam * o3)
        sq = ot * ot
        ms0 = jnp.sum(sq[0:64], axis=0, keepdims=True) * (1.0 / A_VDIM)
        ms1 = jnp.sum(sq[64:128], axis=0, keepdims=True) * (1.0 / A_VDIM)
        ms = jnp.where(row < 64, ms0, ms1)
        yt = ot * lax.rsqrt(ms + RMS_EPS) * g_ref[...] * (1.0 - lam_init)
        o_ref[0, :, pr * 128:(pr + 1) * 128] = yt.T.astype(BF16)


def _attn_b_kernel(qt_ref, k_ref, vt_ref, o_ref, *, tq, pairs):
    i = pl.program_id(2)
    row = lax.broadcasted_iota(jnp.int32, (128, tq), 0)
    mask = _chunk_mask_t(tq)
    heads = 2 * pairs

    def koff(j):
        return pl.multiple_of(j * tq, tq)

    qts = [qt_ref[0, h * B_PAD:(h + 1) * B_PAD, :] for h in range(heads)]

    def scores_fn(j):
        vts = [vt_ref[0, pr * 128:(pr + 1) * 128, pl.ds(koff(j), tq)] for pr in range(pairs)]
        return [(_dot(k_ref[0, pl.ds(koff(j), tq), h * B_PAD:(h + 1) * B_PAD], qts[h]), vts[h // 2])
                for h in range(heads)]

    outs = _flash_t(i, tq, heads, scores_fn, mask)
    for pr in range(pairs):
        o_ref[0, :, pr * 128:(pr + 1) * 128] = jnp.where(
            row < 64, outs[2 * pr], outs[2 * pr + 1]).T.astype(BF16)


def _attn_c_kernel(qt_ref, k_ref, vt_ref, ccol_ref, crow_ref, o_ref, *, tq):
    i = pl.program_id(2)
    row = lax.broadcasted_iota(jnp.int32, (128, tq), 0)
    mask = _causal_mask_t(tq)
    pairs = C_HEADS // 2

    def koff(j):
        return pl.multiple_of(j * tq, tq)

    qms = []
    for pr in range(pairs):
        qt = qt_ref[0, pr * 128:(pr + 1) * 128, :]
        qms += [jnp.where((row >= 64 * hh) & (row < 64 * hh + 64), qt, jnp.zeros_like(qt))
                for hh in range(2)]
    cts = [crow_ref[0, h // 2, h % 2:h % 2 + 1, :] for h in range(C_HEADS)]

    def scores_fn(j):
        out = []
        for pr in range(pairs):
            k = k_ref[0, pl.ds(koff(j), tq), pr * 128:(pr + 1) * 128]
            vt = vt_ref[0, pr * 128:(pr + 1) * 128, pl.ds(koff(j), tq)]
            for hh in range(2):
                cs = ccol_ref[0, pr, pl.ds(koff(j), tq), hh:hh + 1]
                out.append((_dot(k, qms[2 * pr + hh]) + (cts[2 * pr + hh] - cs), vt))
        return out

    outs = _flash_t(i, tq, C_HEADS, scores_fn, mask)
    for pr in range(pairs):
        o_ref[0, :, pr * 128:(pr + 1) * 128] = jnp.where(
            row < 64, outs[2 * pr], outs[2 * pr + 1]).T.astype(BF16)


def _attn_params():
    return pltpu.CompilerParams(
        dimension_semantics=("arbitrary", "arbitrary", "arbitrary"), vmem_limit_bytes=VMEM_LIMIT)


def _attn_a_call(qta, ka, vta, pcol, prow, slopes, lamv, gcol, *, tq, lam_init, pairs):
    b, s, _ = ka.shape
    grid = (b, A_HEADS // (2 * pairs), s // tq)
    w = pairs * 128
    return pl.pallas_call(
        functools.partial(_attn_a_kernel, tq=tq, lam_init=lam_init, pairs=pairs),
        grid=grid,
        in_specs=[
            pl.BlockSpec((1, w, tq), lambda bi, c, i: (bi, c, i)),
            pl.BlockSpec((1, s, w), lambda bi, c, i: (bi, 0, c)),
            pl.BlockSpec((1, w, s), lambda bi, c, i: (bi, c, 0)),
            pl.BlockSpec((1, s, 1), lambda bi, c, i: (bi, 0, 0)),
            pl.BlockSpec((1, 1, tq), lambda bi, c, i: (bi, 0, i)),
            pl.BlockSpec((pairs, 8, 128), lambda bi, c, i: (c, 0, 0)),
            pl.BlockSpec((8, 128), lambda bi, c, i: (0, 0)),
            pl.BlockSpec((128, 1), lambda bi, c, i: (0, 0)),
        ],
        out_specs=pl.BlockSpec((1, tq, w), lambda bi, c, i: (bi, i, c)),
        out_shape=jax.ShapeDtypeStruct((b, s, MIX_A), BF16),
        compiler_params=_attn_params(), name="attn_a",
    )(qta, ka, vta, pcol, prow, slopes, lamv, gcol)


def _attn_b_call(qtb, kb, vtb, *, tq, pairs):
    b, s, _ = kb.shape
    grid = (b, B_HEADS // (2 * pairs), s // tq)
    return pl.pallas_call(
        functools.partial(_attn_b_kernel, tq=tq, pairs=pairs),
        grid=grid,
        in_specs=[
            pl.BlockSpec((1, 2 * pairs * B_PAD, tq), lambda bi, c, i: (bi, c, i)),
            pl.BlockSpec((1, s, 2 * pairs * B_PAD), lambda bi, c, i: (bi, 0, c)),
            pl.BlockSpec((1, pairs * 128, s), lambda bi, c, i: (bi, c, 0)),
        ],
        out_specs=pl.BlockSpec((1, tq, pairs * 128), lambda bi, c, i: (bi, i, c)),
        out_shape=jax.ShapeDtypeStruct((b, s, MIX_B), BF16),
        compiler_params=_attn_params(), name="attn_b",
    )(qtb, kb, vtb)


def _attn_c_call(qtc, kc, vtc, ccol, crow, *, tq):
    b, s, _ = kc.shape
    grid = (b, 1, s // tq)
    return pl.pallas_call(
        functools.partial(_attn_c_kernel, tq=tq),
        grid=grid,
        in_specs=[
            pl.BlockSpec((1, MIX_C, tq), lambda bi, c, i: (bi, 0, i)),
            pl.BlockSpec((1, s, MIX_C), lambda bi, c, i: (bi, 0, 0)),
            pl.BlockSpec((1, MIX_C, s), lambda bi, c, i: (bi, 0, 0)),
            pl.BlockSpec((1, 2, s, 2), lambda bi, c, i: (bi, 0, 0, 0)),
            pl.BlockSpec((1, 2, 2, tq), lambda bi, c, i: (bi, 0, 0, i)),
        ],
        out_specs=pl.BlockSpec((1, tq, MIX_C), lambda bi, c, i: (bi, i, 0)),
        out_shape=jax.ShapeDtypeStruct((b, s, MIX_C), BF16),
        compiler_params=_attn_params(), name="attn_c",
    )(qtc, kc, vtc, ccol, crow)


def _layer_norm(v, g, b):
    mu = jnp.mean(v, axis=-1, keepdims=True)
    c = v - mu
    var = jnp.mean(c * c, axis=-1, keepdims=True)
    return c * lax.rsqrt(var + LN_EPS) * g + b


def _route(h, wr1_ref, wr2_ref, rbias_ref):
    tm = h.shape[0]
    h1 = h.astype(BF16)
    h2 = (h - h1.astype(F32)).astype(BF16)
    r12 = _dot(h1, wr2_ref[...])
    logits = (r12[:, :ROUTER_COLS] + r12[:, ROUTER_COLS:] + _dot(h2, wr1_ref[...])
              + rbias_ref[...])
    col = lax.broadcasted_iota(jnp.int32, (tm, ROUTER_COLS), 1)
    colf = col.astype(F32)
    big = float(ROUTER_COLS)
    gmask = col < N_GROUPS
    gmax = jnp.max(jnp.where(gmask, logits, NEG_BIG), axis=1, keepdims=True)
    g_idx = jnp.min(jnp.where(gmask & (logits == gmax), colf, big), axis=1, keepdims=True)
    g_w = 1.0 / jnp.sum(jnp.where(gmask, jnp.exp(logits - gmax), 0.0), axis=1, keepdims=True)
    lo = N_GROUPS + g_idx * EXPERTS_PER_GROUP
    emask = (colf >= lo) & (colf < lo + EXPERTS_PER_GROUP)
    top1 = jnp.max(jnp.where(emask, logits, NEG_BIG), axis=1, keepdims=True)
    idx1 = jnp.min(jnp.where(emask & (logits == top1), colf, big), axis=1, keepdims=True)
    emask2 = emask & (colf != idx1)
    top2 = jnp.max(jnp.where(emask2, logits, NEG_BIG), axis=1, keepdims=True)
    idx2 = jnp.min(jnp.where(emask2 & (logits == top2), colf, big), axis=1, keepdims=True)
    e2 = jnp.exp(top2 - top1)
    w_1 = g_w / (1.0 + e2)
    w_2 = g_w * e2 / (1.0 + e2)
    return colf, idx1, idx2, w_1, w_2


def _post_kernel(x_ref, oa_ref, ob_ref, oc_ref, p_ref, woa_ref, wob_ref, woc_ref, g1_ref, b1_ref,
                 wr1_ref, wr2_ref, rbias_ref, wpg_ref, wpp_ref,
                 h_ref, base_ref, ri_ref, rw_ref, cnt_ref, carry_ref):
    i = pl.program_id(0)
    tm = x_ref.shape[0]

    @pl.when(i == 0)
    def _():
        carry_ref[...] = jnp.zeros_like(carry_ref)

    mix = (_dot(oa_ref[...], woa_ref[...]) + _dot(ob_ref[...], wob_ref[...])
           + _dot(oc_ref[...], woc_ref[...]))
    h = _layer_norm(DEEPNORM_ALPHA * x_ref[...] + mix, g1_ref[...], b1_ref[...])
    _store_row_tiles(h_ref, h)
    gl = _dot(h.astype(BF16), wpg_ref[...])
    ple = (1.0 / (1.0 + jnp.exp(-gl))) * _dot(p_ref[...].astype(BF16), wpp_ref[...])
    base_ref[...] = DEEPNORM_ALPHA * h + ple

    colf, idx1, idx2, w_1, w_2 = _route(h, wr1_ref, wr2_ref, rbias_ref)
    hit1 = colf == idx1
    hit2 = colf == idx2
    onehot = jnp.where(hit1 | hit2, 1.0, 0.0)
    r_i = lax.broadcasted_iota(jnp.int32, (tm, tm), 0)
    c_i = lax.broadcasted_iota(jnp.int32, (tm, tm), 1)
    before = jnp.where(c_i < r_i, 1.0, 0.0).astype(BF16)
    seen = _dot(before, onehot.astype(BF16)) + carry_ref[0:1, :]
    rank1 = jnp.sum(jnp.where(hit1, seen, 0.0), axis=1, keepdims=True)
    rank2 = jnp.sum(jnp.where(hit2, seen, 0.0), axis=1, keepdims=True)
    total = carry_ref[0:1, :] + jnp.sum(onehot, axis=0, keepdims=True)
    carry_ref[...] = jnp.broadcast_to(total, carry_ref.shape)
    cnt_ref[...] = jnp.broadcast_to(total, cnt_ref.shape)

    lane8 = lax.broadcasted_iota(jnp.int32, (tm, 8), 1)
    ri = jnp.where(lane8 == 0, idx1 - N_GROUPS,
                   jnp.where(lane8 == 1, idx2 - N_GROUPS,
                             jnp.where(lane8 == 2, rank1, jnp.where(lane8 == 3, rank2, 0.0))))
    ri_ref[...] = ri.astype(jnp.int32)
    rw_ref[...] = jnp.where(lane8 == 0, w_1, jnp.where(lane8 == 1, w_2, 0.0))


def _post_call(x2, oa, ob, oc, p2, woa, wob, woc, g1, b1, wr1, wr2, rbias, wpg, wpp, *, tm):
    n, d = x2.shape
    tok = lambda w: pl.BlockSpec((tm, w), lambda t: (t, 0))
    full = lambda a: pl.BlockSpec(a.shape, lambda t: (0,) * a.ndim)
    return pl.pallas_call(
        _post_kernel,
        grid=(n // tm,),
        in_specs=[tok(d), tok(MIX_A), tok(MIX_B), tok(MIX_C), tok(PLE_DIM), full(woa), full(wob),
                  full(woc), full(g1), full(b1), full(wr1), full(wr2), full(rbias), full(wpg),
                  full(wpp)],
        out_specs=(pl.BlockSpec((tm * ROW_TILE, 128), lambda t: (t, 0)), tok(d), tok(8), tok(8),
                   pl.BlockSpec((8, ROUTER_COLS), lambda t: (0, 0))),
        out_shape=(jax.ShapeDtypeStruct((n * ROW_TILE, 128), U32), jax.ShapeDtypeStruct((n, d), F32),
                   jax.ShapeDtypeStruct((n, 8), jnp.int32), jax.ShapeDtypeStruct((n, 8), F32),
                   jax.ShapeDtypeStruct((8, ROUTER_COLS), F32)),
        scratch_shapes=[pltpu.VMEM((8, ROUTER_COLS), F32)],
        compiler_params=pltpu.CompilerParams(
            dimension_semantics=("arbitrary",), vmem_limit_bytes=VMEM_LIMIT),
        name="post",
    )(x2, oa, ob, oc, p2, woa, wob, woc, g1, b1, wr1, wr2, rbias, wpg, wpp)


def _row_copy(src_ref, src_row, dst_ref, dst_row, sem):
    src = pl.ds(pl.multiple_of(src_row * ROW_TILE, ROW_TILE), ROW_TILE)
    dst = pl.ds(pl.multiple_of(dst_row * ROW_TILE, ROW_TILE), ROW_TILE)
    return pltpu.make_async_copy(src_ref.at[src], dst_ref.at[dst], sem)


def _slot_kernel(ri_ref, seg_ref, slot_ref):
    tm = ri_ref.shape[0]
    ri = ri_ref[...]
    colf = lax.broadcasted_iota(jnp.int32, (tm, ROUTER_COLS), 1).astype(F32)
    seg = seg_ref[...]
    lane8 = lax.broadcasted_iota(jnp.int32, (tm, 8), 1)
    out = jnp.zeros((tm, 8), F32)
    for k in range(2):
        e = ri[:, k:k + 1].astype(F32)
        start = jnp.sum(jnp.where(colf == e, seg, 0.0), axis=1, keepdims=True)
        out = jnp.where(lane8 == k, start + ri[:, 2 + k:3 + k].astype(F32), out)
    slot_ref[...] = out.astype(jnp.int32)


def _slot_call(ri, seg_row, *, tm):
    n = ri.shape[0]
    return pl.pallas_call(
        _slot_kernel,
        grid=(n // tm,),
        in_specs=[pl.BlockSpec((tm, 8), lambda t: (t, 0)),
                  pl.BlockSpec((1, ROUTER_COLS), lambda t: (0, 0))],
        out_specs=pl.BlockSpec((tm, 8), lambda t: (t, 0)),
        out_shape=jax.ShapeDtypeStruct((n, 8), jnp.int32),
        compiler_params=pltpu.CompilerParams(dimension_semantics=("arbitrary",)),
        name="slots",
    )(ri, seg_row)


def _dispatch_kernel(seg_ref, pc_ref, nv_ref, slot_ref, h_ref, xs_hbm, zbuf, sem, zsem, *, tb, tg):
    i = pl.program_id(0)
    n_tiles = xs_hbm.shape[0] // (tg * ROW_TILE)

    def unused_tile_zero_copy(jt):
        start = pl.multiple_of(jt * (tg * ROW_TILE), tg * ROW_TILE)
        return pltpu.make_async_copy(zbuf, xs_hbm.at[pl.ds(start, tg * ROW_TILE)], zsem)

    def for_unused_tiles(fn):
        def body(jt, c):
            fn(unused_tile_zero_copy(jt))
            return c
        lax.fori_loop(nv_ref[0], n_tiles, body, 0)

    def tail_zero_copy(e):
        start = pl.multiple_of((seg_ref[e] + pc_ref[e] - tg) * ROW_TILE, tg * ROW_TILE)
        return pltpu.make_async_copy(zbuf, xs_hbm.at[pl.ds(start, tg * ROW_TILE)], zsem)

    @pl.when(i == 0)
    def _():
        zbuf[...] = jnp.zeros_like(zbuf)
        for e in range(N_EXPERTS):
            @pl.when(pc_ref[e] > 0)
            def _():
                tail_zero_copy(e).start()
        for e in range(N_EXPERTS):
            @pl.when(pc_ref[e] > 0)
            def _():
                tail_zero_copy(e).wait()
        for_unused_tiles(lambda cp: cp.start())
        for_unused_tiles(lambda cp: cp.wait())

    def body(g, c):
        t0 = g * DMA_BATCH
        dst = [slot_ref[2 * t0 + u] for u in range(2 * DMA_BATCH)]
        for u in range(2 * DMA_BATCH):
            _row_copy(h_ref, t0 + u // 2, xs_hbm, dst[u], sem).start()
        return c

    lax.fori_loop(0, tb // DMA_BATCH, body, 0)
    for _ in range(2):
        pltpu.make_async_copy(h_ref, xs_hbm.at[pl.ds(0, tb * ROW_TILE)], sem).wait()


def _dispatch_call(seg, pc, nvalid, slot_flat, h, *, rows, tb, tg):
    n = h.shape[0] // ROW_TILE
    smem_blk = pl.BlockSpec((2 * tb,), lambda i, seg, pc, nv: (i,), memory_space=pltpu.SMEM)
    grid_spec = pltpu.PrefetchScalarGridSpec(
        num_scalar_prefetch=3, grid=(n // tb,),
        in_specs=[smem_blk,
                  pl.BlockSpec((tb * ROW_TILE, 128), lambda i, seg, pc, nv: (i, 0))],
        out_specs=pl.BlockSpec(memory_space=pl.ANY),
        scratch_shapes=[pltpu.VMEM((tg * ROW_TILE, 128), U32), pltpu.SemaphoreType.DMA(()),
                        pltpu.SemaphoreType.DMA(())])
    return pl.pallas_call(
        functools.partial(_dispatch_kernel, tb=tb, tg=tg),
        grid_spec=grid_spec,
        out_shape=jax.ShapeDtypeStruct((rows * ROW_TILE, 128), U32),
        compiler_params=pltpu.CompilerParams(
            dimension_semantics=("arbitrary",), vmem_limit_bytes=VMEM_LIMIT),
        name="dispatch",
    )(seg, pc, nvalid, slot_flat, h)


def _experts_kernel(te_ref, nv_ref, xs_ref, wg_ref, wu_ref, wd_ref, ys_ref, wgb_ref, wub_ref,
                    wdb_ref):
    j = pl.program_id(0)

    @pl.when(j < nv_ref[0])
    def _():
        prev = te_ref[jnp.maximum(j - 1, 0)]

        @pl.when((j == 0) | (te_ref[j] != prev))
        def _():
            wgb_ref[...] = wg_ref[0, 0].astype(BF16)
            wub_ref[...] = wu_ref[0, 0].astype(BF16)
            wdb_ref[...] = wd_ref[0, 0].astype(BF16)

        tg = xs_ref.shape[0] // ROW_TILE
        x1, x2 = (t.astype(BF16) for t in _load_row_tiles(xs_ref, 0, tg))
        gate = _dot(x1, wgb_ref[0:HALF, :]) + _dot(x2, wgb_ref[HALF:D_MODEL, :])
        up = _dot(x1, wub_ref[0:HALF, :]) + _dot(x2, wub_ref[HALF:D_MODEL, :])
        act = gate * (1.0 / (1.0 + jnp.exp(-gate))) * up
        _store_row_tiles(ys_ref, _dot(act.astype(BF16), wdb_ref[...]))

    @pl.when(j >= nv_ref[0])
    def _():
        ys_ref[...] = jnp.zeros_like(ys_ref)


def _experts_call(tile_e, nvalid, xs, wg, wu, wd, *, tg, layer):
    rows = xs.shape[0] // ROW_TILE
    last = lambda j, te, nv: jnp.minimum(j, nv[0] - 1)
    row_blk = pl.BlockSpec((tg * ROW_TILE, 128), lambda j, te, nv: (last(j, te, nv), 0))
    out_blk = pl.BlockSpec((tg * ROW_TILE, 128), lambda j, te, nv: (j, 0))
    w_blk = lambda a: pl.BlockSpec((1, 1) + a.shape[2:],
                                   lambda j, te, nv: (layer, te[last(j, te, nv)], 0, 0))
    grid_spec = pltpu.PrefetchScalarGridSpec(
        num_scalar_prefetch=2, grid=(rows // tg,),
        in_specs=[row_blk, w_blk(wg), w_blk(wu), w_blk(wd)],
        out_specs=out_blk,
        scratch_shapes=[pltpu.VMEM(wg.shape[2:], BF16), pltpu.VMEM(wu.shape[2:], BF16),
                        pltpu.VMEM(wd.shape[2:], BF16)])
    return pl.pallas_call(
        _experts_kernel,
        grid_spec=grid_spec,
        out_shape=jax.ShapeDtypeStruct((rows * ROW_TILE, 128), U32),
        compiler_params=pltpu.CompilerParams(
            dimension_semantics=("arbitrary",), vmem_limit_bytes=VMEM_LIMIT),
        name="experts",
    )(tile_e, nvalid, xs, wg, wu, wd)


def _combine_kernel(cur_ref, nxt_ref, base_ref, rw_ref, g2_ref, b2_ref, ys_hbm, y_ref, ybuf, sem,
                    *, tc):
    i = pl.program_id(0)
    n_steps = pl.num_programs(0)
    slot = i % 2

    def gathered(slot_):
        return pltpu.make_async_copy(ys_hbm.at[pl.ds(0, 2 * tc * ROW_TILE)], ybuf.at[slot_],
                                     sem.at[slot_])

    def issue(rows_ref, slot_):
        def body(g, c):
            t0 = g * DMA_BATCH
            src = [rows_ref[2 * t0 + u] for u in range(2 * DMA_BATCH)]
            for u in range(2 * DMA_BATCH):
                _row_copy(ys_hbm, src[u], ybuf.at[slot_], (u % 2) * tc + t0 + u // 2,
                          sem.at[slot_]).start()
            return c
        lax.fori_loop(0, tc // DMA_BATCH, body, 0)

    @pl.when(i == 0)
    def _():
        issue(cur_ref, 0)

    @pl.when(i + 1 < n_steps)
    def _():
        issue(nxt_ref, 1 - slot)

    gathered(slot).wait()
    y1a, y1b = _load_row_tiles(ybuf.at[slot], 0, tc)
    y2a, y2b = _load_row_tiles(ybuf.at[slot], tc, tc)
    rw = rw_ref[...]
    w1, w2 = rw[:, 0:1], rw[:, 1:2]
    moe = jnp.concatenate([w1 * y1a + w2 * y2a, w1 * y1b + w2 * y2b], axis=1)
    y_ref[...] = _layer_norm(base_ref[...] + moe, g2_ref[...], b2_ref[...])


def _combine_call(slot_flat, base, rw, g2, b2, ys, *, tc):
    n, d = base.shape
    n_steps = n // tc
    cur = pl.BlockSpec((2 * tc,), lambda i: (i,), memory_space=pltpu.SMEM)
    nxt = pl.BlockSpec((2 * tc,), lambda i: (jnp.minimum(i + 1, n_steps - 1),),
                       memory_space=pltpu.SMEM)
    tok = lambda w: pl.BlockSpec((tc, w), lambda i: (i, 0))
    full = lambda a: pl.BlockSpec(a.shape, lambda i: (0,) * a.ndim)
    return pl.pallas_call(
        functools.partial(_combine_kernel, tc=tc),
        grid=(n_steps,),
        in_specs=[cur, nxt, tok(d), tok(8), full(g2), full(b2), pl.BlockSpec(memory_space=pl.ANY)],
        out_specs=tok(d),
        out_shape=jax.ShapeDtypeStruct((n, d), F32),
        scratch_shapes=[pltpu.VMEM((2, 2 * tc * ROW_TILE, 128), U32),
                        pltpu.SemaphoreType.DMA((2,))],
        compiler_params=pltpu.CompilerParams(
            dimension_semantics=("arbitrary",), vmem_limit_bytes=VMEM_LIMIT),
        name="combine",
    )(slot_flat, slot_flat, base, rw, g2, b2, ys)


def _expert_segments(counts, tg, n_tiles):
    cnt = counts[0, N_GROUPS:N_GROUPS + N_EXPERTS].astype(jnp.int32)
    pc = ((cnt + tg - 1) // tg) * tg
    seg_end = jnp.cumsum(pc)
    seg = seg_end - pc
    nvalid = (seg_end[-1] // tg).reshape(1)
    tile_e = jnp.searchsorted(seg_end, jnp.arange(n_tiles, dtype=jnp.int32) * tg, side="right")
    tile_e = jnp.minimum(tile_e, N_EXPERTS - 1).astype(jnp.int32)
    return seg.astype(jnp.int32), pc.astype(jnp.int32), tile_e, nvalid.astype(jnp.int32)


def _cols(w, name):
    lo, hi = _OFF[name]
    return w[:, lo:hi]


def _rot_pairs(w):
    return jnp.concatenate([-w[..., 16:32], w[..., 0:16]], axis=-1)


def _layer_weights(w_in, w_uq, w_ukv, b_forget):
    wmain = jnp.concatenate([_cols(w_in, n) for n in ("ka", "cq", "ckv", "kc")], axis=1)
    kr = _cols(w_in, "kr")
    fc_pad = jnp.pad(_cols(w_in, "fc"), ((0, 0), (0, FC_ROWS - C_HEADS)))
    wt = jnp.concatenate([_cols(w_in, "qa"), _cols(w_in, "va"), _cols(w_in, "qc"), _cols(w_in, "vc"),
                          kr, _rot_pairs(kr), fc_pad], axis=1).T
    wq = w_uq.reshape(B_Q_RANK, B_HEADS, B_NOPE + B_ROPE)
    zpad = jnp.zeros((B_Q_RANK, B_HEADS, B_PAD - B_NOPE - B_ROPE), w_uq.dtype)
    plain = jnp.concatenate([wq, zpad], axis=-1).reshape(B_Q_RANK, B_HEADS * B_PAD)
    rot = jnp.concatenate([jnp.zeros_like(wq[..., :B_NOPE]), _rot_pairs(wq[..., B_NOPE:]), zpad],
                          axis=-1).reshape(B_Q_RANK, B_HEADS * B_PAD)
    wuq = jnp.concatenate([plain, rot], axis=1).T
    wkv = w_ukv.reshape(B_KV_RANK, B_HEADS, B_NOPE + B_VDIM)
    wukvk = wkv[..., :B_NOPE].reshape(B_KV_RANK, B_HEADS * B_NOPE).T
    wukvv = wkv[..., B_NOPE:].reshape(B_KV_RANK, B_HEADS * B_VDIM).T
    bfc = jnp.pad(b_forget, (0, FC_ROWS - C_HEADS)).reshape(FC_ROWS, 1)
    return (wmain.astype(BF16), wt.astype(BF16), wuq.astype(BF16), wukvk.astype(BF16),
            wukvv.astype(BF16), bfc.astype(F32))


def _rope_inputs(positions):
    half = B_ROPE // 2
    inv = ROPE_THETA ** (-jnp.arange(half, dtype=F32) / half)
    ang = positions.astype(F32)[..., None] * inv
    cos, sin = jnp.cos(ang), jnp.sin(ang)
    b, s, _ = cos.shape
    ones = jnp.ones((b, s, B_NOPE), F32)
    zeros = jnp.zeros((b, s, B_PAD - B_NOPE - B_ROPE), F32)
    cos128 = jnp.concatenate([ones, cos, cos, zeros], axis=-1)
    sin128 = jnp.concatenate([jnp.zeros_like(ones), sin, sin, zeros], axis=-1)
    cost = jnp.swapaxes(jnp.concatenate([cos, cos], axis=-1), 1, 2)
    sint = jnp.swapaxes(jnp.concatenate([sin, sin], axis=-1), 1, 2)
    return jnp.swapaxes(cos128, 1, 2), jnp.swapaxes(sin128, 1, 2), cost, sint


def kernel(x, p, positions, w_in, w_uq, w_ukv, g_cq, g_ckv, lam_q1, lam_k1, lam_q2, lam_k2, g_diff,
           b_forget, w_out, ln1_g, ln1_b, w_group, b_group, w_erouter, b_erouter, w_gate_e, w_up_e,
           w_down_e, w_ple_gate, w_ple_proj, ln2_g, ln2_b):
    b, s, d = x.shape
    n = b * s
    tq = 512
    tm_prep = 512
    tm_moe = 512
    tg = 512
    rows = 2 * n + N_EXPERTS * tg
    tb = min(2048, n // 2)
    tc = 512
    cos128, sin128, cost, sint = _rope_inputs(positions)
    posf = positions.astype(F32)
    pcol = posf[:, :, None]
    prow = posf[:, None, :]
    slopes = 2.0 ** (-8.0 * jnp.arange(1, A_HEADS + 1, dtype=F32) / A_HEADS)
    slopes = jnp.broadcast_to(
        jnp.pad(slopes.reshape(3, 2), ((0, 0), (0, 6)))[:, :, None], (3, 8, 128))
    row = lambda v: v.reshape(1, -1).astype(F32)

    for i in range(DEPTH):
        wmain, wt, wuq, wukvk, wukvv, bfc = _layer_weights(w_in[i], w_uq[i], w_ukv[i], b_forget[i])
        ka, kb, kc, qta, qtb, qtc, vta, vtb, vtc, cumt = _prep_call(
            x, wmain, wt, wuq, wukvk, wukvv, row(g_cq[i]), row(g_ckv[i]), bfc,
            cos128, sin128, cost, sint, tm=tm_prep)
        lamv = jnp.pad(jnp.stack([lam_q1[i], lam_k1[i], lam_q2[i], lam_k2[i]]).astype(F32),
                       ((0, 4), (0, 128 - A_DIM)))
        gcol = jnp.tile(g_diff[i].astype(F32), 2).reshape(128, 1)
        lam_init = 0.8 - 0.6 * math.exp(-0.3 * i)
        oa = _attn_a_call(qta, ka, vta, pcol, prow, slopes, lamv, gcol, tq=tq, lam_init=lam_init,
                          pairs=1)
        ob = _attn_b_call(qtb, kb, vtb, tq=tq, pairs=3)
        crow = cumt[:, :C_HEADS].reshape(b, 2, 2, s)
        ccol = jnp.swapaxes(crow, 2, 3)
        oc = _attn_c_call(qtc, kc, vtc, ccol, crow, tq=tq)

        wo = w_out[i].astype(BF16)
        wr = jnp.concatenate([w_group[i], jnp.moveaxis(w_erouter[i], 0, 1).reshape(d, N_EXPERTS)],
                             axis=1)
        wr = jnp.pad(wr, ((0, 0), (0, ROUTER_COLS - wr.shape[1]))).astype(F32)
        wr1 = wr.astype(BF16)
        wr2 = (wr - wr1.astype(F32)).astype(BF16)
        rbias = jnp.pad(jnp.concatenate([b_group[i], b_erouter[i].reshape(-1)]),
                        (0, ROUTER_COLS - N_GROUPS - N_EXPERTS)).reshape(1, ROUTER_COLS).astype(F32)
        h, base, ri, rw, counts = _post_call(
            x.reshape(n, d), oa.reshape(n, MIX_A), ob.reshape(n, MIX_B), oc.reshape(n, MIX_C),
            p[i].reshape(n, PLE_DIM), wo[:MIX_A], wo[MIX_A:MIX_A + MIX_B], wo[MIX_A + MIX_B:],
            row(ln1_g[i]), row(ln1_b[i]), wr1, jnp.concatenate([wr1, wr2], axis=1), rbias,
            w_ple_gate[i].astype(BF16),
            w_ple_proj[i].astype(BF16), tm=tm_moe)
        seg, pc, tile_e, nvalid = _expert_segments(counts, tg, rows // tg)
        seg_row = jnp.pad(seg.astype(F32), (0, ROUTER_COLS - N_EXPERTS)).reshape(1, ROUTER_COLS)
        slot_flat = _slot_call(ri, seg_row, tm=min(4096, n))[:, 0:2].reshape(-1)
        xs = _dispatch_call(seg, pc, nvalid, slot_flat, h, rows=rows, tb=tb, tg=tg)
        ys = _experts_call(tile_e, nvalid, xs, w_gate_e, w_up_e, w_down_e, tg=tg, layer=i)
        y = _combine_call(slot_flat, base, rw, row(ln2_g[i]), row(ln2_b[i]), ys, tc=tc)
        x = y.reshape(b, s, d)
    return x
```

```python
import functools
import math

import jax
import jax.numpy as jnp
from jax import lax
from jax.experimental import pallas as pl
from jax.experimental.pallas import tpu as pltpu

F32 = jnp.float32
BF16 = jnp.bfloat16

D_MODEL = 1024
DEPTH = 2
CHUNK = 64
PLE_DIM = 256

A_HEADS = 6
A_DIM = 32
A_VDIM = 64
B_HEADS = 6
B_Q_RANK = 256
B_KV_RANK = 128
B_NOPE = 64
B_ROPE = 32
B_VDIM = 64
ROPE_THETA = 10000.0
C_HEADS = 4
C_DIM = 64

MIX_A = A_HEADS * A_VDIM
MIX_B = B_HEADS * B_VDIM
MIX_C = C_HEADS * C_DIM
B_PAD = 128

N_GROUPS = 4
EXPERTS_PER_GROUP = 8
N_EXPERTS = N_GROUPS * EXPERTS_PER_GROUP
D_EXPERT = 256
ROUTER_COLS = 128

DEEPNORM_ALPHA = (2 * DEPTH) ** 0.25
LN_EPS = 1e-5
RMS_EPS = 1e-6
NEG_BIG = -1e30
LOG2E = math.log2(math.e)

_OFF = {}
_o = 0
for _name, _w in (("qa", 384), ("ka", 384), ("va", 384), ("cq", 256), ("ckv", 128), ("kr", 32),
                  ("qc", 256), ("kc", 256), ("vc", 256), ("fc", 4)):
    _OFF[_name] = (_o, _o + _w)
    _o += _w

T_QA, T_VA, T_QC, T_VC, T_KR, T_KRS, T_FC, T_ROWS = 0, 384, 768, 1024, 1280, 1312, 1344, 1360
FC_ROWS = T_ROWS - T_FC

VMEM_LIMIT = 48 * 1024 * 1024
DMA_BATCH = 4

NT_DIMS = (((1,), (1,)), ((), ()))


def _dot(a, b):
    return jnp.dot(a, b, preferred_element_type=F32)


def _dot_nt(a, b):
    return lax.dot_general(a, b, NT_DIMS, preferred_element_type=F32)


U32 = jnp.uint32
HALF = D_MODEL // 2
ROW_TILE = HALF // 128


def _pack_rows(v):
    hi = lax.bitcast_convert_type(v[:, :HALF].astype(BF16).astype(F32), U32)
    lo = lax.bitcast_convert_type(v[:, HALF:].astype(BF16).astype(F32), U32)
    return hi | lax.shift_right_logical(lo, jnp.full_like(lo, 16))


def _unpack_rows(w):
    first = lax.bitcast_convert_type(w & jnp.full_like(w, 0xFFFF0000), F32)
    second = lax.bitcast_convert_type(lax.shift_left(w, jnp.full_like(w, 16)), F32)
    return first, second


def _store_row_tiles(ref, v):
    rows = v.shape[0]
    w = _pack_rows(v)
    for c in range(ROW_TILE):
        ref[pl.ds(c, rows, stride=ROW_TILE), :] = w[:, c * 128:(c + 1) * 128]


def _load_row_tiles(ref, start, rows):
    w = jnp.concatenate(
        [ref[pl.ds(start * ROW_TILE + c, rows, stride=ROW_TILE), :] for c in range(ROW_TILE)],
        axis=1)
    return _unpack_rows(w)


def _split3(a):
    a1 = a.astype(BF16)
    r1 = a - a1.astype(F32)
    a2 = r1.astype(BF16)
    a3 = (r1 - a2.astype(F32)).astype(BF16)
    return a1, a2, a3


def _prep_kernel(x_ref, wmain_ref, wt_ref, wuq_ref, wukvk_ref, wukvv_ref, gcq_ref, gckv_ref,
                 bf_ref, cos_ref, sin_ref, cost_ref, sint_ref,
                 ka_ref, kb_ref, kc_ref, qta_ref, qtb_ref, qtc_ref, vta_ref, vtb_ref, vtc_ref,
                 cum_ref, carry_ref, *, tm):
    j = pl.program_id(1)

    @pl.when(j == 0)
    def _():
        carry_ref[...] = jnp.zeros_like(carry_ref)

    xb = x_ref[0].astype(BF16)
    z = _dot(xb, wmain_ref[...])
    zt = _dot_nt(wt_ref[...], xb)

    ka_ref[0] = z[:, 0:384].astype(BF16)
    qta_ref[0] = (zt[T_QA:T_QA + 384] * (LOG2E / math.sqrt(A_DIM))).astype(BF16)
    vta_ref[0] = zt[T_VA:T_VA + 384].astype(BF16)
    kc_ref[0] = z[:, 768:1024].astype(BF16)
    qtc_ref[0] = (zt[T_QC:T_QC + 256] * (LOG2E / math.sqrt(C_DIM))).astype(BF16)
    vtc_ref[0] = zt[T_VC:T_VC + 256].astype(BF16)

    fct = zt[T_FC:T_ROWS] + bf_ref[...]
    logf = jnp.minimum(fct, 0.0) - jnp.log(1.0 + jnp.exp(-jnp.abs(fct)))
    r_i = lax.broadcasted_iota(jnp.int32, (tm, tm), 0)
    c_i = lax.broadcasted_iota(jnp.int32, (tm, tm), 1)
    tri = jnp.where(r_i <= c_i, 1.0, 0.0).astype(BF16)
    l1, l2, l3 = _split3(logf)
    cum = _dot(l1, tri) + _dot(l2, tri) + _dot(l3, tri) + carry_ref[:, 0:1]
    cum_ref[0] = cum * LOG2E
    carry_ref[...] = jnp.broadcast_to(cum[:, tm - 1:tm], carry_ref.shape)

    cq = z[:, 384:640]
    cq_n = cq * lax.rsqrt(jnp.mean(cq * cq, axis=-1, keepdims=True) + RMS_EPS) * gcq_ref[...]
    zqt = _dot_nt(wuq_ref[...], cq_n.astype(BF16))
    cosb = cos_ref[0]
    sinb = sin_ref[0]
    scale_b = LOG2E / math.sqrt(B_NOPE + B_ROPE)
    for h in range(B_HEADS):
        lo = h * B_PAD
        qh = zqt[lo:lo + B_PAD] * cosb + zqt[768 + lo:768 + lo + B_PAD] * sinb
        qtb_ref[0, lo:lo + B_PAD, :] = (qh * scale_b).astype(BF16)

    ckv = z[:, 640:768]
    ckv_n = ckv * lax.rsqrt(jnp.mean(ckv * ckv, axis=-1, keepdims=True) + RMS_EPS) * gckv_ref[...]
    ckv_b = ckv_n.astype(BF16)
    vtb_ref[0] = _dot_nt(wukvv_ref[...], ckv_b).astype(BF16)
    knt = _dot_nt(wukvk_ref[...], ckv_b)
    krt = zt[T_KR:T_KR + 32] * cost_ref[0] + zt[T_KRS:T_KRS + 32] * sint_ref[0]
    zpad = jnp.zeros((B_PAD - B_NOPE - B_ROPE, tm), F32)
    pieces = []
    for h in range(B_HEADS):
        pieces += [knt[h * B_NOPE:(h + 1) * B_NOPE], krt, zpad]
    kb_ref[0] = jnp.concatenate(pieces, axis=0).T.astype(BF16)


def _prep_call(x, wmain, wt, wuq, wukvk, wukvv, gcq, gckv, bfc, cos128, sin128, cost, sint, *, tm):
    b, s, d = x.shape
    grid = (b, s // tm)
    full = lambda shape: pl.BlockSpec(shape, lambda bi, j: (0,) * len(shape))
    tok = lambda w: pl.BlockSpec((1, tm, w), lambda bi, j: (bi, j, 0))
    tr = lambda r: pl.BlockSpec((1, r, tm), lambda bi, j: (bi, 0, j))
    out_shape = (
        jax.ShapeDtypeStruct((b, s, 384), BF16),
        jax.ShapeDtypeStruct((b, s, 768), BF16),
        jax.ShapeDtypeStruct((b, s, 256), BF16),
        jax.ShapeDtypeStruct((b, 384, s), BF16),
        jax.ShapeDtypeStruct((b, 768, s), BF16),
        jax.ShapeDtypeStruct((b, 256, s), BF16),
        jax.ShapeDtypeStruct((b, 384, s), BF16),
        jax.ShapeDtypeStruct((b, 384, s), BF16),
        jax.ShapeDtypeStruct((b, 256, s), BF16),
        jax.ShapeDtypeStruct((b, FC_ROWS, s), F32),
    )
    out_specs = (tok(384), tok(768), tok(256), tr(384), tr(768), tr(256), tr(384), tr(384),
                 tr(256), tr(FC_ROWS))
    in_specs = [tok(d), full(wmain.shape), full(wt.shape), full(wuq.shape), full(wukvk.shape),
                full(wukvv.shape), full(gcq.shape), full(gckv.shape), full(bfc.shape),
                tr(128), tr(128), tr(32), tr(32)]
    return pl.pallas_call(
        functools.partial(_prep_kernel, tm=tm),
        grid=grid, in_specs=in_specs, out_specs=out_specs, out_shape=out_shape,
        scratch_shapes=[pltpu.VMEM((FC_ROWS, 128), F32)],
        compiler_params=pltpu.CompilerParams(
            dimension_semantics=("arbitrary", "arbitrary"), vmem_limit_bytes=VMEM_LIMIT),
        name="prep",
    )(x, wmain, wt, wuq, wukvk, wukvv, gcq, gckv, bfc, cos128, sin128, cost, sint)


def _flash(i, tq, n_maps, scores_fn, mask):
    def update(state, j, diag):
        new = []
        for (m, l, acc), (s, v) in zip(state, scores_fn(j)):
            if diag:
                s = jnp.where(mask, s, NEG_BIG)
            m_new = jnp.maximum(m, jnp.max(s, axis=1, keepdims=True))
            alpha = jnp.exp2(m - m_new)
            p = jnp.exp2(s - m_new)
            l_new = alpha * l + jnp.sum(p, axis=1, keepdims=True)
            acc_new = alpha * acc + _dot(p.astype(BF16), v)
            new.append((m_new, l_new, acc_new))
        return tuple(new)

    init = tuple((jnp.full((tq, 1), NEG_BIG, F32), jnp.zeros((tq, 1), F32),
                  jnp.zeros((tq, 128), F32)) for _ in range(n_maps))
    state = update(init, i, True)
    state = lax.fori_loop(0, i, lambda j, st: update(st, j, False), state)
    return [acc * (1.0 / l) for (_, l, acc) in state]


def _chunk_mask(tq):
    r = lax.broadcasted_iota(jnp.int32, (tq, tq), 0)
    c = lax.broadcasted_iota(jnp.int32, (tq, tq), 1)
    return (c // CHUNK) <= (r // CHUNK)


def _causal_mask(tq):
    r = lax.broadcasted_iota(jnp.int32, (tq, tq), 0)
    c = lax.broadcasted_iota(jnp.int32, (tq, tq), 1)
    return c <= r


ONES_ROWS = 16


def _flash_t(i, tq, n_maps, scores_fn, mask):
    ones = jnp.ones((ONES_ROWS, tq), BF16)

    def update(state, j, diag):
        new = []
        for (m, acc), (s, vt) in zip(state, scores_fn(j)):
            if diag:
                s = jnp.where(mask, s, NEG_BIG)
            m_new = jnp.maximum(m, jnp.max(s, axis=0, keepdims=True))
            alpha = jnp.exp2(m - m_new)
            p = jnp.exp2(s - m_new).astype(BF16)
            vext = jnp.concatenate([vt, ones], axis=0)
            new.append((m_new, alpha * acc + _dot(vext, p)))
        return tuple(new)

    init = tuple((jnp.full((1, tq), NEG_BIG, F32), jnp.zeros((128 + ONES_ROWS, tq), F32))
                 for _ in range(n_maps))
    state = update(init, i, True)
    state = lax.fori_loop(0, i, lambda j, st: update(st, j, False), state)
    return [acc[0:128] * (1.0 / acc[128:129]) for (_, acc) in state]


def _chunk_mask_t(tq):
    r = lax.broadcasted_iota(jnp.int32, (tq, tq), 0)
    c = lax.broadcasted_iota(jnp.int32, (tq, tq), 1)
    return (r // CHUNK) <= (c // CHUNK)


def _causal_mask_t(tq):
    r = lax.broadcasted_iota(jnp.int32, (tq, tq), 0)
    c = lax.broadcasted_iota(jnp.int32, (tq, tq), 1)
    return r <= c


def _attn_a_kernel(qt_ref, k_ref, vt_ref, pcol_ref, prow_ref, slope_ref, lam_ref, g_ref, o_ref,
                   *, tq, lam_init, pairs):
    i = pl.program_id(2)
    row = lax.broadcasted_iota(jnp.int32, (128, tq), 0)
    pt = prow_ref[0]
    mask = _chunk_mask_t(tq)
    lam_v = lam_ref[...]
    lam = (jnp.exp(jnp.sum(lam_v[0:1] * lam_v[1:2], axis=1, keepdims=True))
           - jnp.exp(jnp.sum(lam_v[2:3] * lam_v[3:4], axis=1, keepdims=True)) + lam_init)

    def koff(j):
        return pl.multiple_of(j * tq, tq)

    qms, nslopes = [], []
    for pr in range(pairs):
        qt = qt_ref[0, pr * 128:(pr + 1) * 128, :]
        qms += [jnp.where((row >= 32 * mi) & (row < 32 * mi + 32), qt, jnp.zeros_like(qt))
                for mi in range(4)]
        nslopes += [-LOG2E * slope_ref[pr, hh:hh + 1, 0:1] for hh in range(2)]

    def scores_fn(j):
        ps = pcol_ref[0, pl.ds(koff(j), tq), :]
        dist = jnp.abs(ps - pt)
        out = []
        for pr in range(pairs):
            k = k_ref[0, pl.ds(koff(j), tq), pr * 128:(pr + 1) * 128]
            vt = vt_ref[0, pr * 128:(pr + 1) * 128, pl.ds(koff(j), tq)]
            for hh in range(2):
                bias = nslopes[2 * pr + hh] * dist
                out += [(_dot(k, qms[4 * pr + 2 * hh + mm]) + bias, vt) for mm in range(2)]
        return out

    o = _flash_t(i, tq, 4 * pairs, scores_fn, mask)
    for pr in range(pairs):
        o0, o1, o2, o3 = o[4 * pr:4 * pr + 4]
        ot = jnp.where(row < 64, o0 - lam * o1, o2 - lam * o3)
        sq = ot * ot
        ms0 = jnp.sum(sq[0:64], axis=0, keepdims=True) * (1.0 / A_VDIM)
        ms1 = jnp.sum(sq[64:128], axis=0, keepdims=True) * (1.0 / A_VDIM)
        ms = jnp.where(row < 64, ms0, ms1)
        yt = ot * lax.rsqrt(ms + RMS_EPS) * g_ref[...] * (1.0 - lam_init)
        o_ref[0, :, pr * 128:(pr + 1) * 128] = yt.T.astype(BF16)


def _attn_b_kernel(qt_ref, k_ref, vt_ref, o_ref, *, tq, pairs):
    i = pl.program_id(2)
    row = lax.broadcasted_iota(jnp.int32, (128, tq), 0)
    mask = _chunk_mask_t(tq)
    heads = 2 * pairs

    def koff(j):
        return pl.multiple_of(j * tq, tq)

    qts = [qt_ref[0, h * B_PAD:(h + 1) * B_PAD, :] for h in range(heads)]

    def scores_fn(j):
        vts = [vt_ref[0, pr * 128:(pr + 1) * 128, pl.ds(koff(j), tq)] for pr in range(pairs)]
        return [(_dot(k_ref[0, pl.ds(koff(j), tq), h * B_PAD:(h + 1) * B_PAD], qts[h]), vts[h // 2])
                for h in range(heads)]

    outs = _flash_t(i, tq, heads, scores_fn, mask)
    for pr in range(pairs):
        o_ref[0, :, pr * 128:(pr + 1) * 128] = jnp.where(
            row < 64, outs[2 * pr], outs[2 * pr + 1]).T.astype(BF16)


def _attn_c_kernel(qt_ref, k_ref, vt_ref, ccol_ref, crow_ref, o_ref, *, tq):
    i = pl.program_id(2)
    row = lax.broadcasted_iota(jnp.int32, (128, tq), 0)
    mask = _causal_mask_t(tq)
    pairs = C_HEADS // 2

    def koff(j):
        return pl.multiple_of(j * tq, tq)

    qms = []
    for pr in range(pairs):
        qt = qt_ref[0, pr * 128:(pr + 1) * 128, :]
        qms += [jnp.where((row >= 64 * hh) & (row < 64 * hh + 64), qt, jnp.zeros_like(qt))
                for hh in range(2)]
    cts = [crow_ref[0, h // 2, h % 2:h % 2 + 1, :] for h in range(C_HEADS)]

    def scores_fn(j):
        out = []
        for pr in range(pairs):
            k = k_ref[0, pl.ds(koff(j), tq), pr * 128:(pr + 1) * 128]
            vt = vt_ref[0, pr * 128:(pr + 1) * 128, pl.ds(koff(j), tq)]
            for hh in range(2):
                cs = ccol_ref[0, pr, pl.ds(koff(j), tq), hh:hh + 1]
                out.append((_dot(k, qms[2 * pr + hh]) + (cts[2 * pr + hh] - cs), vt))
        return out

    outs = _flash_t(i, tq, C_HEADS, scores_fn, mask)
    for pr in range(pairs):
        o_ref[0, :, pr * 128:(pr + 1) * 128] = jnp.where(
            row < 64, outs[2 * pr], outs[2 * pr + 1]).T.astype(BF16)


def _attn_params():
    return pltpu.CompilerParams(
        dimension_semantics=("arbitrary", "arbitrary", "arbitrary"), vmem_limit_bytes=VMEM_LIMIT)


def _attn_a_call(qta, ka, vta, pcol, prow, slopes, lamv, gcol, *, tq, lam_init, pairs):
    b, s, _ = ka.shape
    grid = (b, A_HEADS // (2 * pairs), s // tq)
    w = pairs * 128
    return pl.pallas_call(
        functools.partial(_attn_a_kernel, tq=tq, lam_init=lam_init, pairs=pairs),
        grid=grid,
        in_specs=[
            pl.BlockSpec((1, w, tq), lambda bi, c, i: (bi, c, i)),
            pl.BlockSpec((1, s, w), lambda bi, c, i: (bi, 0, c)),
            pl.BlockSpec((1, w, s), lambda bi, c, i: (bi, c, 0)),
            pl.BlockSpec((1, s, 1), lambda bi, c, i: (bi, 0, 0)),
            pl.BlockSpec((1, 1, tq), lambda bi, c, i: (bi, 0, i)),
            pl.BlockSpec((pairs, 8, 128), lambda bi, c, i: (c, 0, 0)),
            pl.BlockSpec((8, 128), lambda bi, c, i: (0, 0)),
            pl.BlockSpec((128, 1), lambda bi, c, i: (0, 0)),
        ],
        out_specs=pl.BlockSpec((1, tq, w), lambda bi, c, i: (bi, i, c)),
        out_shape=jax.ShapeDtypeStruct((b, s, MIX_A), BF16),
        compiler_params=_attn_params(), name="attn_a",
    )(qta, ka, vta, pcol, prow, slopes, lamv, gcol)


def _attn_b_call(qtb, kb, vtb, *, tq, pairs):
    b, s, _ = kb.shape
    grid = (b, B_HEADS // (2 * pairs), s // tq)
    return pl.pallas_call(
        functools.partial(_attn_b_kernel, tq=tq, pairs=pairs),
        grid=grid,
        in_specs=[
            pl.BlockSpec((1, 2 * pairs * B_PAD, tq), lambda bi, c, i: (bi, c, i)),
            pl.BlockSpec((1, s, 2 * pairs * B_PAD), lambda bi, c, i: (bi, 0, c)),
            pl.BlockSpec((1, pairs * 128, s), lambda bi, c, i: (bi, c, 0)),
        ],
        out_specs=pl.BlockSpec((1, tq, pairs * 128), lambda bi, c, i: (bi, i, c)),
        out_shape=jax.ShapeDtypeStruct((b, s, MIX_B), BF16),
        compiler_params=_attn_params(), name="attn_b",
    )(qtb, kb, vtb)


def _attn_c_call(qtc, kc, vtc, ccol, crow, *, tq):
    b, s, _ = kc.shape
    grid = (b, 1, s // tq)
    return pl.pallas_call(
        functools.partial(_attn_c_kernel, tq=tq),
        grid=grid,
        in_specs=[
            pl.BlockSpec((1, MIX_C, tq), lambda bi, c, i: (bi, 0, i)),
            pl.BlockSpec((1, s, MIX_C), lambda bi, c, i: (bi, 0, 0)),
            pl.BlockSpec((1, MIX_C, s), lambda bi, c, i: (bi, 0, 0)),
            pl.BlockSpec((1, 2, s, 2), lambda bi, c, i: (bi, 0, 0, 0)),
            pl.BlockSpec((1, 2, 2, tq), lambda bi, c, i: (bi, 0, 0, i)),
        ],
        out_specs=pl.BlockSpec((1, tq, MIX_C), lambda bi, c, i: (bi, i, 0)),
        out_shape=jax.ShapeDtypeStruct((b, s, MIX_C), BF16),
        compiler_params=_attn_params(), name="attn_c",
    )(qtc, kc, vtc, ccol, crow)


def _layer_norm(v, g, b):
    mu = jnp.mean(v, axis=-1, keepdims=True)
    c = v - mu
    var = jnp.mean(c * c, axis=-1, keepdims=True)
    return c * lax.rsqrt(var + LN_EPS) * g + b


def _route(h, wr1_ref, wr2_ref, rbias_ref):
    tm = h.shape[0]
    h1 = h.astype(BF16)
    h2 = (h - h1.astype(F32)).astype(BF16)
    r12 = _dot(h1, wr2_ref[...])
    logits = (r12[:, :ROUTER_COLS] + r12[:, ROUTER_COLS:] + _dot(h2, wr1_ref[...])
              + rbias_ref[...])
    col = lax.broadcasted_iota(jnp.int32, (tm, ROUTER_COLS), 1)
    colf = col.astype(F32)
    big = float(ROUTER_COLS)
    gmask = col < N_GROUPS
    gmax = jnp.max(jnp.where(gmask, logits, NEG_BIG), axis=1, keepdims=True)
    g_idx = jnp.min(jnp.where(gmask & (logits == gmax), colf, big), axis=1, keepdims=True)
    g_w = 1.0 / jnp.sum(jnp.where(gmask, jnp.exp(logits - gmax), 0.0), axis=1, keepdims=True)
    lo = N_GROUPS + g_idx * EXPERTS_PER_GROUP
    emask = (colf >= lo) & (colf < lo + EXPERTS_PER_GROUP)
    top1 = jnp.max(jnp.where(emask, logits, NEG_BIG), axis=1, keepdims=True)
    idx1 = jnp.min(jnp.where(emask & (logits == top1), colf, big), axis=1, keepdims=True)
    emask2 = emask & (colf != idx1)
    top2 = jnp.max(jnp.where(emask2, logits, NEG_BIG), axis=1, keepdims=True)
    idx2 = jnp.min(jnp.where(emask2 & (logits == top2), colf, big), axis=1, keepdims=True)
    e2 = jnp.exp(top2 - top1)
    w_1 = g_w / (1.0 + e2)
    w_2 = g_w * e2 / (1.0 + e2)
    return colf, idx1, idx2, w_1, w_2


def _post_kernel(x_ref, oa_ref, ob_ref, oc_ref, p_ref, woa_ref, wob_ref, woc_ref, g1_ref, b1_ref,
                 wr1_ref, wr2_ref, rbias_ref, wpg_ref, wpp_ref,
                 h_ref, base_ref, ri_ref, rw_ref, cnt_ref, carry_ref):
    i = pl.program_id(0)
    tm = x_ref.shape[0]

    @pl.when(i == 0)
    def _():
        carry_ref[...] = jnp.zeros_like(carry_ref)

    mix = (_dot(oa_ref[...], woa_ref[...]) + _dot(ob_ref[...], wob_ref[...])
           + _dot(oc_ref[...], woc_ref[...]))
    h = _layer_norm(DEEPNORM_ALPHA * x_ref[...] + mix, g1_ref[...], b1_ref[...])
    _store_row_tiles(h_ref, h)
    gl = _dot(h.astype(BF16), wpg_ref[...])
    ple = (1.0 / (1.0 + jnp.exp(-gl))) * _dot(p_ref[...].astype(BF16), wpp_ref[...])
    base_ref[...] = DEEPNORM_ALPHA * h + ple

    colf, idx1, idx2, w_1, w_2 = _route(h, wr1_ref, wr2_ref, rbias_ref)
    hit1 = colf == idx1
    hit2 = colf == idx2
    onehot = jnp.where(hit1 | hit2, 1.0, 0.0)
    r_i = lax.broadcasted_iota(jnp.int32, (tm, tm), 0)
    c_i = lax.broadcasted_iota(jnp.int32, (tm, tm), 1)
    before = jnp.where(c_i < r_i, 1.0, 0.0).astype(BF16)
    seen = _dot(before, onehot.astype(BF16)) + carry_ref[0:1, :]
    rank1 = jnp.sum(jnp.where(hit1, seen, 0.0), axis=1, keepdims=True)
    rank2 = jnp.sum(jnp.where(hit2, seen, 0.0), axis=1, keepdims=True)
    total = carry_ref[0:1, :] + jnp.sum(onehot, axis=0, keepdims=True)
    carry_ref[...] = jnp.broadcast_to(total, carry_ref.shape)
    cnt_ref[...] = jnp.broadcast_to(total, cnt_ref.shape)

    lane8 = lax.broadcasted_iota(jnp.int32, (tm, 8), 1)
    ri = jnp.where(lane8 == 0, idx1 - N_GROUPS,
                   jnp.where(lane8 == 1, idx2 - N_GROUPS,
                             jnp.where(lane8 == 2, rank1, jnp.where(lane8 == 3, rank2, 0.0))))
    ri_ref[...] = ri.astype(jnp.int32)
    rw_ref[...] = jnp.where(lane8 == 0, w_1, jnp.where(lane8 == 1, w_2, 0.0))


def _post_call(x2, oa, ob, oc, p2, woa, wob, woc, g1, b1, wr1, wr2, rbias, wpg, wpp, *, tm):
    n, d = x2.shape
    tok = lambda w: pl.BlockSpec((tm, w), lambda t: (t, 0))
    full = lambda a: pl.BlockSpec(a.shape, lambda t: (0,) * a.ndim)
    return pl.pallas_call(
        _post_kernel,
        grid=(n // tm,),
        in_specs=[tok(d), tok(MIX_A), tok(MIX_B), tok(MIX_C), tok(PLE_DIM), full(woa), full(wob),
                  full(woc), full(g1), full(b1), full(wr1), full(wr2), full(rbias), full(wpg),
                  full(wpp)],
        out_specs=(pl.BlockSpec((tm * ROW_TILE, 128), lambda t: (t, 0)), tok(d), tok(8), tok(8),
                   pl.BlockSpec((8, ROUTER_COLS), lambda t: (0, 0))),
        out_shape=(jax.ShapeDtypeStruct((n * ROW_TILE, 128), U32), jax.ShapeDtypeStruct((n, d), F32),
                   jax.ShapeDtypeStruct((n, 8), jnp.int32), jax.ShapeDtypeStruct((n, 8), F32),
                   jax.ShapeDtypeStruct((8, ROUTER_COLS), F32)),
        scratch_shapes=[pltpu.VMEM((8, ROUTER_COLS), F32)],
        compiler_params=pltpu.CompilerParams(
            dimension_semantics=("arbitrary",), vmem_limit_bytes=VMEM_LIMIT),
        name="post",
    )(x2, oa, ob, oc, p2, woa, wob, woc, g1, b1, wr1, wr2, rbias, wpg, wpp)


def _row_copy(src_ref, src_row, dst_ref, dst_row, sem):
    src = pl.ds(pl.multiple_of(src_row * ROW_TILE, ROW_TILE), ROW_TILE)
    dst = pl.ds(pl.multiple_of(dst_row * ROW_TILE, ROW_TILE), ROW_TILE)
    return pltpu.make_async_copy(src_ref.at[src], dst_ref.at[dst], sem)


def _slot_kernel(ri_ref, seg_ref, slot_ref):
    tm = ri_ref.shape[0]
    ri = ri_ref[...]
    colf = lax.broadcasted_iota(jnp.int32, (tm, ROUTER_COLS), 1).astype(F32)
    seg = seg_ref[...]
    lane8 = lax.broadcasted_iota(jnp.int32, (tm, 8), 1)
    out = jnp.zeros((tm, 8), F32)
    for k in range(2):
        e = ri[:, k:k + 1].astype(F32)
        start = jnp.sum(jnp.where(colf == e, seg, 0.0), axis=1, keepdims=True)
        out = jnp.where(lane8 == k, start + ri[:, 2 + k:3 + k].astype(F32), out)
    slot_ref[...] = out.astype(jnp.int32)


def _slot_call(ri, seg_row, *, tm):
    n = ri.shape[0]
    return pl.pallas_call(
        _slot_kernel,
        grid=(n // tm,),
        in_specs=[pl.BlockSpec((tm, 8), lambda t: (t, 0)),
                  pl.BlockSpec((1, ROUTER_COLS), lambda t: (0, 0))],
        out_specs=pl.BlockSpec((tm, 8), lambda t: (t, 0)),
        out_shape=jax.ShapeDtypeStruct((n, 8), jnp.int32),
        compiler_params=pltpu.CompilerParams(dimension_semantics=("arbitrary",)),
        name="slots",
    )(ri, seg_row)


def _dispatch_kernel(seg_ref, pc_ref, nv_ref, slot_ref, h_ref, xs_hbm, zbuf, sem, zsem, *, tb, tg):
    i = pl.program_id(0)
    n_tiles = xs_hbm.shape[0] // (tg * ROW_TILE)

    def unused_tile_zero_copy(jt):
        start = pl.multiple_of(jt * (tg * ROW_TILE), tg * ROW_TILE)
        return pltpu.make_async_copy(zbuf, xs_hbm.at[pl.ds(start, tg * ROW_TILE)], zsem)

    def for_unused_tiles(fn):
        def body(jt, c):
            fn(unused_tile_zero_copy(jt))
            return c
        lax.fori_loop(nv_ref[0], n_tiles, body, 0)

    def tail_zero_copy(e):
        start = pl.multiple_of((seg_ref[e] + pc_ref[e] - tg) * ROW_TILE, tg * ROW_TILE)
        return pltpu.make_async_copy(zbuf, xs_hbm.at[pl.ds(start, tg * ROW_TILE)], zsem)

    @pl.when(i == 0)
    def _():
        zbuf[...] = jnp.zeros_like(zbuf)
        for e in range(N_EXPERTS):
            @pl.when(pc_ref[e] > 0)
            def _():
                tail_zero_copy(e).start()
        for e in range(N_EXPERTS):
            @pl.when(pc_ref[e] > 0)
            def _():
                tail_zero_copy(e).wait()
        for_unused_tiles(lambda cp: cp.start())
        for_unused_tiles(lambda cp: cp.wait())

    def body(g, c):
        t0 = g * DMA_BATCH
        dst = [slot_ref[2 * t0 + u] for u in range(2 * DMA_BATCH)]
        for u in range(2 * DMA_BATCH):
            _row_copy(h_ref, t0 + u // 2, xs_hbm, dst[u], sem).start()
        return c

    lax.fori_loop(0, tb // DMA_BATCH, body, 0)
    for _ in range(2):
        pltpu.make_async_copy(h_ref, xs_hbm.at[pl.ds(0, tb * ROW_TILE)], sem).wait()


def _dispatch_call(seg, pc, nvalid, slot_flat, h, *, rows, tb, tg):
    n = h.shape[0] // ROW_TILE
    smem_blk = pl.BlockSpec((2 * tb,), lambda i, seg, pc, nv: (i,), memory_space=pltpu.SMEM)
    grid_spec = pltpu.PrefetchScalarGridSpec(
        num_scalar_prefetch=3, grid=(n // tb,),
        in_specs=[smem_blk,
                  pl.BlockSpec((tb * ROW_TILE, 128), lambda i, seg, pc, nv: (i, 0))],
        out_specs=pl.BlockSpec(memory_space=pl.ANY),
        scratch_shapes=[pltpu.VMEM((tg * ROW_TILE, 128), U32), pltpu.SemaphoreType.DMA(()),
                        pltpu.SemaphoreType.DMA(())])
    return pl.pallas_call(
        functools.partial(_dispatch_kernel, tb=tb, tg=tg),
        grid_spec=grid_spec,
        out_shape=jax.ShapeDtypeStruct((rows * ROW_TILE, 128), U32),
        compiler_params=pltpu.CompilerParams(
            dimension_semantics=("arbitrary",), vmem_limit_bytes=VMEM_LIMIT),
        name="dispatch",
    )(seg, pc, nvalid, slot_flat, h)


def _experts_kernel(te_ref, nv_ref, xs_ref, wg_ref, wu_ref, wd_ref, ys_ref, wgb_ref, wub_ref,
                    wdb_ref):
    j = pl.program_id(0)

    @pl.when(j < nv_ref[0])
    def _():
        prev = te_ref[jnp.maximum(j - 1, 0)]

        @pl.when((j == 0) | (te_ref[j] != prev))
        def _():
            wgb_ref[...] = wg_ref[0, 0].astype(BF16)
            wub_ref[...] = wu_ref[0, 0].astype(BF16)
            wdb_ref[...] = wd_ref[0, 0].astype(BF16)

        tg = xs_ref.shape[0] // ROW_TILE
        x1, x2 = (t.astype(BF16) for t in _load_row_tiles(xs_ref, 0, tg))
        gate = _dot(x1, wgb_ref[0:HALF, :]) + _dot(x2, wgb_ref[HALF:D_MODEL, :])
        up = _dot(x1, wub_ref[0:HALF, :]) + _dot(x2, wub_ref[HALF:D_MODEL, :])
        act = gate * (1.0 / (1.0 + jnp.exp(-gate))) * up
        _store_row_tiles(ys_ref, _dot(act.astype(BF16), wdb_ref[...]))

    @pl.when(j >= nv_ref[0])
    def _():
        ys_ref[...] = jnp.zeros_like(ys_ref)


def _experts_call(tile_e, nvalid, xs, wg, wu, wd, *, tg, layer):
    rows = xs.shape[0] // ROW_TILE
    last = lambda j, te, nv: jnp.minimum(j, nv[0] - 1)
    row_blk = pl.BlockSpec((tg * ROW_TILE, 128), lambda j, te, nv: (last(j, te, nv), 0))
    out_blk = pl.BlockSpec((tg * ROW_TILE, 128), lambda j, te, nv: (j, 0))
    w_blk = lambda a: pl.BlockSpec((1, 1) + a.shape[2:],
                                   lambda j, te, nv: (layer, te[last(j, te, nv)], 0, 0))
    grid_spec = pltpu.PrefetchScalarGridSpec(
        num_scalar_prefetch=2, grid=(rows // tg,),
        in_specs=[row_blk, w_blk(wg), w_blk(wu), w_blk(wd)],
        out_specs=out_blk,
        scratch_shapes=[pltpu.VMEM(wg.shape[2:], BF16), pltpu.VMEM(wu.shape[2:], BF16),
                        pltpu.VMEM(wd.shape[2:], BF16)])
    return pl.pallas_call(
        _experts_kernel,
        grid_spec=grid_spec,
        out_shape=jax.ShapeDtypeStruct((rows * ROW_TILE, 128), U32),
        compiler_params=pltpu.CompilerParams(
            dimension_semantics=("arbitrary",), vmem_limit_bytes=VMEM_LIMIT),
        name="experts",
    )(tile_e, nvalid, xs, wg, wu, wd)


def _combine_kernel(cur_ref, nxt_ref, base_ref, rw_ref, g2_ref, b2_ref, ys_hbm, y_ref, ybuf, sem,
                    *, tc):
    i = pl.program_id(0)
    n_steps = pl.num_programs(0)
    slot = i % 2

    def gathered(slot_):
        return pltpu.make_async_copy(ys_hbm.at[pl.ds(0, 2 * tc * ROW_TILE)], ybuf.at[slot_],
                                     sem.at[slot_])

    def issue(rows_ref, slot_):
        def body(g, c):
            t0 = g * DMA_BATCH
            src = [rows_ref[2 * t0 + u] for u in range(2 * DMA_BATCH)]
            for u in range(2 * DMA_BATCH):
                _row_copy(ys_hbm, src[u], ybuf.at[slot_], (u % 2) * tc + t0 + u // 2,
                          sem.at[slot_]).start()
            return c
        lax.fori_loop(0, tc // DMA_BATCH, body, 0)

    @pl.when(i == 0)
    def _():
        issue(cur_ref, 0)

    @pl.when(i + 1 < n_steps)
    def _():
        issue(nxt_ref, 1 - slot)

    gathered(slot).wait()
    y1a, y1b = _load_row_tiles(ybuf.at[slot], 0, tc)
    y2a, y2b = _load_row_tiles(ybuf.at[slot], tc, tc)
    rw = rw_ref[...]
    w1, w2 = rw[:, 0:1], rw[:, 1:2]
    moe = jnp.concatenate([w1 * y1a + w2 * y2a, w1 * y1b + w2 * y2b], axis=1)
    y_ref[...] = _layer_norm(base_ref[...] + moe, g2_ref[...], b2_ref[...])


def _combine_call(slot_flat, base, rw, g2, b2, ys, *, tc):
    n, d = base.shape
    n_steps = n // tc
    cur = pl.BlockSpec((2 * tc,), lambda i: (i,), memory_space=pltpu.SMEM)
    nxt = pl.BlockSpec((2 * tc,), lambda i: (jnp.minimum(i + 1, n_steps - 1),),
                       memory_space=pltpu.SMEM)
    tok = lambda w: pl.BlockSpec((tc, w), lambda i: (i, 0))
    full = lambda a: pl.BlockSpec(a.shape, lambda i: (0,) * a.ndim)
    return pl.pallas_call(
        functools.partial(_combine_kernel, tc=tc),
        grid=(n_steps,),
        in_specs=[cur, nxt, tok(d), tok(8), full(g2), full(b2), pl.BlockSpec(memory_space=pl.ANY)],
        out_specs=tok(d),
        out_shape=jax.ShapeDtypeStruct((n, d), F32),
        scratch_shapes=[pltpu.VMEM((2, 2 * tc * ROW_TILE, 128), U32),
                        pltpu.SemaphoreType.DMA((2,))],
        compiler_params=pltpu.CompilerParams(
            dimension_semantics=("arbitrary",), vmem_limit_bytes=VMEM_LIMIT),
        name="combine",
    )(slot_flat, slot_flat, base, rw, g2, b2, ys)


def _expert_segments(counts, tg, n_tiles):
    cnt = counts[0, N_GROUPS:N_GROUPS + N_EXPERTS].astype(jnp.int32)
    pc = ((cnt + tg - 1) // tg) * tg
    seg_end = jnp.cumsum(pc)
    seg = seg_end - pc
    nvalid = (seg_end[-1] // tg).reshape(1)
    tile_e = jnp.searchsorted(seg_end, jnp.arange(n_tiles, dtype=jnp.int32) * tg, side="right")
    tile_e = jnp.minimum(tile_e, N_EXPERTS - 1).astype(jnp.int32)
    return seg.astype(jnp.int32), pc.astype(jnp.int32), tile_e, nvalid.astype(jnp.int32)


def _cols(w, name):
    lo, hi = _OFF[name]
    return w[:, lo:hi]


def _rot_pairs(w):
    return jnp.concatenate([-w[..., 16:32], w[..., 0:16]], axis=-1)


def _layer_weights(w_in, w_uq, w_ukv, b_forget):
    wmain = jnp.concatenate([_cols(w_in, n) for n in ("ka", "cq", "ckv", "kc")], axis=1)
    kr = _cols(w_in, "kr")
    fc_pad = jnp.pad(_cols(w_in, "fc"), ((0, 0), (0, FC_ROWS - C_HEADS)))
    wt = jnp.concatenate([_cols(w_in, "qa"), _cols(w_in, "va"), _cols(w_in, "qc"), _cols(w_in, "vc"),
                          kr, _rot_pairs(kr), fc_pad], axis=1).T
    wq = w_uq.reshape(B_Q_RANK, B_HEADS, B_NOPE + B_ROPE)
    zpad = jnp.zeros((B_Q_RANK, B_HEADS, B_PAD - B_NOPE - B_ROPE), w_uq.dtype)
    plain = jnp.concatenate([wq, zpad], axis=-1).reshape(B_Q_RANK, B_HEADS * B_PAD)
    rot = jnp.concatenate([jnp.zeros_like(wq[..., :B_NOPE]), _rot_pairs(wq[..., B_NOPE:]), zpad],
                          axis=-1).reshape(B_Q_RANK, B_HEADS * B_PAD)
    wuq = jnp.concatenate([plain, rot], axis=1).T
    wkv = w_ukv.reshape(B_KV_RANK, B_HEADS, B_NOPE + B_VDIM)
    wukvk = wkv[..., :B_NOPE].reshape(B_KV_RANK, B_HEADS * B_NOPE).T
    wukvv = wkv[..., B_NOPE:].reshape(B_KV_RANK, B_HEADS * B_VDIM).T
    bfc = jnp.pad(b_forget, (0, FC_ROWS - C_HEADS)).reshape(FC_ROWS, 1)
    return (wmain.astype(BF16), wt.astype(BF16), wuq.astype(BF16), wukvk.astype(BF16),
            wukvv.astype(BF16), bfc.astype(F32))


def _rope_inputs(positions):
    half = B_ROPE // 2
    inv = ROPE_THETA ** (-jnp.arange(half, dtype=F32) / half)
    ang = positions.astype(F32)[..., None] * inv
    cos, sin = jnp.cos(ang), jnp.sin(ang)
    b, s, _ = cos.shape
    ones = jnp.ones((b, s, B_NOPE), F32)
    zeros = jnp.zeros((b, s, B_PAD - B_NOPE - B_ROPE), F32)
    cos128 = jnp.concatenate([ones, cos, cos, zeros], axis=-1)
    sin128 = jnp.concatenate([jnp.zeros_like(ones), sin, sin, zeros], axis=-1)
    cost = jnp.swapaxes(jnp.concatenate([cos, cos], axis=-1), 1, 2)
    sint = jnp.swapaxes(jnp.concatenate([sin, sin], axis=-1), 1, 2)
    return jnp.swapaxes(cos128, 1, 2), jnp.swapaxes(sin128, 1, 2), cost, sint


def kernel(x, p, positions, w_in, w_uq, w_ukv, g_cq, g_ckv, lam_q1, lam_k1, lam_q2, lam_k2, g_diff,
           b_forget, w_out, ln1_g, ln1_b, w_group, b_group, w_erouter, b_erouter, w_gate_e, w_up_e,
           w_down_e, w_ple_gate, w_ple_proj, ln2_g, ln2_b):
    b, s, d = x.shape
    n = b * s
    tq = 512
    tm_prep = 512
    tm_moe = 512
    tg = 512
    rows = 2 * n + N_EXPERTS * tg
    tb = min(2048, n // 2)
    tc = 512
    cos128, sin128, cost, sint = _rope_inputs(positions)
    posf = positions.astype(F32)
    pcol = posf[:, :, None]
    prow = posf[:, None, :]
    slopes = 2.0 ** (-8.0 * jnp.arange(1, A_HEADS + 1, dtype=F32) / A_HEADS)
    slopes = jnp.broadcast_to(
        jnp.pad(slopes.reshape(3, 2), ((0, 0), (0, 6)))[:, :, None], (3, 8, 128))
    row = lambda v: v.reshape(1, -1).astype(F32)

    for i in range(DEPTH):
        wmain, wt, wuq, wukvk, wukvv, bfc = _layer_weights(w_in[i], w_uq[i], w_ukv[i], b_forget[i])
        ka, kb, kc, qta, qtb, qtc, vta, vtb, vtc, cumt = _prep_call(
            x, wmain, wt, wuq, wukvk, wukvv, row(g_cq[i]), row(g_ckv[i]), bfc,
            cos128, sin128, cost, sint, tm=tm_prep)
        lamv = jnp.pad(jnp.stack([lam_q1[i], lam_k1[i], lam_q2[i], lam_k2[i]]).astype(F32),
                       ((0, 4), (0, 128 - A_DIM)))
        gcol = jnp.tile(g_diff[i].astype(F32), 2).reshape(128, 1)
        lam_init = 0.8 - 0.6 * math.exp(-0.3 * i)
        oa = _attn_a_call(qta, ka, vta, pcol, prow, slopes, lamv, gcol, tq=tq, lam_init=lam_init,
                          pairs=3)
        ob = _attn_b_call(qtb, kb, vtb, tq=tq, pairs=3)
        crow = cumt[:, :C_HEADS].reshape(b, 2, 2, s)
        ccol = jnp.swapaxes(crow, 2, 3)
        oc = _attn_c_call(qtc, kc, vtc, ccol, crow, tq=tq)

        wo = w_out[i].astype(BF16)
        wr = jnp.concatenate([w_group[i], jnp.moveaxis(w_erouter[i], 0, 1).reshape(d, N_EXPERTS)],
                             axis=1)
        wr = jnp.pad(wr, ((0, 0), (0, ROUTER_COLS - wr.shape[1]))).astype(F32)
        wr1 = wr.astype(BF16)
        wr2 = (wr - wr1.astype(F32)).astype(BF16)
        rbias = jnp.pad(jnp.concatenate([b_group[i], b_erouter[i].reshape(-1)]),
                        (0, ROUTER_COLS - N_GROUPS - N_EXPERTS)).reshape(1, ROUTER_COLS).astype(F32)
        h, base, ri, rw, counts = _post_call(
            x.reshape(n, d), oa.reshape(n, MIX_A), ob.reshape(n, MIX_B), oc.reshape(n, MIX_C),
            p[i].reshape(n, PLE_DIM), wo[:MIX_A], wo[MIX_A:MIX_A + MIX_B], wo[MIX_A + MIX_B:],
            row(ln1_g[i]), row(ln1_b[i]), wr1, jnp.concatenate([wr1, wr2], axis=1), rbias,
            w_ple_gate[i].astype(BF16),
            w_ple_proj[i].astype(BF16), tm=tm_moe)
        seg, pc, tile_e, nvalid = _expert_segments(counts, tg, rows // tg)
        seg_row = jnp.pad(seg.astype(F32), (0, ROUTER_COLS - N_EXPERTS)).reshape(1, ROUTER_COLS)
        slot_flat = _slot_call(ri, seg_row, tm=min(4096, n))[:, 0:2].reshape(-1)
        xs = _dispatch_call(seg, pc, nvalid, slot_flat, h, rows=rows, tb=tb, tg=tg)
        ys = _experts_call(tile_e, nvalid, xs, w_gate_e, w_up_e, w_down_e, tg=tg, layer=i)
        y = _combine_call(slot_flat, base, rw, row(ln2_g[i]), row(ln2_b[i]), ys, tc=tc)
        x = y.reshape(b, s, d)
    return x
```

```python
import functools
import math

import jax
import jax.numpy as jnp
from jax import lax
from jax.experimental import pallas as pl
from jax.experimental.pallas import tpu as pltpu

F32 = jnp.float32
BF16 = jnp.bfloat16

D_MODEL = 1024
DEPTH = 2
CHUNK = 64
PLE_DIM = 256

A_HEADS = 6
A_DIM = 32
A_VDIM = 64
B_HEADS = 6
B_Q_RANK = 256
B_KV_RANK = 128
B_NOPE = 64
B_ROPE = 32
B_VDIM = 64
ROPE_THETA = 10000.0
C_HEADS = 4
C_DIM = 64

MIX_A = A_HEADS * A_VDIM
MIX_B = B_HEADS * B_VDIM
MIX_C = C_HEADS * C_DIM
B_PAD = 128

N_GROUPS = 4
EXPERTS_PER_GROUP = 8
N_EXPERTS = N_GROUPS * EXPERTS_PER_GROUP
D_EXPERT = 256
ROUTER_COLS = 128

DEEPNORM_ALPHA = (2 * DEPTH) ** 0.25
LN_EPS = 1e-5
RMS_EPS = 1e-6
NEG_BIG = -1e30
LOG2E = math.log2(math.e)

_OFF = {}
_o = 0
for _name, _w in (("qa", 384), ("ka", 384), ("va", 384), ("cq", 256), ("ckv", 128), ("kr", 32),
                  ("qc", 256), ("kc", 256), ("vc", 256), ("fc", 4)):
    _OFF[_name] = (_o, _o + _w)
    _o += _w

T_QA, T_VA, T_QC, T_VC, T_KR, T_KRS, T_FC, T_ROWS = 0, 384, 768, 1024, 1280, 1312, 1344, 1360
FC_ROWS = T_ROWS - T_FC

VMEM_LIMIT = 48 * 1024 * 1024
DMA_BATCH = 4

NT_DIMS = (((1,), (1,)), ((), ()))


def _dot(a, b):
    return jnp.dot(a, b, preferred_element_type=F32)


def _dot_nt(a, b):
    return lax.dot_general(a, b, NT_DIMS, preferred_element_type=F32)


U32 = jnp.uint32
HALF = D_MODEL // 2
ROW_TILE = HALF // 128


def _pack_rows(v):
    hi = lax.bitcast_convert_type(v[:, :HALF].astype(BF16).astype(F32), U32)
    lo = lax.bitcast_convert_type(v[:, HALF:].astype(BF16).astype(F32), U32)
    return hi | lax.shift_right_logical(lo, jnp.full_like(lo, 16))


def _unpack_rows(w):
    first = lax.bitcast_convert_type(w & jnp.full_like(w, 0xFFFF0000), F32)
    second = lax.bitcast_convert_type(lax.shift_left(w, jnp.full_like(w, 16)), F32)
    return first, second


def _store_row_tiles(ref, v):
    rows = v.shape[0]
    w = _pack_rows(v)
    for c in range(ROW_TILE):
        ref[pl.ds(c, rows, stride=ROW_TILE), :] = w[:, c * 128:(c + 1) * 128]


def _load_row_tiles(ref, start, rows):
    w = jnp.concatenate(
        [ref[pl.ds(start * ROW_TILE + c, rows, stride=ROW_TILE), :] for c in range(ROW_TILE)],
        axis=1)
    return _unpack_rows(w)


def _split3(a):
    a1 = a.astype(BF16)
    r1 = a - a1.astype(F32)
    a2 = r1.astype(BF16)
    a3 = (r1 - a2.astype(F32)).astype(BF16)
    return a1, a2, a3


def _prep_kernel(x_ref, wmain_ref, wt_ref, wuq_ref, wukvk_ref, wukvv_ref, gcq_ref, gckv_ref,
                 bf_ref, cos_ref, sin_ref, cost_ref, sint_ref,
                 ka_ref, kb_ref, kc_ref, qta_ref, qtb_ref, qtc_ref, vta_ref, vtb_ref, vtc_ref,
                 cum_ref, carry_ref, *, tm):
    j = pl.program_id(1)

    @pl.when(j == 0)
    def _():
        carry_ref[...] = jnp.zeros_like(carry_ref)

    xb = x_ref[0].astype(BF16)
    z = _dot(xb, wmain_ref[...])
    zt = _dot_nt(wt_ref[...], xb)

    ka_ref[0] = z[:, 0:384].astype(BF16)
    qta_ref[0] = (zt[T_QA:T_QA + 384] * (LOG2E / math.sqrt(A_DIM))).astype(BF16)
    vta_ref[0] = zt[T_VA:T_VA + 384].astype(BF16)
    kc_ref[0] = z[:, 768:1024].astype(BF16)
    qtc_ref[0] = (zt[T_QC:T_QC + 256] * (LOG2E / math.sqrt(C_DIM))).astype(BF16)
    vtc_ref[0] = zt[T_VC:T_VC + 256].astype(BF16)

    fct = zt[T_FC:T_ROWS] + bf_ref[...]
    logf = jnp.minimum(fct, 0.0) - jnp.log(1.0 + jnp.exp(-jnp.abs(fct)))
    r_i = lax.broadcasted_iota(jnp.int32, (tm, tm), 0)
    c_i = lax.broadcasted_iota(jnp.int32, (tm, tm), 1)
    tri = jnp.where(r_i <= c_i, 1.0, 0.0).astype(BF16)
    l1, l2, l3 = _split3(logf)
    cum = _dot(l1, tri) + _dot(l2, tri) + _dot(l3, tri) + carry_ref[:, 0:1]
    cum_ref[0] = cum * LOG2E
    carry_ref[...] = jnp.broadcast_to(cum[:, tm - 1:tm], carry_ref.shape)

    cq = z[:, 384:640]
    cq_n = cq * lax.rsqrt(jnp.mean(cq * cq, axis=-1, keepdims=True) + RMS_EPS) * gcq_ref[...]
    zqt = _dot_nt(wuq_ref[...], cq_n.astype(BF16))
    cosb = cos_ref[0]
    sinb = sin_ref[0]
    scale_b = LOG2E / math.sqrt(B_NOPE + B_ROPE)
    for h in range(B_HEADS):
        lo = h * B_PAD
        qh = zqt[lo:lo + B_PAD] * cosb + zqt[768 + lo:768 + lo + B_PAD] * sinb
        qtb_ref[0, lo:lo + B_PAD, :] = (qh * scale_b).astype(BF16)

    ckv = z[:, 640:768]
    ckv_n = ckv * lax.rsqrt(jnp.mean(ckv * ckv, axis=-1, keepdims=True) + RMS_EPS) * gckv_ref[...]
    ckv_b = ckv_n.astype(BF16)
    vtb_ref[0] = _dot_nt(wukvv_ref[...], ckv_b).astype(BF16)
    knt = _dot_nt(wukvk_ref[...], ckv_b)
    krt = zt[T_KR:T_KR + 32] * cost_ref[0] + zt[T_KRS:T_KRS + 32] * sint_ref[0]
    zpad = jnp.zeros((B_PAD - B_NOPE - B_ROPE, tm), F32)
    pieces = []
    for h in range(B_HEADS):
        pieces += [knt[h * B_NOPE:(h + 1) * B_NOPE], krt, zpad]
    kb_ref[0] = jnp.concatenate(pieces, axis=0).T.astype(BF16)


def _prep_call(x, wmain, wt, wuq, wukvk, wukvv, gcq, gckv, bfc, cos128, sin128, cost, sint, *, tm):
    b, s, d = x.shape
    grid = (b, s // tm)
    full = lambda shape: pl.BlockSpec(shape, lambda bi, j: (0,) * len(shape))
    tok = lambda w: pl.BlockSpec((1, tm, w), lambda bi, j: (bi, j, 0))
    tr = lambda r: pl.BlockSpec((1, r, tm), lambda bi, j: (bi, 0, j))
    out_shape = (
        jax.ShapeDtypeStruct((b, s, 384), BF16),
        jax.ShapeDtypeStruct((b, s, 768), BF16),
        jax.ShapeDtypeStruct((b, s, 256), BF16),
        jax.ShapeDtypeStruct((b, 384, s), BF16),
        jax.ShapeDtypeStruct((b, 768, s), BF16),
        jax.ShapeDtypeStruct((b, 256, s), BF16),
        jax.ShapeDtypeStruct((b, 384, s), BF16),
        jax.ShapeDtypeStruct((b, 384, s), BF16),
        jax.ShapeDtypeStruct((b, 256, s), BF16),
        jax.ShapeDtypeStruct((b, FC_ROWS, s), F32),
    )
    out_specs = (tok(384), tok(768), tok(256), tr(384), tr(768), tr(256), tr(384), tr(384),
                 tr(256), tr(FC_ROWS))
    in_specs = [tok(d), full(wmain.shape), full(wt.shape), full(wuq.shape), full(wukvk.shape),
                full(wukvv.shape), full(gcq.shape), full(gckv.shape), full(bfc.shape),
                tr(128), tr(128), tr(32), tr(32)]
    return pl.pallas_call(
        functools.partial(_prep_kernel, tm=tm),
        grid=grid, in_specs=in_specs, out_specs=out_specs, out_shape=out_shape,
        scratch_shapes=[pltpu.VMEM((FC_ROWS, 128), F32)],
        compiler_params=pltpu.CompilerParams(
            dimension_semantics=("arbitrary", "arbitrary"), vmem_limit_bytes=VMEM_LIMIT),
        name="prep",
    )(x, wmain, wt, wuq, wukvk, wukvv, gcq, gckv, bfc, cos128, sin128, cost, sint)


def _flash(i, tq, n_maps, scores_fn, mask):
    def update(state, j, diag):
        new = []
        for (m, l, acc), (s, v) in zip(state, scores_fn(j)):
            if diag:
                s = jnp.where(mask, s, NEG_BIG)
            m_new = jnp.maximum(m, jnp.max(s, axis=1, keepdims=True))
            alpha = jnp.exp2(m - m_new)
            p = jnp.exp2(s - m_new)
            l_new = alpha * l + jnp.sum(p, axis=1, keepdims=True)
            acc_new = alpha * acc + _dot(p.astype(BF16), v)
            new.append((m_new, l_new, acc_new))
        return tuple(new)

    init = tuple((jnp.full((tq, 1), NEG_BIG, F32), jnp.zeros((tq, 1), F32),
                  jnp.zeros((tq, 128), F32)) for _ in range(n_maps))
    state = update(init, i, True)
    state = lax.fori_loop(0, i, lambda j, st: update(st, j, False), state)
    return [acc * (1.0 / l) for (_, l, acc) in state]


def _chunk_mask(tq):
    r = lax.broadcasted_iota(jnp.int32, (tq, tq), 0)
    c = lax.broadcasted_iota(jnp.int32, (tq, tq), 1)
    return (c // CHUNK) <= (r // CHUNK)


def _causal_mask(tq):
    r = lax.broadcasted_iota(jnp.int32, (tq, tq), 0)
    c = lax.broadcasted_iota(jnp.int32, (tq, tq), 1)
    return c <= r


ONES_ROWS = 16


def _flash_t(i, tq, n_maps, scores_fn, mask):
    ones = jnp.ones((ONES_ROWS, tq), BF16)

    def update(state, j, diag):
        new = []
        for (m, acc), (s, vt) in zip(state, scores_fn(j)):
            if diag:
                s = jnp.where(mask, s, NEG_BIG)
            m_new = jnp.maximum(m, jnp.max(s, axis=0, keepdims=True))
            alpha = jnp.exp2(m - m_new)
            p = jnp.exp2(s - m_new).astype(BF16)
            vext = jnp.concatenate([vt, ones], axis=0)
            new.append((m_new, alpha * acc + _dot(vext, p)))
        return tuple(new)

    init = tuple((jnp.full((1, tq), NEG_BIG, F32), jnp.zeros((128 + ONES_ROWS, tq), F32))
                 for _ in range(n_maps))
    state = update(init, i, True)
    state = lax.fori_loop(0, i, lambda j, st: update(st, j, False), state)
    return [acc[0:128] * (1.0 / acc[128:129]) for (_, acc) in state]


def _chunk_mask_t(tq):
    r = lax.broadcasted_iota(jnp.int32, (tq, tq), 0)
    c = lax.broadcasted_iota(jnp.int32, (tq, tq), 1)
    return (r // CHUNK) <= (c // CHUNK)


def _causal_mask_t(tq):
    r = lax.broadcasted_iota(jnp.int32, (tq, tq), 0)
    c = lax.broadcasted_iota(jnp.int32, (tq, tq), 1)
    return r <= c


def _attn_a_kernel(qt_ref, k_ref, vt_ref, pcol_ref, prow_ref, slope_ref, lam_ref, g_ref, o_ref,
                   *, tq, lam_init, pairs):
    i = pl.program_id(2)
    row = lax.broadcasted_iota(jnp.int32, (128, tq), 0)
    pt = prow_ref[0]
    mask = _chunk_mask_t(tq)
    lam_v = lam_ref[...]
    lam = (jnp.exp(jnp.sum(lam_v[0:1] * lam_v[1:2], axis=1, keepdims=True))
           - jnp.exp(jnp.sum(lam_v[2:3] * lam_v[3:4], axis=1, keepdims=True)) + lam_init)

    def koff(j):
        return pl.multiple_of(j * tq, tq)

    qms, nslopes = [], []
    for pr in range(pairs):
        qt = qt_ref[0, pr * 128:(pr + 1) * 128, :]
        qms += [jnp.where((row >= 32 * mi) & (row < 32 * mi + 32), qt, jnp.zeros_like(qt))
                for mi in range(4)]
        nslopes += [-LOG2E * slope_ref[pr, hh:hh + 1, 0:1] for hh in range(2)]

    def scores_fn(j):
        ps = pcol_ref[0, pl.ds(koff(j), tq), :]
        dist = jnp.abs(ps - pt)
        out = []
        for pr in range(pairs):
            k = k_ref[0, pl.ds(koff(j), tq), pr * 128:(pr + 1) * 128]
            vt = vt_ref[0, pr * 128:(pr + 1) * 128, pl.ds(koff(j), tq)]
            for hh in range(2):
                bias = nslopes[2 * pr + hh] * dist
                out += [(_dot(k, qms[4 * pr + 2 * hh + mm]) + bias, vt) for mm in range(2)]
        return out

    o = _flash_t(i, tq, 4 * pairs, scores_fn, mask)
    for pr in range(pairs):
        o0, o1, o2, o3 = o[4 * pr:4 * pr + 4]
        ot = jnp.where(row < 64, o0 - lam * o1, o2 - lam * o3)
        sq = ot * ot
        ms0 = jnp.sum(sq[0:64], axis=0, keepdims=True) * (1.0 / A_VDIM)
        ms1 = jnp.sum(sq[64:128], axis=0, keepdims=True) * (1.0 / A_VDIM)
        ms = jnp.where(row < 64, ms0, ms1)
        yt = ot * lax.rsqrt(ms + RMS_EPS) * g_ref[...] * (1.0 - lam_init)
        o_ref[0, :, pr * 128:(pr + 1) * 128] = yt.T.astype(BF16)


def _attn_b_kernel(qt_ref, k_ref, vt_ref, o_ref, *, tq, pairs):
    i = pl.program_id(2)
    row = lax.broadcasted_iota(jnp.int32, (128, tq), 0)
    mask = _chunk_mask_t(tq)
    heads = 2 * pairs

    def koff(j):
        return pl.multiple_of(j * tq, tq)

    qts = [qt_ref[0, h * B_PAD:(h + 1) * B_PAD, :] for h in range(heads)]

    def scores_fn(j):
        vts = [vt_ref[0, pr * 128:(pr + 1) * 128, pl.ds(koff(j), tq)] for pr in range(pairs)]
        return [(_dot(k_ref[0, pl.ds(koff(j), tq), h * B_PAD:(h + 1) * B_PAD], qts[h]), vts[h // 2])
                for h in range(heads)]

    outs = _flash_t(i, tq, heads, scores_fn, mask)
    for pr in range(pairs):
        o_ref[0, :, pr * 128:(pr + 1) * 128] = jnp.where(
            row < 64, outs[2 * pr], outs[2 * pr + 1]).T.astype(BF16)


def _attn_c_kernel(qt_ref, k_ref, vt_ref, ccol_ref, crow_ref, o_ref, *, tq):
    i = pl.program_id(2)
    row = lax.broadcasted_iota(jnp.int32, (128, tq), 0)
    mask = _causal_mask_t(tq)
    pairs = C_HEADS // 2

    def koff(j):
        return pl.multiple_of(j * tq, tq)

    qms = []
    for pr in range(pairs):
        qt = qt_ref[0, pr * 128:(pr + 1) * 128, :]
        qms += [jnp.where((row >= 64 * hh) & (row < 64 * hh + 64), qt, jnp.zeros_like(qt))
                for hh in range(2)]
    cts = [crow_ref[0, h // 2, h % 2:h % 2 + 1, :] for h in range(C_HEADS)]

    def scores_fn(j):
        out = []
        for pr in range(pairs):
            k = k_ref[0, pl.ds(koff(j), tq), pr * 128:(pr + 1) * 128]
            vt = vt_ref[0, pr * 128:(pr + 1) * 128, pl.ds(koff(j), tq)]
            for hh in range(2):
                cs = ccol_ref[0, pr, pl.ds(koff(j), tq), hh:hh + 1]
                out.append((_dot(k, qms[2 * pr + hh]) + (cts[2 * pr + hh] - cs), vt))
        return out

    outs = _flash_t(i, tq, C_HEADS, scores_fn, mask)
    for pr in range(pairs):
        o_ref[0, :, pr * 128:(pr + 1) * 128] = jnp.where(
            row < 64, outs[2 * pr], outs[2 * pr + 1]).T.astype(BF16)


def _attn_params():
    return pltpu.CompilerParams(
        dimension_semantics=("arbitrary", "arbitrary", "arbitrary"), vmem_limit_bytes=VMEM_LIMIT)


def _attn_a_call(qta, ka, vta, pcol, prow, slopes, lamv, gcol, *, tq, lam_init, pairs):
    b, s, _ = ka.shape
    grid = (b, A_HEADS // (2 * pairs), s // tq)
    w = pairs * 128
    return pl.pallas_call(
        functools.partial(_attn_a_kernel, tq=tq, lam_init=lam_init, pairs=pairs),
        grid=grid,
        in_specs=[
            pl.BlockSpec((1, w, tq), lambda bi, c, i: (bi, c, i)),
            pl.BlockSpec((1, s, w), lambda bi, c, i: (bi, 0, c)),
            pl.BlockSpec((1, w, s), lambda bi, c, i: (bi, c, 0)),
            pl.BlockSpec((1, s, 1), lambda bi, c, i: (bi, 0, 0)),
            pl.BlockSpec((1, 1, tq), lambda bi, c, i: (bi, 0, i)),
            pl.BlockSpec((pairs, 8, 128), lambda bi, c, i: (c, 0, 0)),
            pl.BlockSpec((8, 128), lambda bi, c, i: (0, 0)),
            pl.BlockSpec((128, 1), lambda bi, c, i: (0, 0)),
        ],
        out_specs=pl.BlockSpec((1, tq, w), lambda bi, c, i: (bi, i, c)),
        out_shape=jax.ShapeDtypeStruct((b, s, MIX_A), BF16),
        compiler_params=_attn_params(), name="attn_a",
    )(qta, ka, vta, pcol, prow, slopes, lamv, gcol)


def _attn_b_call(qtb, kb, vtb, *, tq, pairs):
    b, s, _ = kb.shape
    grid = (b, B_HEADS // (2 * pairs), s // tq)
    return pl.pallas_call(
        functools.partial(_attn_b_kernel, tq=tq, pairs=pairs),
        grid=grid,
        in_specs=[
            pl.BlockSpec((1, 2 * pairs * B_PAD, tq), lambda bi, c, i: (bi, c, i)),
            pl.BlockSpec((1, s, 2 * pairs * B_PAD), lambda bi, c, i: (bi, 0, c)),
            pl.BlockSpec((1, pairs * 128, s), lambda bi, c, i: (bi, c, 0)),
        ],
        out_specs=pl.BlockSpec((1, tq, pairs * 128), lambda bi, c, i: (bi, i, c)),
        out_shape=jax.ShapeDtypeStruct((b, s, MIX_B), BF16),
        compiler_params=_attn_params(), name="attn_b",
    )(qtb, kb, vtb)


def _attn_c_call(qtc, kc, vtc, ccol, crow, *, tq):
    b, s, _ = kc.shape
    grid = (b, 1, s // tq)
    return pl.pallas_call(
        functools.partial(_attn_c_kernel, tq=tq),
        grid=grid,
        in_specs=[
            pl.BlockSpec((1, MIX_C, tq), lambda bi, c, i: (bi, 0, i)),
            pl.BlockSpec((1, s, MIX_C), lambda bi, c, i: (bi, 0, 0)),
            pl.BlockSpec((1, MIX_C, s), lambda bi, c, i: (bi, 0, 0)),
            pl.BlockSpec((1, 2, s, 2), lambda bi, c, i: (bi, 0, 0, 0)),
            pl.BlockSpec((1, 2, 2, tq), lambda bi, c, i: (bi, 0, 0, i)),
        ],
        out_specs=pl.BlockSpec((1, tq, MIX_C), lambda bi, c, i: (bi, i, 0)),
        out_shape=jax.ShapeDtypeStruct((b, s, MIX_C), BF16),
        compiler_params=_attn_params(), name="attn_c",
    )(qtc, kc, vtc, ccol, crow)


def _layer_norm(v, g, b):
    mu = jnp.mean(v, axis=-1, keepdims=True)
    c = v - mu
    var = jnp.mean(c * c, axis=-1, keepdims=True)
    return c * lax.rsqrt(var + LN_EPS) * g + b


def _route(h, wr1_ref, wr2_ref, rbias_ref):
    tm = h.shape[0]
    h1 = h.astype(BF16)
    h2 = (h - h1.astype(F32)).astype(BF16)
    r12 = _dot(h1, wr2_ref[...])
    logits = (r12[:, :ROUTER_COLS] + r12[:, ROUTER_COLS:] + _dot(h2, wr1_ref[...])
              + rbias_ref[...])
    col = lax.broadcasted_iota(jnp.int32, (tm, ROUTER_COLS), 1)
    colf = col.astype(F32)
    big = float(ROUTER_COLS)
    gmask = col < N_GROUPS
    gmax = jnp.max(jnp.where(gmask, logits, NEG_BIG), axis=1, keepdims=True)
    g_idx = jnp.min(jnp.where(gmask & (logits == gmax), colf, big), axis=1, keepdims=True)
    g_w = 1.0 / jnp.sum(jnp.where(gmask, jnp.exp(logits - gmax), 0.0), axis=1, keepdims=True)
    lo = N_GROUPS + g_idx * EXPERTS_PER_GROUP
    emask = (colf >= lo) & (colf < lo + EXPERTS_PER_GROUP)
    top1 = jnp.max(jnp.where(emask, logits, NEG_BIG), axis=1, keepdims=True)
    idx1 = jnp.min(jnp.where(emask & (logits == top1), colf, big), axis=1, keepdims=True)
    emask2 = emask & (colf != idx1)
    top2 = jnp.max(jnp.where(emask2, logits, NEG_BIG), axis=1, keepdims=True)
    idx2 = jnp.min(jnp.where(emask2 & (logits == top2), colf, big), axis=1, keepdims=True)
    e2 = jnp.exp(top2 - top1)
    w_1 = g_w / (1.0 + e2)
    w_2 = g_w * e2 / (1.0 + e2)
    return colf, idx1, idx2, w_1, w_2


def _post_kernel(x_ref, oa_ref, ob_ref, oc_ref, p_ref, woa_ref, wob_ref, woc_ref, g1_ref, b1_ref,
                 wr1_ref, wr2_ref, rbias_ref, wpg_ref, wpp_ref,
                 h_ref, base_ref, ri_ref, rw_ref, cnt_ref, carry_ref):
    i = pl.program_id(0)
    tm = x_ref.shape[0]

    @pl.when(i == 0)
    def _():
        carry_ref[...] = jnp.zeros_like(carry_ref)

    mix = (_dot(oa_ref[...], woa_ref[...]) + _dot(ob_ref[...], wob_ref[...])
           + _dot(oc_ref[...], woc_ref[...]))
    h = _layer_norm(DEEPNORM_ALPHA * x_ref[...] + mix, g1_ref[...], b1_ref[...])
    _store_row_tiles(h_ref, h)
    gl = _dot(h.astype(BF16), wpg_ref[...])
    ple = (1.0 / (1.0 + jnp.exp(-gl))) * _dot(p_ref[...].astype(BF16), wpp_ref[...])
    base_ref[...] = DEEPNORM_ALPHA * h + ple

    colf, idx1, idx2, w_1, w_2 = _route(h, wr1_ref, wr2_ref, rbias_ref)
    hit1 = colf == idx1
    hit2 = colf == idx2
    onehot = jnp.where(hit1 | hit2, 1.0, 0.0)
    r_i = lax.broadcasted_iota(jnp.int32, (tm, tm), 0)
    c_i = lax.broadcasted_iota(jnp.int32, (tm, tm), 1)
    before = jnp.where(c_i < r_i, 1.0, 0.0).astype(BF16)
    seen = _dot(before, onehot.astype(BF16)) + carry_ref[0:1, :]
    rank1 = jnp.sum(jnp.where(hit1, seen, 0.0), axis=1, keepdims=True)
    rank2 = jnp.sum(jnp.where(hit2, seen, 0.0), axis=1, keepdims=True)
    total = carry_ref[0:1, :] + jnp.sum(onehot, axis=0, keepdims=True)
    carry_ref[...] = jnp.broadcast_to(total, carry_ref.shape)
    cnt_ref[...] = jnp.broadcast_to(total, cnt_ref.shape)

    lane8 = lax.broadcasted_iota(jnp.int32, (tm, 8), 1)
    ri = jnp.where(lane8 == 0, idx1 - N_GROUPS,
                   jnp.where(lane8 == 1, idx2 - N_GROUPS,
                             jnp.where(lane8 == 2, rank1, jnp.where(lane8 == 3, rank2, 0.0))))
    ri_ref[...] = ri.astype(jnp.int32)
    rw_ref[...] = jnp.where(lane8 == 0, w_1, jnp.where(lane8 == 1, w_2, 0.0))


def _post_call(x2, oa, ob, oc, p2, woa, wob, woc, g1, b1, wr1, wr2, rbias, wpg, wpp, *, tm):
    n, d = x2.shape
    tok = lambda w: pl.BlockSpec((tm, w), lambda t: (t, 0))
    full = lambda a: pl.BlockSpec(a.shape, lambda t: (0,) * a.ndim)
    return pl.pallas_call(
        _post_kernel,
        grid=(n // tm,),
        in_specs=[tok(d), tok(MIX_A), tok(MIX_B), tok(MIX_C), tok(PLE_DIM), full(woa), full(wob),
                  full(woc), full(g1), full(b1), full(wr1), full(wr2), full(rbias), full(wpg),
                  full(wpp)],
        out_specs=(pl.BlockSpec((tm * ROW_TILE, 128), lambda t: (t, 0)), tok(d), tok(8), tok(8),
                   pl.BlockSpec((8, ROUTER_COLS), lambda t: (0, 0))),
        out_shape=(jax.ShapeDtypeStruct((n * ROW_TILE, 128), U32), jax.ShapeDtypeStruct((n, d), F32),
                   jax.ShapeDtypeStruct((n, 8), jnp.int32), jax.ShapeDtypeStruct((n, 8), F32),
                   jax.ShapeDtypeStruct((8, ROUTER_COLS), F32)),
        scratch_shapes=[pltpu.VMEM((8, ROUTER_COLS), F32)],
        compiler_params=pltpu.CompilerParams(
            dimension_semantics=("arbitrary",), vmem_limit_bytes=VMEM_LIMIT),
        name="post",
    )(x2, oa, ob, oc, p2, woa, wob, woc, g1, b1, wr1, wr2, rbias, wpg, wpp)


def _row_copy(src_ref, src_row, dst_ref, dst_row, sem):
    src = pl.ds(pl.multiple_of(src_row * ROW_TILE, ROW_TILE), ROW_TILE)
    dst = pl.ds(pl.multiple_of(dst_row * ROW_TILE, ROW_TILE), ROW_TILE)
    return pltpu.make_async_copy(src_ref.at[src], dst_ref.at[dst], sem)


def _slot_kernel(ri_ref, seg_ref, slot_ref):
    tm = ri_ref.shape[0]
    ri = ri_ref[...]
    colf = lax.broadcasted_iota(jnp.int32, (tm, ROUTER_COLS), 1).astype(F32)
    seg = seg_ref[...]
    lane8 = lax.broadcasted_iota(jnp.int32, (tm, 8), 1)
    out = jnp.zeros((tm, 8), F32)
    for k in range(2):
        e = ri[:, k:k + 1].astype(F32)
        start = jnp.sum(jnp.where(colf == e, seg, 0.0), axis=1, keepdims=True)
        out = jnp.where(lane8 == k, start + ri[:, 2 + k:3 + k].astype(F32), out)
    slot_ref[...] = out.astype(jnp.int32)


def _slot_call(ri, seg_row, *, tm):
    n = ri.shape[0]
    return pl.pallas_call(
        _slot_kernel,
        grid=(n // tm,),
        in_specs=[pl.BlockSpec((tm, 8), lambda t: (t, 0)),
                  pl.BlockSpec((1, ROUTER_COLS), lambda t: (0, 0))],
        out_specs=pl.BlockSpec((tm, 8), lambda t: (t, 0)),
        out_shape=jax.ShapeDtypeStruct((n, 8), jnp.int32),
        compiler_params=pltpu.CompilerParams(dimension_semantics=("arbitrary",)),
        name="slots",
    )(ri, seg_row)


def _dispatch_kernel(seg_ref, pc_ref, nv_ref, slot_ref, h_ref, xs_hbm, zbuf, sem, zsem, *, tb, tg):
    i = pl.program_id(0)
    n_tiles = xs_hbm.shape[0] // (tg * ROW_TILE)

    def unused_tile_zero_copy(jt):
        start = pl.multiple_of(jt * (tg * ROW_TILE), tg * ROW_TILE)
        return pltpu.make_async_copy(zbuf, xs_hbm.at[pl.ds(start, tg * ROW_TILE)], zsem)

    def for_unused_tiles(fn):
        def body(jt, c):
            fn(unused_tile_zero_copy(jt))
            return c
        lax.fori_loop(nv_ref[0], n_tiles, body, 0)

    def tail_zero_copy(e):
        start = pl.multiple_of((seg_ref[e] + pc_ref[e] - tg) * ROW_TILE, tg * ROW_TILE)
        return pltpu.make_async_copy(zbuf, xs_hbm.at[pl.ds(start, tg * ROW_TILE)], zsem)

    @pl.when(i == 0)
    def _():
        zbuf[...] = jnp.zeros_like(zbuf)
        for e in range(N_EXPERTS):
            @pl.when(pc_ref[e] > 0)
            def _():
                tail_zero_copy(e).start()
        for e in range(N_EXPERTS):
            @pl.when(pc_ref[e] > 0)
            def _():
                tail_zero_copy(e).wait()
        for_unused_tiles(lambda cp: cp.start())
        for_unused_tiles(lambda cp: cp.wait())

    def body(g, c):
        t0 = g * DMA_BATCH
        dst = [slot_ref[2 * t0 + u] for u in range(2 * DMA_BATCH)]
        for u in range(2 * DMA_BATCH):
            _row_copy(h_ref, t0 + u // 2, xs_hbm, dst[u], sem).start()
        return c

    lax.fori_loop(0, tb // DMA_BATCH, body, 0)
    for _ in range(2):
        pltpu.make_async_copy(h_ref, xs_hbm.at[pl.ds(0, tb * ROW_TILE)], sem).wait()


def _dispatch_call(seg, pc, nvalid, slot_flat, h, *, rows, tb, tg):
    n = h.shape[0] // ROW_TILE
    smem_blk = pl.BlockSpec((2 * tb,), lambda i, seg, pc, nv: (i,), memory_space=pltpu.SMEM)
    grid_spec = pltpu.PrefetchScalarGridSpec(
        num_scalar_prefetch=3, grid=(n // tb,),
        in_specs=[smem_blk,
                  pl.BlockSpec((tb * ROW_TILE, 128), lambda i, seg, pc, nv: (i, 0))],
        out_specs=pl.BlockSpec(memory_space=pl.ANY),
        scratch_shapes=[pltpu.VMEM((tg * ROW_TILE, 128), U32), pltpu.SemaphoreType.DMA(()),
                        pltpu.SemaphoreType.DMA(())])
    return pl.pallas_call(
        functools.partial(_dispatch_kernel, tb=tb, tg=tg),
        grid_spec=grid_spec,
        out_shape=jax.ShapeDtypeStruct((rows * ROW_TILE, 128), U32),
        compiler_params=pltpu.CompilerParams(
            dimension_semantics=("arbitrary",), vmem_limit_bytes=VMEM_LIMIT),
        name="dispatch",
    )(seg, pc, nvalid, slot_flat, h)


def _experts_kernel(te_ref, nv_ref, xs_ref, wg_ref, wu_ref, wd_ref, ys_ref, wgb_ref, wub_ref,
                    wdb_ref):
    j = pl.program_id(0)

    @pl.when(j < nv_ref[0])
    def _():
        prev = te_ref[jnp.maximum(j - 1, 0)]

        @pl.when((j == 0) | (te_ref[j] != prev))
        def _():
            wgb_ref[...] = wg_ref[0, 0].astype(BF16)
            wub_ref[...] = wu_ref[0, 0].astype(BF16)
            wdb_ref[...] = wd_ref[0, 0].astype(BF16)

        tg = xs_ref.shape[0] // ROW_TILE
        x1, x2 = (t.astype(BF16) for t in _load_row_tiles(xs_ref, 0, tg))
        gate = _dot(x1, wgb_ref[0:HALF, :]) + _dot(x2, wgb_ref[HALF:D_MODEL, :])
        up = _dot(x1, wub_ref[0:HALF, :]) + _dot(x2, wub_ref[HALF:D_MODEL, :])
        act = gate * (1.0 / (1.0 + jnp.exp(-gate))) * up
        _store_row_tiles(ys_ref, _dot(act.astype(BF16), wdb_ref[...]))

    @pl.when(j >= nv_ref[0])
    def _():
        ys_ref[...] = jnp.zeros_like(ys_ref)


def _experts_call(tile_e, nvalid, xs, wg, wu, wd, *, tg, layer):
    rows = xs.shape[0] // ROW_TILE
    last = lambda j, te, nv: jnp.minimum(j, nv[0] - 1)
    row_blk = pl.BlockSpec((tg * ROW_TILE, 128), lambda j, te, nv: (last(j, te, nv), 0))
    out_blk = pl.BlockSpec((tg * ROW_TILE, 128), lambda j, te, nv: (j, 0))
    w_blk = lambda a: pl.BlockSpec((1, 1) + a.shape[2:],
                                   lambda j, te, nv: (layer, te[last(j, te, nv)], 0, 0))
    grid_spec = pltpu.PrefetchScalarGridSpec(
        num_scalar_prefetch=2, grid=(rows // tg,),
        in_specs=[row_blk, w_blk(wg), w_blk(wu), w_blk(wd)],
        out_specs=out_blk,
        scratch_shapes=[pltpu.VMEM(wg.shape[2:], BF16), pltpu.VMEM(wu.shape[2:], BF16),
                        pltpu.VMEM(wd.shape[2:], BF16)])
    return pl.pallas_call(
        _experts_kernel,
        grid_spec=grid_spec,
        out_shape=jax.ShapeDtypeStruct((rows * ROW_TILE, 128), U32),
        compiler_params=pltpu.CompilerParams(
            dimension_semantics=("arbitrary",), vmem_limit_bytes=VMEM_LIMIT),
        name="experts",
    )(tile_e, nvalid, xs, wg, wu, wd)


def _combine_kernel(cur_ref, nxt_ref, base_ref, rw_ref, g2_ref, b2_ref, ys_hbm, y_ref, ybuf, sem,
                    *, tc):
    i = pl.program_id(0)
    n_steps = pl.num_programs(0)
    slot = i % 2

    def gathered(slot_):
        return pltpu.make_async_copy(ys_hbm.at[pl.ds(0, 2 * tc * ROW_TILE)], ybuf.at[slot_],
                                     sem.at[slot_])

    def issue(rows_ref, slot_):
        def body(g, c):
            t0 = g * DMA_BATCH
            src = [rows_ref[2 * t0 + u] for u in range(2 * DMA_BATCH)]
            for u in range(2 * DMA_BATCH):
                _row_copy(ys_hbm, src[u], ybuf.at[slot_], (u % 2) * tc + t0 + u // 2,
                          sem.at[slot_]).start()
            return c
        lax.fori_loop(0, tc // DMA_BATCH, body, 0)

    @pl.when(i == 0)
    def _():
        issue(cur_ref, 0)

    @pl.when(i + 1 < n_steps)
    def _():
        issue(nxt_ref, 1 - slot)

    gathered(slot).wait()
    y1a, y1b = _load_row_tiles(ybuf.at[slot], 0, tc)
    y2a, y2b = _load_row_tiles(ybuf.at[slot], tc, tc)
    rw = rw_ref[...]
    w1, w2 = rw[:, 0:1], rw[:, 1:2]
    moe = jnp.concatenate([w1 * y1a + w2 * y2a, w1 * y1b + w2 * y2b], axis=1)
    y_ref[...] = _layer_norm(base_ref[...] + moe, g2_ref[...], b2_ref[...])


def _combine_call(slot_flat, base, rw, g2, b2, ys, *, tc):
    n, d = base.shape
    n_steps = n // tc
    cur = pl.BlockSpec((2 * tc,), lambda i: (i,), memory_space=pltpu.SMEM)
    nxt = pl.BlockSpec((2 * tc,), lambda i: (jnp.minimum(i + 1, n_steps - 1),),
                       memory_space=pltpu.SMEM)
    tok = lambda w: pl.BlockSpec((tc, w), lambda i: (i, 0))
    full = lambda a: pl.BlockSpec(a.shape, lambda i: (0,) * a.ndim)
    return pl.pallas_call(
        functools.partial(_combine_kernel, tc=tc),
        grid=(n_steps,),
        in_specs=[cur, nxt, tok(d), tok(8), full(g2), full(b2), pl.BlockSpec(memory_space=pl.ANY)],
        out_specs=tok(d),
        out_shape=jax.ShapeDtypeStruct((n, d), F32),
        scratch_shapes=[pltpu.VMEM((2, 2 * tc * ROW_TILE, 128), U32),
                        pltpu.SemaphoreType.DMA((2,))],
        compiler_params=pltpu.CompilerParams(
            dimension_semantics=("arbitrary",), vmem_limit_bytes=VMEM_LIMIT),
        name="combine",
    )(slot_flat, slot_flat, base, rw, g2, b2, ys)


def _expert_segments(counts, tg, n_tiles):
    cnt = counts[0, N_GROUPS:N_GROUPS + N_EXPERTS].astype(jnp.int32)
    pc = ((cnt + tg - 1) // tg) * tg
    seg_end = jnp.cumsum(pc)
    seg = seg_end - pc
    nvalid = (seg_end[-1] // tg).reshape(1)
    tile_e = jnp.searchsorted(seg_end, jnp.arange(n_tiles, dtype=jnp.int32) * tg, side="right")
    tile_e = jnp.minimum(tile_e, N_EXPERTS - 1).astype(jnp.int32)
    return seg.astype(jnp.int32), pc.astype(jnp.int32), tile_e, nvalid.astype(jnp.int32)


def _cols(w, name):
    lo, hi = _OFF[name]
    return w[:, lo:hi]


def _rot_pairs(w):
    return jnp.concatenate([-w[..., 16:32], w[..., 0:16]], axis=-1)


def _layer_weights(w_in, w_uq, w_ukv, b_forget):
    wmain = jnp.concatenate([_cols(w_in, n) for n in ("ka", "cq", "ckv", "kc")], axis=1)
    kr = _cols(w_in, "kr")
    fc_pad = jnp.pad(_cols(w_in, "fc"), ((0, 0), (0, FC_ROWS - C_HEADS)))
    wt = jnp.concatenate([_cols(w_in, "qa"), _cols(w_in, "va"), _cols(w_in, "qc"), _cols(w_in, "vc"),
                          kr, _rot_pairs(kr), fc_pad], axis=1).T
    wq = w_uq.reshape(B_Q_RANK, B_HEADS, B_NOPE + B_ROPE)
    zpad = jnp.zeros((B_Q_RANK, B_HEADS, B_PAD - B_NOPE - B_ROPE), w_uq.dtype)
    plain = jnp.concatenate([wq, zpad], axis=-1).reshape(B_Q_RANK, B_HEADS * B_PAD)
    rot = jnp.concatenate([jnp.zeros_like(wq[..., :B_NOPE]), _rot_pairs(wq[..., B_NOPE:]), zpad],
                          axis=-1).reshape(B_Q_RANK, B_HEADS * B_PAD)
    wuq = jnp.concatenate([plain, rot], axis=1).T
    wkv = w_ukv.reshape(B_KV_RANK, B_HEADS, B_NOPE + B_VDIM)
    wukvk = wkv[..., :B_NOPE].reshape(B_KV_RANK, B_HEADS * B_NOPE).T
    wukvv = wkv[..., B_NOPE:].reshape(B_KV_RANK, B_HEADS * B_VDIM).T
    bfc = jnp.pad(b_forget, (0, FC_ROWS - C_HEADS)).reshape(FC_ROWS, 1)
    return (wmain.astype(BF16), wt.astype(BF16), wuq.astype(BF16), wukvk.astype(BF16),
            wukvv.astype(BF16), bfc.astype(F32))


def _rope_inputs(positions):
    half = B_ROPE // 2
    inv = ROPE_THETA ** (-jnp.arange(half, dtype=F32) / half)
    ang = positions.astype(F32)[..., None] * inv
    cos, sin = jnp.cos(ang), jnp.sin(ang)
    b, s, _ = cos.shape
    ones = jnp.ones((b, s, B_NOPE), F32)
    zeros = jnp.zeros((b, s, B_PAD - B_NOPE - B_ROPE), F32)
    cos128 = jnp.concatenate([ones, cos, cos, zeros], axis=-1)
    sin128 = jnp.concatenate([jnp.zeros_like(ones), sin, sin, zeros], axis=-1)
    cost = jnp.swapaxes(jnp.concatenate([cos, cos], axis=-1), 1, 2)
    sint = jnp.swapaxes(jnp.concatenate([sin, sin], axis=-1), 1, 2)
    return jnp.swapaxes(cos128, 1, 2), jnp.swapaxes(sin128, 1, 2), cost, sint


def kernel(x, p, positions, w_in, w_uq, w_ukv, g_cq, g_ckv, lam_q1, lam_k1, lam_q2, lam_k2, g_diff,
           b_forget, w_out, ln1_g, ln1_b, w_group, b_group, w_erouter, b_erouter, w_gate_e, w_up_e,
           w_down_e, w_ple_gate, w_ple_proj, ln2_g, ln2_b):
    b, s, d = x.shape
    n = b * s
    tq = 512
    tm_prep = 512
    tm_moe = 1024
    tg = 512
    rows = 2 * n + N_EXPERTS * tg
    tb = min(2048, n // 2)
    tc = 512
    cos128, sin128, cost, sint = _rope_inputs(positions)
    posf = positions.astype(F32)
    pcol = posf[:, :, None]
    prow = posf[:, None, :]
    slopes = 2.0 ** (-8.0 * jnp.arange(1, A_HEADS + 1, dtype=F32) / A_HEADS)
    slopes = jnp.broadcast_to(
        jnp.pad(slopes.reshape(3, 2), ((0, 0), (0, 6)))[:, :, None], (3, 8, 128))
    row = lambda v: v.reshape(1, -1).astype(F32)

    for i in range(DEPTH):
        wmain, wt, wuq, wukvk, wukvv, bfc = _layer_weights(w_in[i], w_uq[i], w_ukv[i], b_forget[i])
        ka, kb, kc, qta, qtb, qtc, vta, vtb, vtc, cumt = _prep_call(
            x, wmain, wt, wuq, wukvk, wukvv, row(g_cq[i]), row(g_ckv[i]), bfc,
            cos128, sin128, cost, sint, tm=tm_prep)
        lamv = jnp.pad(jnp.stack([lam_q1[i], lam_k1[i], lam_q2[i], lam_k2[i]]).astype(F32),
                       ((0, 4), (0, 128 - A_DIM)))
        gcol = jnp.tile(g_diff[i].astype(F32), 2).reshape(128, 1)
        lam_init = 0.8 - 0.6 * math.exp(-0.3 * i)
        oa = _attn_a_call(qta, ka, vta, pcol, prow, slopes, lamv, gcol, tq=tq, lam_init=lam_init,
                          pairs=3)
        ob = _attn_b_call(qtb, kb, vtb, tq=tq, pairs=3)
        crow = cumt[:, :C_HEADS].reshape(b, 2, 2, s)
        ccol = jnp.swapaxes(crow, 2, 3)
        oc = _attn_c_call(qtc, kc, vtc, ccol, crow, tq=tq)

        wo = w_out[i].astype(BF16)
        wr = jnp.concatenate([w_group[i], jnp.moveaxis(w_erouter[i], 0, 1).reshape(d, N_EXPERTS)],
                             axis=1)
        wr = jnp.pad(wr, ((0, 0), (0, ROUTER_COLS - wr.shape[1]))).astype(F32)
        wr1 = wr.astype(BF16)
        wr2 = (wr - wr1.astype(F32)).astype(BF16)
        rbias = jnp.pad(jnp.concatenate([b_group[i], b_erouter[i].reshape(-1)]),
                        (0, ROUTER_COLS - N_GROUPS - N_EXPERTS)).reshape(1, ROUTER_COLS).astype(F32)
        h, base, ri, rw, counts = _post_call(
            x.reshape(n, d), oa.reshape(n, MIX_A), ob.reshape(n, MIX_B), oc.reshape(n, MIX_C),
            p[i].reshape(n, PLE_DIM), wo[:MIX_A], wo[MIX_A:MIX_A + MIX_B], wo[MIX_A + MIX_B:],
            row(ln1_g[i]), row(ln1_b[i]), wr1, jnp.concatenate([wr1, wr2], axis=1), rbias,
            w_ple_gate[i].astype(BF16),
            w_ple_proj[i].astype(BF16), tm=tm_moe)
        seg, pc, tile_e, nvalid = _expert_segments(counts, tg, rows // tg)
        seg_row = jnp.pad(seg.astype(F32), (0, ROUTER_COLS - N_EXPERTS)).reshape(1, ROUTER_COLS)
        slot_flat = _slot_call(ri, seg_row, tm=min(4096, n))[:, 0:2].reshape(-1)
        xs = _dispatch_call(seg, pc, nvalid, slot_flat, h, rows=rows, tb=tb, tg=tg)
        ys = _experts_call(tile_e, nvalid, xs, w_gate_e, w_up_e, w_down_e, tg=tg, layer=i)
        y = _combine_call(slot_flat, base, rw, row(ln2_g[i]), row(ln2_b[i]), ys, tc=tc)
        x = y.reshape(b, s, d)
    return x
```
